```python
import math
import jax, jax.numpy as jnp
from jax import lax
import numpy as np


D_MODEL = 1024
BATCH = 16
SEQ = 2048
DEPTH = 2
DEC_BATCH = 16
DEC_SEQ = 16
PAST_LEN = 4096

CHUNK = 64
N_A = max(1, DEPTH // 2)
N_B = DEPTH - N_A
EPS = 1e-6

D_FF = 2816
PLE_DIM = 256

SSM_EXPAND = 2
D_INNER = SSM_EXPAND * D_MODEL
SSM_HEAD_DIM = 64
SSM_HEADS = D_INNER // SSM_HEAD_DIM
SSM_GROUPS = 4
SSM_HPG = SSM_HEADS // SSM_GROUPS
D_STATE = 128
CONV_W = 4
GN = SSM_GROUPS * D_STATE
CONV_DIM = D_INNER + 2 * GN
IN_A = D_INNER + CONV_DIM + SSM_HEADS
SSD_CHUNK = CHUNK

MLA_HEADS = 16
Q_LORA = 512
KV_LORA = 256
QK_NOPE = 64
QK_ROPE = 32
V_HEAD = 64
ROPE_BASE = 10000.0
ATTN_SCALE = (QK_NOPE + QK_ROPE) ** -0.5
Q_BLOCK = 128

kernel_name = 'yoco_ssd_mla_macaron_stream_step'


def rmsnorm(x, g):
    xf = x.astype(jnp.float32)
    y = xf * lax.rsqrt(jnp.mean(xf * xf, axis=-1, keepdims=True) + EPS)
    return (y * g.astype(jnp.float32)).astype(x.dtype)


def swiglu(u, wg, wu, wd):
    return (jax.nn.silu(u @ wg) * (u @ wu)) @ wd


def rope(x, pos):
    half = QK_ROPE // 2
    inv = ROPE_BASE ** (-jnp.arange(half, dtype=jnp.float32) / half)
    ang = pos.astype(jnp.float32)[:, None] * inv[None, :]
    shape = (pos.shape[0],) + (1,) * (x.ndim - 3) + (half,)
    cos = jnp.cos(ang).reshape(shape)
    sin = jnp.sin(ang).reshape(shape)
    xf = x.astype(jnp.float32)
    x1, x2 = xf[..., :half], xf[..., half:]
    return jnp.concatenate([x1 * cos - x2 * sin, x1 * sin + x2 * cos], axis=-1).astype(x.dtype)


def causal_conv(u, buf, w, b):
    T = u.shape[1]
    up = jnp.concatenate([buf.astype(u.dtype), u], axis=1)
    y = b + sum(up[:, k:k + T] * w[k] for k in range(CONV_W))
    return y, up[:, -(CONV_W - 1):]


def ssd_scan(x, dt, A, Bm, Cm, s0):
    b, T = x.shape[0], x.shape[1]
    L = min(SSD_CHUNK, T)
    nc = T // L
    f32 = jnp.float32
    xdt = (x.astype(f32) * dt[..., None]).reshape(b, nc, L, SSM_GROUPS, SSM_HPG, SSM_HEAD_DIM)
    a = (dt * A).reshape(b, nc, L, SSM_GROUPS, SSM_HPG)
    Bc = Bm.astype(f32).reshape(b, nc, L, SSM_GROUPS, D_STATE)
    Cc = Cm.astype(f32).reshape(b, nc, L, SSM_GROUPS, D_STATE)
    causal = jnp.tril(jnp.ones((L, L), dtype=bool))

    def step(s, inp):
        xk, ak, bk, ck = inp
        acs = jnp.cumsum(ak, axis=1)
        at = jnp.moveaxis(acs, 1, -1)
        seg = jnp.exp(jnp.where(causal, at[..., :, None] - at[..., None, :], -jnp.inf))
        cb = jnp.einsum('blgn,bsgn->bgls', ck, bk)
        y_diag = jnp.einsum('bghls,bsghp->blghp', cb[:, :, None] * seg, xk)
        y_off = jnp.einsum('blgn,bghpn->blghp', ck, s) * jnp.exp(acs)[..., None]
        to_end = jnp.exp(acs[:, -1:] - acs)
        s_new = s * jnp.exp(acs[:, -1])[..., None, None] + jnp.einsum('blgn,blghp->bghpn', bk, xk * to_end[..., None])
        return s_new, y_diag + y_off

    sw = lambda t: jnp.swapaxes(t, 0, 1)
    s_init = s0.astype(f32).reshape(b, SSM_GROUPS, SSM_HPG, SSM_HEAD_DIM, D_STATE)
    s_fin, y = lax.scan(step, s_init, (sw(xdt), sw(a), sw(Bc), sw(Cc)))
    y = sw(y).reshape(b, T, SSM_HEADS, SSM_HEAD_DIM)
    return y, s_fin.reshape(b, SSM_HEADS, SSM_HEAD_DIM, D_STATE)


def mamba2_mixer(u, conv_buf, ssm_state, w_in, conv_w, conv_b, dt_bias, A_log, Dp, norm_g, w_out):
    b, T, _ = u.shape
    zxbcdt = u @ w_in
    z = zxbcdt[..., :D_INNER]
    xbc = zxbcdt[..., D_INNER:D_INNER + CONV_DIM]
    dt_raw = zxbcdt[..., D_INNER + CONV_DIM:]
    xbc, new_buf = causal_conv(xbc, conv_buf, conv_w, conv_b)
    xbc = jax.nn.silu(xbc)
    xs = xbc[..., :D_INNER].reshape(b, T, SSM_HEADS, SSM_HEAD_DIM)
    Bm = xbc[..., D_INNER:D_INNER + GN].reshape(b, T, SSM_GROUPS, D_STATE)
    Cm = xbc[..., D_INNER + GN:].reshape(b, T, SSM_GROUPS, D_STATE)
    dt = jax.nn.softplus(dt_raw.astype(jnp.float32) + dt_bias.astype(jnp.float32))
    A = -jnp.exp(A_log.astype(jnp.float32))
    y, s_new = ssd_scan(xs, dt, A, Bm, Cm, ssm_state)
    y = y + Dp.astype(jnp.float32)[:, None] * xs.astype(jnp.float32)
    y = y.reshape(b, T, D_INNER).astype(u.dtype)
    y = rmsnorm(y * jax.nn.silu(z), norm_g)
    return y @ w_out, new_buf, s_new


def mla_kv_side(h, pos, ln_kv, w_kv_a, kv_norm, kr_norm):
    kv = rmsnorm(h, ln_kv) @ w_kv_a
    c_kv = rmsnorm(kv[..., :KV_LORA], kv_norm)
    k_rope = rope(rmsnorm(kv[..., KV_LORA:], kr_norm), pos)
    return c_kv, k_rope


def mla_expand(c_kv, w_kv_b, k_norm):
    b, S, _ = c_kv.shape
    kvb = (c_kv @ w_kv_b).reshape(b, S, MLA_HEADS, QK_NOPE + V_HEAD)
    k_nope = rmsnorm(kvb[..., :QK_NOPE], k_norm)
    return k_nope, kvb[..., QK_NOPE:]


def mla_queries(u, pos, w_dq, q_norm, w_q_b, qn_nope, qn_rope):
    b, T, _ = u.shape
    cq = rmsnorm(u @ w_dq, q_norm)
    q = (cq @ w_q_b).reshape(b, T, MLA_HEADS, QK_NOPE + QK_ROPE)
    q_nope = rmsnorm(q[..., :QK_NOPE], qn_nope)
    q_rope = rope(rmsnorm(q[..., QK_NOPE:], qn_rope), pos)
    return q_nope, q_rope


def attend(q_nope, q_rope, k_nope, k_rope, v, q_pos, k_pos):
    s = (jnp.einsum('bqhd,bkhd->bhqk', q_nope, k_nope)
         + jnp.einsum('bqhr,bkr->bhqk', q_rope, k_rope)).astype(jnp.float32) * ATTN_SCALE
    mask = (k_pos // CHUNK)[None, :] <= (q_pos // CHUNK)[:, None]
    s = jnp.where(mask[None, None], s, -jnp.inf)
    p = jax.nn.softmax(s, axis=-1).astype(v.dtype)
    return jnp.einsum('bhqk,bkhd->bqhd', p, v)


def attend_blocked(q_nope, q_rope, k_nope, k_rope, v, pos):
    b, T = q_nope.shape[0], q_nope.shape[1]
    nb = T // Q_BLOCK
    qn = jnp.swapaxes(q_nope.reshape(b, nb, Q_BLOCK, MLA_HEADS, QK_NOPE), 0, 1)
    qr = jnp.swapaxes(q_rope.reshape(b, nb, Q_BLOCK, MLA_HEADS, QK_ROPE), 0, 1)
    qp = pos.reshape(nb, Q_BLOCK)
    out = lax.map(lambda a: attend(a[0], a[1], k_nope, k_rope, v, a[2], pos), (qn, qr, qp))
    return jnp.swapaxes(out, 0, 1).reshape(b, T, MLA_HEADS, V_HEAD)


def setup_inputs(seed: int = 0) -> dict:
    key = jax.random.key(seed)
    ks = iter(jax.random.split(key, 64))
    f32 = jnp.float32

    def nrm(shape, fan_in):
        return jax.random.normal(next(ks), shape, f32) * fan_in ** -0.5

    def gain(shape):
        return 1.0 + 0.05 * jax.random.normal(next(ks), shape, f32)

    x_prompt = jax.random.normal(next(ks), (BATCH, SEQ, D_MODEL), f32)
    x_sample = jax.random.normal(next(ks), (DEC_BATCH, DEC_SEQ, D_MODEL), f32)
    p_prompt = jax.random.normal(next(ks), (DEPTH, BATCH, SEQ, PLE_DIM), f32)
    p_sample = jax.random.normal(next(ks), (DEPTH, DEC_BATCH, DEC_SEQ, PLE_DIM), f32)
    state_conv = jax.random.normal(next(ks), (N_A, DEC_BATCH, CONV_W - 1, CONV_DIM), f32)
    state_ssm = 0.1 * jax.random.normal(next(ks), (N_A, DEC_BATCH, SSM_HEADS, SSM_HEAD_DIM, D_STATE), f32)
    cache_kv_latent = jax.random.normal(next(ks), (DEC_BATCH, PAST_LEN, KV_LORA), f32)
    cache_k_rope = jax.random.normal(next(ks), (DEC_BATCH, PAST_LEN, QK_ROPE), f32)

    ln_ffn1 = gain((DEPTH, D_MODEL))
    w1_gate = nrm((DEPTH, D_MODEL, D_FF), D_MODEL)
    w1_up = nrm((DEPTH, D_MODEL, D_FF), D_MODEL)
    w1_down = nrm((DEPTH, D_FF, D_MODEL), D_FF)
    ln_mix = gain((DEPTH, D_MODEL))
    ln_ffn2 = gain((DEPTH, D_MODEL))
    w2_gate = nrm((DEPTH, D_MODEL, D_FF), D_MODEL)
    w2_up = nrm((DEPTH, D_MODEL, D_FF), D_MODEL)
    w2_down = nrm((DEPTH, D_FF, D_MODEL), D_FF)
    ln_ple = gain((DEPTH, D_MODEL))
    w_ple_in = nrm((DEPTH, PLE_DIM, D_MODEL), PLE_DIM)
    w_ple_gate = nrm((DEPTH, D_MODEL, D_MODEL), D_MODEL)

    a_in_proj = nrm((N_A, D_MODEL, IN_A), D_MODEL)
    a_conv_w = nrm((N_A, CONV_W, CONV_DIM), CONV_W)
    a_conv_b = 0.01 * jax.random.normal(next(ks), (N_A, CONV_DIM), f32)
    dt0 = jnp.exp(jax.random.uniform(next(ks), (N_A, SSM_HEADS), f32, math.log(1e-3), math.log(1e-1)))
    a_dt_bias = dt0 + jnp.log(-jnp.expm1(-dt0))
    a_A_log = jnp.log(jax.random.uniform(next(ks), (N_A, SSM_HEADS), f32, 1.0, 16.0))
    a_D = 1.0 + 0.1 * jax.random.normal(next(ks), (N_A, SSM_HEADS), f32)
    a_norm = gain((N_A, D_INNER))
    a_out_proj = nrm((N_A, D_INNER, D_MODEL), D_INNER)

    ln_kv = gain((D_MODEL,))
    w_kv_a = nrm((D_MODEL, KV_LORA + QK_ROPE), D_MODEL)
    kv_norm = gain((KV_LORA,))
    kr_norm = gain((QK_ROPE,))
    w_kv_b = nrm((KV_LORA, MLA_HEADS * (QK_NOPE + V_HEAD)), KV_LORA)
    k_norm = gain((QK_NOPE,))

    b_w_dq = nrm((N_B, D_MODEL, Q_LORA), D_MODEL)
    b_q_norm = gain((N_B, Q_LORA))
    b_w_q_b = nrm((N_B, Q_LORA, MLA_HEADS * (QK_NOPE + QK_ROPE)), Q_LORA)
    b_qn_nope = gain((N_B, QK_NOPE))
    b_qn_rope = gain((N_B, QK_ROPE))
    b_w_o = nrm((N_B, MLA_HEADS * V_HEAD, D_MODEL), MLA_HEADS * V_HEAD)

    return {'x_prompt': x_prompt, 'x_sample': x_sample, 'p_prompt': p_prompt, 'p_sample': p_sample,
            'state_conv': state_conv, 'state_ssm': state_ssm,
            'cache_kv_latent': cache_kv_latent, 'cache_k_rope': cache_k_rope,
            'ln_ffn1': ln_ffn1, 'w1_gate': w1_gate, 'w1_up': w1_up, 'w1_down': w1_down,
            'ln_mix': ln_mix, 'ln_ffn2': ln_ffn2, 'w2_gate': w2_gate, 'w2_up': w2_up, 'w2_down': w2_down,
            'ln_ple': ln_ple, 'w_ple_in': w_ple_in, 'w_ple_gate': w_ple_gate,
            'a_in_proj': a_in_proj, 'a_conv_w': a_conv_w, 'a_conv_b': a_conv_b, 'a_dt_bias': a_dt_bias,
            'a_A_log': a_A_log, 'a_D': a_D, 'a_norm': a_norm, 'a_out_proj': a_out_proj,
            'ln_kv': ln_kv, 'w_kv_a': w_kv_a, 'kv_norm': kv_norm, 'kr_norm': kr_norm,
            'w_kv_b': w_kv_b, 'k_norm': k_norm,
            'b_w_dq': b_w_dq, 'b_q_norm': b_q_norm, 'b_w_q_b': b_w_q_b,
            'b_qn_nope': b_qn_nope, 'b_qn_rope': b_qn_rope, 'b_w_o': b_w_o}


def reference(x_prompt, x_sample, p_prompt, p_sample, state_conv, state_ssm, cache_kv_latent, cache_k_rope,
              ln_ffn1, w1_gate, w1_up, w1_down, ln_mix, ln_ffn2, w2_gate, w2_up, w2_down,
              ln_ple, w_ple_in, w_ple_gate,
              a_in_proj, a_conv_w, a_conv_b, a_dt_bias, a_A_log, a_D, a_norm, a_out_proj,
              ln_kv, w_kv_a, kv_norm, kr_norm, w_kv_b, k_norm,
              b_w_dq, b_q_norm, b_w_q_b, b_qn_nope, b_qn_rope, b_w_o):

    def run_group(x, p, pos, conv_in, ssm_in, past_ckv, past_kr):
        b, T = x.shape[0], x.shape[1]
        h = x
        convs, ssms = [], []
        ckv_new = kr_new = k_nope = v = k_rope_all = k_pos = None
        for i in range(DEPTH):
            h = h + 0.5 * swiglu(rmsnorm(h, ln_ffn1[i]), w1_gate[i], w1_up[i], w1_down[i])
            u = rmsnorm(h, ln_mix[i])
            if i < N_A:
                mix, cbuf, sst = mamba2_mixer(u, conv_in[i], ssm_in[i], a_in_proj[i], a_conv_w[i], a_conv_b[i],
                                              a_dt_bias[i], a_A_log[i], a_D[i], a_norm[i], a_out_proj[i])
                convs.append(cbuf)
                ssms.append(sst)
            else:
                j = i - N_A
                q_nope, q_rope = mla_queries(u, pos, b_w_dq[j], b_q_norm[j], b_w_q_b[j], b_qn_nope[j], b_qn_rope[j])
                if past_ckv is None:
                    o = attend_blocked(q_nope, q_rope, k_nope, k_rope_all, v, pos)
                else:
                    o = attend(q_nope, q_rope, k_nope, k_rope_all, v, pos, k_pos)
                mix = o.reshape(b, T, MLA_HEADS * V_HEAD) @ b_w_o[j]
            h = h + mix
            h = h + 0.5 * swiglu(rmsnorm(h, ln_ffn2[i]), w2_gate[i], w2_up[i], w2_down[i])
            h = h + (p[i] @ w_ple_in[i]) * jax.nn.sigmoid(rmsnorm(h, ln_ple[i]) @ w_ple_gate[i])
            if i == N_A - 1:
                ckv_new, kr_new = mla_kv_side(h, pos, ln_kv, w_kv_a, kv_norm, kr_norm)
                if past_ckv is None:
                    ckv_all, k_rope_all, k_pos = ckv_new, kr_new, pos
                else:
                    ckv_all = jnp.concatenate([past_ckv.astype(ckv_new.dtype), ckv_new], axis=1)
                    k_rope_all = jnp.concatenate([past_kr.astype(kr_new.dtype), kr_new], axis=1)
                    k_pos = jnp.arange(past_ckv.shape[1] + T, dtype=jnp.int32)
                k_nope, v = mla_expand(ckv_all, w_kv_b, k_norm)
        return h, jnp.stack(convs), jnp.stack(ssms), ckv_new, kr_new

    pos_p = jnp.arange(x_prompt.shape[1], dtype=jnp.int32)
    conv0 = jnp.zeros((N_A, x_prompt.shape[0], CONV_W - 1, CONV_DIM), x_prompt.dtype)
    ssm0 = jnp.zeros((N_A, x_prompt.shape[0], SSM_HEADS, SSM_HEAD_DIM, D_STATE), jnp.float32)
    y_prompt, conv_p, ssm_p, kv_p, kr_p = run_group(x_prompt, p_prompt, pos_p, conv0, ssm0, None, None)

    past_len = cache_kv_latent.shape[1]
    pos_s = past_len + jnp.arange(x_sample.shape[1], dtype=jnp.int32)
    y_sample, conv_s, ssm_s, kv_s, kr_s = run_group(x_sample, p_sample, pos_s, state_conv, state_ssm,
                                                    cache_kv_latent, cache_k_rope)

    return (y_prompt, y_sample, conv_p, ssm_p, kv_p, kr_p, conv_s, ssm_s, kv_s, kr_s)
```

```python
import functools

import numpy as np
import jax
import jax.numpy as jnp
from jax import lax
from jax.experimental import pallas as pl
from jax.experimental.pallas import tpu as pltpu

F32 = jnp.float32
BF16 = jnp.bfloat16

EPS = 1e-6
CHUNK = 64
SSM_HEAD_DIM = 64
SSM_GROUPS = 4
D_STATE = 128
CONV_W = 4
MLA_HEADS = 16
QK_NOPE = 64
QK_ROPE = 32
V_HEAD = 64
ROPE_BASE = 10000.0
ATTN_SCALE = (QK_NOPE + QK_ROPE) ** -0.5

LANES = 128
SUBLANES = 8
MXU_DIM = 256
VMEM_LIMIT_BYTES = 56 * 1024 * 1024
ROW_TILE = 512
ATTN_TILE = 256
MAMBA_TILE = 256


def _cparams(n_grid):
    return pltpu.CompilerParams(dimension_semantics=("arbitrary",) * n_grid,
                                vmem_limit_bytes=VMEM_LIMIT_BYTES)


def _const_spec(shape):
    nd = len(shape)
    return pl.BlockSpec(shape, lambda *_: (0,) * nd, pipeline_mode=pl.Buffered(1))


def _row_spec(tm, width):
    return pl.BlockSpec((tm, width), lambda i: (i, 0))


def _row_tile(m):
    tm = min(ROW_TILE, m)
    assert m % tm == 0 and tm % SUBLANES == 0
    return tm


def _dot(a, b):
    return jnp.dot(a, b, preferred_element_type=F32)


def _dot_nt(a, b):
    return lax.dot_general(a, b, (((1,), (1,)), ((), ())), preferred_element_type=F32)


def _dot_tn(a, b):
    return lax.dot_general(a, b, (((0,), (0,)), ((), ())), preferred_element_type=F32)


def _rms(x, g):
    ms = jnp.mean(x * x, axis=-1, keepdims=True)
    return x * lax.rsqrt(ms + EPS) * g


def _silu(x):
    return x * jax.nn.sigmoid(x)


def _softplus(x):
    return jnp.maximum(x, 0.0) + jnp.log1p(jnp.exp(-jnp.abs(x)))


def _split_bf16(v, n):
    pieces = []
    r = v
    for _ in range(n):
        p = r.astype(BF16)
        pieces.append(p)
        r = r - p.astype(F32)
    return pieces


def _ffn_body(h, g, wg_ref, wu_ref, wd_ref):
    u = _rms(h, g).astype(BF16)
    a = _dot(u, wg_ref[...])
    b = _dot(u, wu_ref[...])
    act = (_silu(a) * b).astype(BF16)
    return h + 0.5 * _dot(act, wd_ref[...])


def _ffn_kernel(h_ref, g_ref, wg_ref, wu_ref, wd_ref, o_ref):
    o_ref[...] = _ffn_body(h_ref[...], g_ref[...], wg_ref, wu_ref, wd_ref)


def _ffn_ple_kernel(h_ref, p_ref, g_ref, wg_ref, wu_ref, wd_ref, gp_ref, wpi_ref, wpg_ref, o_ref):
    h2 = _ffn_body(h_ref[...], g_ref[...], wg_ref, wu_ref, wd_ref)
    gate = jax.nn.sigmoid(_dot(_rms(h2, gp_ref[...]).astype(BF16), wpg_ref[...]))
    o_ref[...] = h2 + _dot(p_ref[...].astype(BF16), wpi_ref[...]) * gate


def _ffn(h, g, wg, wu, wd):
    m, d = h.shape
    tm = _row_tile(m)
    return pl.pallas_call(
        _ffn_kernel,
        grid=(m // tm,),
        in_specs=[_row_spec(tm, d), _const_spec(g.shape), _const_spec(wg.shape),
                  _const_spec(wu.shape), _const_spec(wd.shape)],
        out_specs=_row_spec(tm, d),
        out_shape=jax.ShapeDtypeStruct((m, d), F32),
        compiler_params=_cparams(1),
    )(h, g, wg, wu, wd)


def _ffn_ple(h, p, g, wg, wu, wd, gp, wpi, wpg):
    m, d = h.shape
    tm = _row_tile(m)
    return pl.pallas_call(
        _ffn_ple_kernel,
        grid=(m // tm,),
        in_specs=[_row_spec(tm, d), _row_spec(tm, p.shape[1]), _const_spec(g.shape),
                  _const_spec(wg.shape), _const_spec(wu.shape), _const_spec(wd.shape),
                  _const_spec(gp.shape), _const_spec(wpi.shape), _const_spec(wpg.shape)],
        out_specs=_row_spec(tm, d),
        out_shape=jax.ShapeDtypeStruct((m, d), F32),
        compiler_params=_cparams(1),
    )(h, p, g, wg, wu, wd, gp, wpi, wpg)


def _inproj_kernel(h_ref, g_ref, wz_ref, wx_ref, wdt_ref, z_ref, xbc_ref, dt_ref):
    u = _rms(h_ref[...], g_ref[...]).astype(BF16)
    z_ref[...] = _dot(u, wz_ref[...])
    xbc_ref[...] = _dot(u, wx_ref[...])
    dt_ref[...] = _dot(u, wdt_ref[...])


def _inproj(h, g, wz, wx, wdt):
    m, d = h.shape
    tm = _row_tile(m)
    widths = (wz.shape[1], wx.shape[1], wdt.shape[1])
    return pl.pallas_call(
        _inproj_kernel,
        grid=(m // tm,),
        in_specs=[_row_spec(tm, d), _const_spec(g.shape), _const_spec(wz.shape),
                  _const_spec(wx.shape), _const_spec(wdt.shape)],
        out_specs=[_row_spec(tm, w) for w in widths],
        out_shape=[jax.ShapeDtypeStruct((m, w), F32) for w in widths],
        compiler_params=_cparams(1),
    )(h, g, wz, wx, wdt)


def _mamba_kernel(chunk, z_ref, xbc_ref, dt_ref, h_ref, cst_ref, sst_ref,
                  convw_ref, convb_ref, dtb_ref, alog_ref, dexp_ref, ng_ref, wout_ref,
                  blt_ref, e_ref, i2_ref,
                  o_ref, cout_ref, sout_ref,
                  xpad, st, x_s, bc_s, ae_s, de_s, y_s, at_s):
    t = pl.program_id(1)
    n_t = pl.num_programs(1)
    tb = xbc_ref.shape[0]
    d_inner = x_s.shape[1]
    gn = SSM_GROUPS * D_STATE
    n_chunks = tb // chunk
    gw = d_inner // SSM_GROUPS
    half = LANES // 2
    pad_rows = half - chunk

    @pl.when(t == 0)
    def _():
        xpad[0:SUBLANES, :] = cst_ref[0]
        for g in range(SSM_GROUPS):
            st[g] = sst_ref[0, g].T

    xpad[SUBLANES:SUBLANES + tb, :] = xbc_ref[...]
    w = convw_ref[...]
    base = SUBLANES - (CONV_W - 1)
    y = convb_ref[...]
    for k in range(CONV_W):
        y = y + xpad[base + k:base + k + tb, :] * w[k:k + 1, :]
    tail = xpad[tb:tb + SUBLANES, :]
    cout_ref[0] = tail
    xpad[0:SUBLANES, :] = tail
    act = _silu(y)
    x_s[...] = act[:, :d_inner]
    bc_s[...] = act[:, d_inner:]

    dt = _softplus(dt_ref[...] + dtb_ref[...])
    a = dt * (-jnp.exp(alog_ref[...]))
    blt = blt_ref[...]
    acs = sum(_dot(blt, p) for p in _split_bf16(a, 3))
    e = e_ref[...]
    ae_s[...] = sum(_dot(p, e) for p in _split_bf16(acs, 2))
    de_s[...] = sum(_dot(p, e) for p in _split_bf16(dt, 2))
    i2 = i2_ref[...]
    at_all = sum(_dot_tn(p, i2) for p in _split_bf16(acs, 3))
    for c in range(n_chunks):
        at_s[c] = at_all[:, c * LANES:(c + 1) * LANES]

    lane = lax.broadcasted_iota(jnp.int32, (chunk, LANES), 1)
    row = lax.broadcasted_iota(jnp.int32, (chunk, LANES), 0)
    low_half = lane < half
    src = lane & (half - 1)
    causal = (src <= row) & (src < chunk)
    dexp = dexp_ref[...]

    def stack_pair(top, bot):
        if pad_rows == 0:
            return jnp.concatenate([top, bot], axis=0)
        zeros = jnp.zeros((pad_rows, LANES), top.dtype)
        return jnp.concatenate([top, zeros, bot, zeros], axis=0)

    def chunk_body(c, carry):
        rows = pl.ds(pl.multiple_of(c * chunk, chunk), chunk)
        ae = ae_s[rows, :]
        xc = x_s[rows, :]
        last = ae[chunk - 1:chunk, :]
        xdt = xc * de_s[rows, :]
        xw = (xdt * jnp.exp(last - ae)).astype(BF16)
        xdt_b = xdt.astype(BF16)
        e_acs = jnp.exp(ae)
        decay = jnp.exp(last)
        bb = bc_s[rows, 0:gn].astype(BF16)
        cb = bc_s[rows, gn:2 * gn].astype(BF16)
        at2 = at_s[c]
        for g in range(SSM_GROUPS):
            gl = slice(g * gw, (g + 1) * gw)
            c_g = cb[:, g * D_STATE:(g + 1) * D_STATE]
            b_g = bb[:, g * D_STATE:(g + 1) * D_STATE]
            cb2 = _dot_nt(c_g, stack_pair(b_g, b_g))
            s_g = st[g]
            y_off = _dot(c_g, s_g.astype(BF16)) * e_acs[:, gl]
            pairs = []
            for q in range(gw // LANES):
                pq = g * (gw // LANES) + q
                pl_ = slice(pq * LANES, (pq + 1) * LANES)
                rowvec = jnp.where(low_half[0:1, :], at2[2 * pq:2 * pq + 1, :], at2[2 * pq + 1:2 * pq + 2, :])
                seg = jnp.exp(jnp.where(causal, ae[:, pl_] - rowvec, -jnp.inf))
                wmat = (cb2 * seg).astype(BF16)
                xp = xdt_b[:, pl_]
                zero = jnp.zeros_like(xp)
                x2 = stack_pair(jnp.where(low_half, xp, zero), jnp.where(low_half, zero, xp))
                pairs.append(_dot(wmat, x2))
            y_diag = jnp.concatenate(pairs, axis=1)
            y_s[rows, gl] = y_diag + y_off + dexp[:, gl] * xc[:, gl]
            st[g] = s_g * decay[:, gl] + _dot_tn(b_g, xw[:, gl])
        return carry

    lax.fori_loop(0, n_chunks, chunk_body, 0)

    yz = y_s[...] * _silu(z_ref[...])
    yn = _rms(yz, ng_ref[...]).astype(BF16)
    o_ref[...] = h_ref[...] + _dot(yn, wout_ref[...])

    @pl.when(t == n_t - 1)
    def _():
        for g in range(SSM_GROUPS):
            sout_ref[0, g] = st[g].T


def _mamba(z, xbc, dt, h, conv_state8, ssm_state, convw, convb, dtb, alog, dexp, ng, wout, batch, seq):
    m, d_inner = z.shape
    conv_dim = xbc.shape[1]
    heads = dt.shape[1]
    d = h.shape[1]
    tb = min(MAMBA_TILE, seq)
    chunk = min(CHUNK, seq)
    assert seq % tb == 0 and tb % chunk == 0 and chunk % SUBLANES == 0 and LANES // 2 % chunk == 0
    n_t = seq // tb
    n_chunks = tb // chunk
    gw = d_inner // SSM_GROUPS

    r = np.arange(tb)
    blt = ((r[:, None] // chunk == r[None, :] // chunk) & (r[None, :] <= r[:, None])).astype(np.float32)
    e = (np.arange(d_inner)[None, :] // SSM_HEAD_DIM == np.arange(heads)[:, None]).astype(np.float32)
    col = np.arange(n_chunks * LANES)
    i2 = ((col[None, :] // LANES == r[:, None] // chunk)
          & (col[None, :] % (LANES // 2) == r[:, None] % chunk)).astype(np.float32)
    blt, e, i2 = (jnp.asarray(v, BF16) for v in (blt, e, i2))

    row_map = lambda b, t: (b * n_t + t, 0)
    const = lambda v: _const_spec(v.shape)
    return pl.pallas_call(
        functools.partial(_mamba_kernel, chunk),
        grid=(batch, n_t),
        in_specs=[pl.BlockSpec((tb, d_inner), row_map), pl.BlockSpec((tb, conv_dim), row_map),
                  pl.BlockSpec((tb, heads), row_map), pl.BlockSpec((tb, d), row_map),
                  pl.BlockSpec((1, SUBLANES, conv_dim), lambda b, t: (b, 0, 0)),
                  pl.BlockSpec((1, SSM_GROUPS, gw, D_STATE), lambda b, t: (b, 0, 0, 0)),
                  const(convw), const(convb), const(dtb), const(alog), const(dexp), const(ng), const(wout),
                  const(blt), const(e), const(i2)],
        out_specs=[pl.BlockSpec((tb, d), row_map),
                   pl.BlockSpec((1, SUBLANES, conv_dim), lambda b, t: (b, 0, 0)),
                   pl.BlockSpec((1, SSM_GROUPS, gw, D_STATE), lambda b, t: (b, 0, 0, 0))],
        out_shape=[jax.ShapeDtypeStruct((m, d), F32),
                   jax.ShapeDtypeStruct((batch, SUBLANES, conv_dim), F32),
                   jax.ShapeDtypeStruct((batch, SSM_GROUPS, gw, D_STATE), F32)],
        scratch_shapes=[pltpu.VMEM((tb + SUBLANES, conv_dim), F32),
                        pltpu.VMEM((SSM_GROUPS, D_STATE, gw), F32),
                        pltpu.VMEM((tb, d_inner), F32),
                        pltpu.VMEM((tb, conv_dim - d_inner), F32),
                        pltpu.VMEM((tb, d_inner), F32),
                        pltpu.VMEM((tb, d_inner), F32),
                        pltpu.VMEM((tb, d_inner), F32),
                        pltpu.VMEM((n_chunks, heads, LANES), F32)],
        compiler_params=_cparams(2),
    )(z, xbc, dt, h, conv_state8, ssm_state, convw, convb, dtb, alog, dexp, ng, wout, blt, e, i2)


def _table_spec(tm, seq, width):
    if seq <= tm:
        return pl.BlockSpec((tm, width), lambda i: (0, 0))
    per = seq // tm
    return pl.BlockSpec((tm, width), lambda i: (i % per, 0))


def _tile_table(tab, tm, seq):
    return jnp.tile(tab, (tm // seq, 1)) if seq < tm else tab


def _kvside_kernel(h_ref, g_ref, wc_ref, wr_ref, wrs_ref, gc_ref, gr_ref, grs_ref, cos_ref, sin_ref,
                   ckv_ref, kr_ref):
    u = _rms(h_ref[...], g_ref[...]).astype(BF16)
    ckv_ref[...] = _rms(_dot(u, wc_ref[...]), gc_ref[...])
    r = _dot(u, wr_ref[...])
    rs = _dot(u, wrs_ref[...])
    inv = lax.rsqrt(jnp.mean(r * r, axis=-1, keepdims=True) + EPS)
    kr_ref[...] = r * inv * gr_ref[...] * cos_ref[...] + rs * inv * grs_ref[...] * sin_ref[...]


def _kvside(h, g, wc, wr, wrs, gc, gr, grs, cos32, sin32, seq):
    m, d = h.shape
    tm = _row_tile(m)
    assert tm % seq == 0 or seq % tm == 0
    cos32, sin32 = _tile_table(cos32, tm, seq), _tile_table(sin32, tm, seq)
    consts = (g, wc, wr, wrs, gc, gr, grs)
    return pl.pallas_call(
        _kvside_kernel,
        grid=(m // tm,),
        in_specs=[_row_spec(tm, d)] + [_const_spec(v.shape) for v in consts]
                 + [_table_spec(tm, seq, QK_ROPE)] * 2,
        out_specs=[_row_spec(tm, wc.shape[1]), _row_spec(tm, QK_ROPE)],
        out_shape=[jax.ShapeDtypeStruct((m, wc.shape[1]), F32), jax.ShapeDtypeStruct((m, QK_ROPE), F32)],
        compiler_params=_cparams(1),
    )(h, *consts, cos32, sin32)


def _group_mean_sq(x, bd_ref):
    bd = bd_ref[...]
    sq = (x * x).astype(BF16)
    return jnp.concatenate([_dot(sq[:, s:s + MXU_DIM], bd) for s in range(0, x.shape[1], MXU_DIM)], axis=1)


def _expand_kernel(ckv_ref, kr_ref, wk_ref, wv_ref, bd_ref, gk_ref, place_ref, k_ref, v_ref):
    c = ckv_ref[...].astype(BF16)
    k = _dot(c, wk_ref[...])
    kn = k * lax.rsqrt(_group_mean_sq(k, bd_ref) + EPS) * gk_ref[...]
    kr = _dot(kr_ref[...].astype(BF16), place_ref[...])
    for hh in range(MLA_HEADS):
        hl = slice(hh * LANES, (hh + 1) * LANES)
        k_ref[:, hl] = (kn[:, hl] + kr).astype(BF16)
    v_ref[...] = _dot(c, wv_ref[...]).astype(BF16)


def _expand(ckv, kr, wk, wv, bd, gk, place):
    m = ckv.shape[0]
    tm = _row_tile(m)
    consts = (wk, wv, bd, gk, place)
    width = MLA_HEADS * LANES
    return pl.pallas_call(
        _expand_kernel,
        grid=(m // tm,),
        in_specs=[_row_spec(tm, ckv.shape[1]), _row_spec(tm, kr.shape[1])]
                 + [_const_spec(v.shape) for v in consts],
        out_specs=[_row_spec(tm, width)] * 2,
        out_shape=[jax.ShapeDtypeStruct((m, width), BF16)] * 2,
        compiler_params=_cparams(1),
    )(ckv, kr, *consts)


def _q_kernel(h_ref, g_ref, wdq_ref, gq_ref, wq_ref, wqs_ref, bd_ref, gh_ref, ghs_ref, c_ref, s_ref, q_ref):
    u = _rms(h_ref[...], g_ref[...]).astype(BF16)
    cq = _rms(_dot(u, wdq_ref[...]), gq_ref[...]).astype(BF16)
    q = _dot(cq, wq_ref[...])
    qs = _dot(cq, wqs_ref[...])
    inv = lax.rsqrt(_group_mean_sq(q, bd_ref) + EPS)
    qn = q * inv * gh_ref[...]
    qsn = qs * inv * ghs_ref[...]
    c = c_ref[...]
    s = s_ref[...]
    for hh in range(MLA_HEADS):
        hl = slice(hh * LANES, (hh + 1) * LANES)
        q_ref[:, hl] = (qn[:, hl] * c + qsn[:, hl] * s).astype(BF16)


def _queries(h, g, wdq, gq, wq, wqs, bd, gh, ghs, c128, s128, seq):
    m, d = h.shape
    tm = _row_tile(m)
    assert tm % seq == 0 or seq % tm == 0
    c128, s128 = _tile_table(c128, tm, seq), _tile_table(s128, tm, seq)
    consts = (g, wdq, gq, wq, wqs, bd, gh, ghs)
    width = MLA_HEADS * LANES
    return pl.pallas_call(
        _q_kernel,
        grid=(m // tm,),
        in_specs=[_row_spec(tm, d)] + [_const_spec(v.shape) for v in consts]
                 + [_table_spec(tm, seq, LANES)] * 2,
        out_specs=_row_spec(tm, width),
        out_shape=jax.ShapeDtypeStruct((m, width), BF16),
        compiler_params=_cparams(1),
    )(h, *consts, c128, s128)


def _attn_kernel(tk, q_pos0, s_valid, causal_skip, q_ref, k_ref, v_ref, h_ref, wo_ref, o_ref, o_s):
    i = pl.program_id(1)
    tq = q_ref.shape[1]
    n_kv_all = k_ref.shape[1] // tk
    if causal_skip:
        q_end = q_pos0 + (i + 1) * tq
        n_kv = jnp.minimum((q_end + tk - 1) // tk, n_kv_all)
    else:
        n_kv = n_kv_all
    chunk_shift = CHUNK.bit_length() - 1
    assert CHUNK == 1 << chunk_shift
    q_chunk = (q_pos0 + i * tq + lax.broadcasted_iota(jnp.int32, (tq, tk), 0)) >> chunk_shift
    k_iota = lax.broadcasted_iota(jnp.int32, (tq, tk), 1)

    for hh in range(MLA_HEADS):
        hl = slice(hh * LANES, (hh + 1) * LANES)
        qh = q_ref[0, :, hl]

        def body(j, carry, hl=hl, qh=qh):
            m_i, l_i, acc = carry
            k0 = pl.multiple_of(j * tk, tk)
            kb = k_ref[0, pl.ds(k0, tk), hl]
            vb = v_ref[0, pl.ds(k0, tk), hl]
            s = _dot_nt(qh, kb) * ATTN_SCALE
            k_idx = k0 + k_iota
            s = jnp.where(((k_idx >> chunk_shift) <= q_chunk) & (k_idx < s_valid), s, -jnp.inf)
            m_new = jnp.maximum(m_i, jnp.max(s, axis=-1, keepdims=True))
            alpha = jnp.exp(m_i - m_new)
            p = jnp.exp(s - m_new)
            l_new = alpha * l_i + jnp.sum(p, axis=-1, keepdims=True)
            acc_new = alpha * acc + _dot(p.astype(BF16), vb)
            return m_new, l_new, acc_new

        init = (jnp.full((tq, 1), -jnp.inf, F32), jnp.zeros((tq, 1), F32), jnp.zeros((tq, LANES), F32))
        _, l_f, acc_f = lax.fori_loop(0, n_kv, body, init)
        o_s[:, hl] = (acc_f / l_f).astype(BF16)

    o_ref[0] = h_ref[0] + _dot(o_s[...], wo_ref[...])


def _attention(q, k, v, h, wo, batch, tq, q_pos0, s_valid, causal_skip):
    width = q.shape[2]
    t_q = q.shape[1]
    s_len = k.shape[1]
    d = h.shape[2]
    tk = min(ATTN_TILE, s_len)
    assert t_q % tq == 0 and s_len % tk == 0
    kv_bytes = s_len * width * jnp.dtype(BF16).itemsize
    kv_buffers = 2 if 4 * kv_bytes <= VMEM_LIMIT_BYTES // 2 else 1
    kv_spec = pl.BlockSpec((1, s_len, width), lambda b, i: (b, 0, 0), pipeline_mode=pl.Buffered(kv_buffers))
    return pl.pallas_call(
        functools.partial(_attn_kernel, tk, q_pos0, s_valid, causal_skip),
        grid=(batch, t_q // tq),
        in_specs=[pl.BlockSpec((1, tq, width), lambda b, i: (b, i, 0)),
                  kv_spec, kv_spec,
                  pl.BlockSpec((1, tq, d), lambda b, i: (b, i, 0)),
                  _const_spec(wo.shape)],
        out_specs=pl.BlockSpec((1, tq, d), lambda b, i: (b, i, 0)),
        out_shape=jax.ShapeDtypeStruct((batch, t_q, d), F32),
        scratch_shapes=[pltpu.VMEM((tq, width), BF16)],
        compiler_params=_cparams(2),
    )(q, k, v, h, wo)


def _head_pad_cols(w, per_head, start, count, offset=0):
    k = w.shape[0]
    w3 = w.reshape(k, MLA_HEADS, per_head)[:, :, start:start + count]
    out = jnp.zeros((k, MLA_HEADS, LANES), w.dtype)
    out = out.at[:, :, offset:offset + count].set(w3)
    return out.reshape(k, MLA_HEADS * LANES)


def _head_pad_vec(pieces):
    blk = jnp.zeros((LANES,), F32)
    for off, vec in pieces:
        blk = blk.at[off:off + vec.shape[0]].set(vec.astype(F32))
    return jnp.tile(blk, MLA_HEADS)[None, :]


def _block_diag_mean(groups):
    bd = np.zeros((MXU_DIM, MXU_DIM), np.float32)
    for tile in range(MXU_DIM // LANES):
        for off, size in groups:
            lo = tile * LANES + off
            bd[lo:lo + size, lo:lo + size] = 1.0 / size
    return jnp.asarray(bd, BF16)


def _rope_tables(pos):
    half = QK_ROPE // 2
    inv = ROPE_BASE ** (-jnp.arange(half, dtype=F32) / half)
    ang = pos.astype(F32)[:, None] * inv[None, :]
    return jnp.cos(ang), jnp.sin(ang)


def kernel(x_prompt, x_sample, p_prompt, p_sample, state_conv, state_ssm, cache_kv_latent, cache_k_rope, ln_ffn1, w1_gate, w1_up, w1_down, ln_mix, ln_ffn2, w2_gate, w2_up, w2_down, ln_ple, w_ple_in, w_ple_gate, a_in_proj, a_conv_w, a_conv_b, a_dt_bias, a_A_log, a_D, a_norm, a_out_proj, ln_kv, w_kv_a, kv_norm, kr_norm, w_kv_b, k_norm, b_w_dq, b_q_norm, b_w_q_b, b_qn_nope, b_qn_rope, b_w_o):
    depth = ln_ffn1.shape[0]
    n_a = a_in_proj.shape[0]
    assert depth == 2 and n_a == 1 and b_w_dq.shape[0] == 1
    d_model = x_prompt.shape[-1]
    heads = a_dt_bias.shape[-1]
    d_inner = heads * SSM_HEAD_DIM
    conv_dim = a_conv_w.shape[-1]
    kv_lora = kv_norm.shape[0]
    half = QK_ROPE // 2
    bf = lambda w: w.astype(BF16)
    row = lambda v: v.astype(F32)[None, :]

    ffn1 = [(row(ln_ffn1[i]), bf(w1_gate[i]), bf(w1_up[i]), bf(w1_down[i])) for i in range(depth)]
    ffn2 = [(row(ln_ffn2[i]), bf(w2_gate[i]), bf(w2_up[i]), bf(w2_down[i]),
             row(ln_ple[i]), bf(w_ple_in[i]), bf(w_ple_gate[i])) for i in range(depth)]
    w_in = a_in_proj[0]
    mamba_w = dict(
        g=row(ln_mix[0]), wz=bf(w_in[:, :d_inner]), wx=bf(w_in[:, d_inner:d_inner + conv_dim]),
        wdt=bf(w_in[:, d_inner + conv_dim:]), convw=a_conv_w[0], convb=row(a_conv_b[0]),
        dtb=row(a_dt_bias[0]), alog=row(a_A_log[0]), dexp=row(jnp.repeat(a_D[0], SSM_HEAD_DIM)),
        ng=row(a_norm[0]), wout=bf(a_out_proj[0]))

    swap = np.concatenate([np.arange(half, QK_ROPE), np.arange(half)])
    w_r = w_kv_a[:, kv_lora:]
    kv_w = dict(g=row(ln_kv), wc=bf(w_kv_a[:, :kv_lora]), wr=bf(w_r), wrs=bf(w_r[:, swap]),
                gc=row(kv_norm), gr=row(kr_norm), grs=row(kr_norm[swap]))
    per_kv = QK_NOPE + V_HEAD
    ex_w = dict(wk=bf(_head_pad_cols(w_kv_b, per_kv, 0, QK_NOPE)),
                wv=bf(_head_pad_cols(w_kv_b, per_kv, QK_NOPE, V_HEAD)),
                bd=_block_diag_mean([(0, QK_NOPE)]),
                gk=_head_pad_vec([(0, k_norm)]),
                place=jnp.asarray(np.eye(QK_ROPE, LANES, k=QK_NOPE), BF16))
    per_q = QK_NOPE + QK_ROPE
    wqb = b_w_q_b[0]
    wq = _head_pad_cols(wqb, per_q, 0, per_q)
    wqs = (_head_pad_cols(wqb, per_q, QK_NOPE + half, half, offset=QK_NOPE)
           + _head_pad_cols(wqb, per_q, QK_NOPE, half, offset=QK_NOPE + half))
    gq_rope = b_qn_rope[0]
    q_w = dict(g=row(ln_mix[1]), wdq=bf(b_w_dq[0]), gq=row(b_q_norm[0]), wq=bf(wq), wqs=bf(wqs),
               bd=_block_diag_mean([(0, QK_NOPE), (QK_NOPE, QK_ROPE)]),
               gh=_head_pad_vec([(0, b_qn_nope[0]), (QK_NOPE, gq_rope)]),
               ghs=_head_pad_vec([(QK_NOPE, gq_rope[swap])]))
    wo = b_w_o[0].reshape(MLA_HEADS, V_HEAD, d_model)
    wo = bf(jnp.zeros((MLA_HEADS, LANES, d_model), F32).at[:, :V_HEAD].set(wo).reshape(MLA_HEADS * LANES, d_model))

    def rope_tabs(pos):
        cos, sin = _rope_tables(pos)
        seq = pos.shape[0]
        cos32 = jnp.concatenate([cos, cos], axis=1)
        sin32 = jnp.concatenate([-sin, sin], axis=1)
        c128 = jnp.concatenate([jnp.ones((seq, QK_NOPE), F32), cos32,
                                jnp.zeros((seq, LANES - per_q), F32)], axis=1)
        s128 = jnp.concatenate([jnp.zeros((seq, QK_NOPE), F32), sin32,
                                jnp.zeros((seq, LANES - per_q), F32)], axis=1)
        return cos32, sin32, c128, s128

    def run_group(x, p, pos0, conv_in, ssm_in, past_ckv, past_kr):
        batch, seq, _ = x.shape
        m = batch * seq
        pos = pos0 + jnp.arange(seq, dtype=jnp.int32)
        cos32, sin32, c128, s128 = rope_tabs(pos)
        h = x.reshape(m, d_model)
        p2 = p.reshape(depth, m, p.shape[-1])

        h = _ffn(h, *ffn1[0])
        mw = mamba_w
        z, xbc, dt = _inproj(h, mw["g"], mw["wz"], mw["wx"], mw["wdt"])
        conv8 = jnp.pad(conv_in, ((0, 0), (SUBLANES - (CONV_W - 1), 0), (0, 0)))
        ssm4 = ssm_in.reshape(batch, SSM_GROUPS, d_inner // SSM_GROUPS, D_STATE)
        h, conv_out, ssm_out = _mamba(z, xbc, dt, h, conv8, ssm4, mw["convw"], mw["convb"], mw["dtb"],
                                      mw["alog"], mw["dexp"], mw["ng"], mw["wout"], batch, seq)
        conv_out = conv_out[None, :, SUBLANES - (CONV_W - 1):, :]
        ssm_out = ssm_out.reshape(1, batch, heads, SSM_HEAD_DIM, D_STATE)
        h = _ffn_ple(h, p2[0], *ffn2[0])

        kw = kv_w
        ckv_new, kr_new = _kvside(h, kw["g"], kw["wc"], kw["wr"], kw["wrs"], kw["gc"], kw["gr"], kw["grs"],
                                  cos32, sin32, seq)
        ew = ex_w
        expand = lambda c, r: _expand(c, r, ew["wk"], ew["wv"], ew["bd"], ew["gk"], ew["place"])
        k_new, v_new = expand(ckv_new, kr_new)
        width = MLA_HEADS * LANES
        k_new = k_new.reshape(batch, seq, width)
        v_new = v_new.reshape(batch, seq, width)
        if past_ckv is None:
            k_all, v_all, s_valid = k_new, v_new, seq
            tq = min(ATTN_TILE, seq)
        else:
            past = past_ckv.shape[1]
            k_old, v_old = expand(past_ckv.reshape(batch * past, kv_lora), past_kr.reshape(batch * past, QK_ROPE))
            s_valid = past + seq
            s_pad = -(-s_valid // ATTN_TILE) * ATTN_TILE
            fill = jnp.zeros((batch, s_pad - s_valid, width), BF16)
            k_all = jnp.concatenate([k_old.reshape(batch, past, width), k_new, fill], axis=1)
            v_all = jnp.concatenate([v_old.reshape(batch, past, width), v_new, fill], axis=1)
            tq = seq

        h = _ffn(h, *ffn1[1])
        qw = q_w
        q = _queries(h, qw["g"], qw["wdq"], qw["gq"], qw["wq"], qw["wqs"], qw["bd"], qw["gh"], qw["ghs"],
                     c128, s128, seq)
        h = _attention(q.reshape(batch, seq, width), k_all, v_all, h.reshape(batch, seq, d_model), wo,
                       batch, tq, pos0, s_valid, past_ckv is None)
        h = _ffn_ple(h.reshape(m, d_model), p2[1], *ffn2[1])
        return (h.reshape(batch, seq, d_model), conv_out, ssm_out,
                ckv_new.reshape(batch, seq, kv_lora), kr_new.reshape(batch, seq, QK_ROPE))

    b_p = x_prompt.shape[0]
    conv0 = jnp.zeros((b_p, CONV_W - 1, conv_dim), F32)
    ssm0 = jnp.zeros((b_p, heads, SSM_HEAD_DIM, D_STATE), F32)
    y_p, conv_p, ssm_p, kv_p, kr_p = run_group(x_prompt, p_prompt, 0, conv0, ssm0, None, None)
    y_s, conv_s, ssm_s, kv_s, kr_s = run_group(x_sample, p_sample, cache_kv_latent.shape[1],
                                               state_conv[0], state_ssm[0], cache_kv_latent, cache_k_rope)
    return (y_p, y_s, conv_p, ssm_p, kv_p, kr_p, conv_s, ssm_s, kv_s, kr_s)
```

```python
import functools

import numpy as np
import jax
import jax.numpy as jnp
from jax import lax
from jax.experimental import pallas as pl
from jax.experimental.pallas import tpu as pltpu

F32 = jnp.float32
BF16 = jnp.bfloat16

EPS = 1e-6
CHUNK = 64
SSM_HEAD_DIM = 64
SSM_GROUPS = 4
D_STATE = 128
CONV_W = 4
MLA_HEADS = 16
QK_NOPE = 64
QK_ROPE = 32
V_HEAD = 64
ROPE_BASE = 10000.0
ATTN_SCALE = (QK_NOPE + QK_ROPE) ** -0.5

LANES = 128
SUBLANES = 8
MXU_DIM = 256
VMEM_LIMIT_BYTES = 56 * 1024 * 1024
ROW_TILE = 512
ATTN_TILE = 256
MAMBA_TILE = 256
CONV_ROWS = 32
GATE_ROWS = 32


def _cparams(n_grid):
    return pltpu.CompilerParams(dimension_semantics=("arbitrary",) * n_grid,
                                vmem_limit_bytes=VMEM_LIMIT_BYTES)


def _const_spec(shape):
    nd = len(shape)
    return pl.BlockSpec(shape, lambda *_: (0,) * nd, pipeline_mode=pl.Buffered(1))


def _row_spec(tm, width):
    return pl.BlockSpec((tm, width), lambda i: (i, 0))


def _row_tile(m):
    tm = min(ROW_TILE, m)
    assert m % tm == 0 and tm % SUBLANES == 0
    return tm


def _dot(a, b):
    return jnp.dot(a, b, preferred_element_type=F32)


def _dot_nt(a, b):
    return lax.dot_general(a, b, (((1,), (1,)), ((), ())), preferred_element_type=F32)


def _dot_tn(a, b):
    return lax.dot_general(a, b, (((0,), (0,)), ((), ())), preferred_element_type=F32)


def _rms(x, g):
    ms = jnp.mean(x * x, axis=-1, keepdims=True)
    return x * lax.rsqrt(ms + EPS) * g


def _silu(x):
    return x * jax.nn.sigmoid(x)


def _softplus(x):
    return jnp.maximum(x, 0.0) + jnp.log1p(jnp.exp(-jnp.abs(x)))


def _split_bf16(v, n):
    pieces = []
    r = v
    for _ in range(n):
        p = r.astype(BF16)
        pieces.append(p)
        r = r - p.astype(F32)
    return pieces


def _ffn_body(h, g, wg_ref, wu_ref, wd_ref):
    u = _rms(h, g).astype(BF16)
    a = _dot(u, wg_ref[...])
    b = _dot(u, wu_ref[...])
    act = (_silu(a) * b).astype(BF16)
    return h + 0.5 * _dot(act, wd_ref[...])


def _ffn_kernel(h_ref, g_ref, wg_ref, wu_ref, wd_ref, o_ref):
    o_ref[...] = _ffn_body(h_ref[...], g_ref[...], wg_ref, wu_ref, wd_ref)


def _ffn_ple_kernel(h_ref, p_ref, g_ref, wg_ref, wu_ref, wd_ref, gp_ref, wpi_ref, wpg_ref, o_ref):
    h2 = _ffn_body(h_ref[...], g_ref[...], wg_ref, wu_ref, wd_ref)
    gate = jax.nn.sigmoid(_dot(_rms(h2, gp_ref[...]).astype(BF16), wpg_ref[...]))
    o_ref[...] = h2 + _dot(p_ref[...].astype(BF16), wpi_ref[...]) * gate


def _ffn(h, g, wg, wu, wd):
    m, d = h.shape
    tm = _row_tile(m)
    return pl.pallas_call(
        _ffn_kernel,
        grid=(m // tm,),
        in_specs=[_row_spec(tm, d), _const_spec(g.shape), _const_spec(wg.shape),
                  _const_spec(wu.shape), _const_spec(wd.shape)],
        out_specs=_row_spec(tm, d),
        out_shape=jax.ShapeDtypeStruct((m, d), F32),
        compiler_params=_cparams(1),
    )(h, g, wg, wu, wd)


def _ffn_ple(h, p, g, wg, wu, wd, gp, wpi, wpg):
    m, d = h.shape
    tm = _row_tile(m)
    return pl.pallas_call(
        _ffn_ple_kernel,
        grid=(m // tm,),
        in_specs=[_row_spec(tm, d), _row_spec(tm, p.shape[1]), _const_spec(g.shape),
                  _const_spec(wg.shape), _const_spec(wu.shape), _const_spec(wd.shape),
                  _const_spec(gp.shape), _const_spec(wpi.shape), _const_spec(wpg.shape)],
        out_specs=_row_spec(tm, d),
        out_shape=jax.ShapeDtypeStruct((m, d), F32),
        compiler_params=_cparams(1),
    )(h, p, g, wg, wu, wd, gp, wpi, wpg)


def _inproj_kernel(h_ref, g_ref, wz_ref, wx_ref, wdt_ref, z_ref, xbc_ref, dt_ref):
    u = _rms(h_ref[...], g_ref[...]).astype(BF16)
    z_ref[...] = _dot(u, wz_ref[...])
    xbc_ref[...] = _dot(u, wx_ref[...])
    dt_ref[...] = _dot(u, wdt_ref[...])


def _inproj(h, g, wz, wx, wdt):
    m, d = h.shape
    tm = _row_tile(m)
    widths = (wz.shape[1], wx.shape[1], wdt.shape[1])
    return pl.pallas_call(
        _inproj_kernel,
        grid=(m // tm,),
        in_specs=[_row_spec(tm, d), _const_spec(g.shape), _const_spec(wz.shape),
                  _const_spec(wx.shape), _const_spec(wdt.shape)],
        out_specs=[_row_spec(tm, w) for w in widths],
        out_shape=[jax.ShapeDtypeStruct((m, w), F32) for w in widths],
        compiler_params=_cparams(1),
    )(h, g, wz, wx, wdt)


def _mamba_kernel(chunk, z_ref, xbc_ref, dt_ref, h_ref, cst_ref, sst_ref,
                  convw_ref, convb_ref, dtb_ref, alog_ref, dexp_ref, ng_ref, wout_ref,
                  blt_ref, e2_ref, i2_ref,
                  o_ref, cout_ref, sout_ref,
                  xpad, st, x_s, bc_s, ae_s, de_s, y_s, at_s, yn_s):
    t = pl.program_id(1)
    n_t = pl.num_programs(1)
    tb = xbc_ref.shape[0]
    d_inner = x_s.shape[1]
    gn = SSM_GROUPS * D_STATE
    n_chunks = tb // chunk
    gw = d_inner // SSM_GROUPS
    half = LANES // 2
    pad_rows = half - chunk

    @pl.when(t == 0)
    def _():
        xpad[0:SUBLANES, :] = cst_ref[0]
        for g in range(SSM_GROUPS):
            st[g] = sst_ref[0, g].T

    xpad[SUBLANES:SUBLANES + tb, :] = xbc_ref[...]
    w = convw_ref[...]
    base = SUBLANES - (CONV_W - 1)
    rb = min(CONV_ROWS, tb)

    def conv_rows(r, carry):
        r0 = pl.multiple_of(r * rb, rb)
        ext = xpad[pl.ds(r0, rb + SUBLANES), :]
        y = convb_ref[...]
        for k in range(CONV_W):
            y = y + ext[base + k:base + k + rb, :] * w[k:k + 1, :]
        act = _silu(y)
        x_s[pl.ds(r0, rb), :] = act[:, :d_inner]
        bc_s[pl.ds(r0, rb), :] = act[:, d_inner:]
        return carry

    lax.fori_loop(0, tb // rb, conv_rows, 0)
    tail = xpad[tb:tb + SUBLANES, :]
    cout_ref[0] = tail
    xpad[0:SUBLANES, :] = tail

    dt = _softplus(dt_ref[...] + dtb_ref[...])
    a = dt * (-jnp.exp(alog_ref[...]))
    blt = blt_ref[...]
    acs = sum(_dot(blt, p) for p in _split_bf16(a, 3))
    e2 = e2_ref[...]
    ae_s[...] = _dot(jnp.concatenate(_split_bf16(acs, 2), axis=1), e2)
    de_s[...] = _dot(jnp.concatenate(_split_bf16(dt, 2), axis=1), e2)
    i2 = i2_ref[...]
    at_all = sum(_dot_tn(p, i2) for p in _split_bf16(acs, 3))
    for c in range(n_chunks):
        at_s[c] = at_all[:, c * LANES:(c + 1) * LANES]

    lane = lax.broadcasted_iota(jnp.int32, (chunk, LANES), 1)
    row = lax.broadcasted_iota(jnp.int32, (chunk, LANES), 0)
    low_half = lane < half
    src = lane & (half - 1)
    causal = (src <= row) & (src < chunk)
    dexp = dexp_ref[...]

    def stack_pair(top, bot):
        if pad_rows == 0:
            return jnp.concatenate([top, bot], axis=0)
        zeros = jnp.zeros((pad_rows, LANES), top.dtype)
        return jnp.concatenate([top, zeros, bot, zeros], axis=0)

    def chunk_body(c, carry):
        rows = pl.ds(pl.multiple_of(c * chunk, chunk), chunk)
        ae = ae_s[rows, :]
        xc = x_s[rows, :]
        last = ae[chunk - 1:chunk, :]
        xdt = xc * de_s[rows, :]
        xw = (xdt * jnp.exp(last - ae)).astype(BF16)
        xdt_b = xdt.astype(BF16)
        e_acs = jnp.exp(ae)
        decay = jnp.exp(last)
        bb = bc_s[rows, 0:gn].astype(BF16)
        cb = bc_s[rows, gn:2 * gn].astype(BF16)
        at2 = at_s[c]
        for g in range(SSM_GROUPS):
            gl = slice(g * gw, (g + 1) * gw)
            c_g = cb[:, g * D_STATE:(g + 1) * D_STATE]
            b_g = bb[:, g * D_STATE:(g + 1) * D_STATE]
            cb2 = _dot_nt(c_g, stack_pair(b_g, b_g))
            s_g = st[g]
            y_off = _dot(c_g, s_g.astype(BF16)) * e_acs[:, gl]
            pairs = []
            for q in range(gw // LANES):
                pq = g * (gw // LANES) + q
                pl_ = slice(pq * LANES, (pq + 1) * LANES)
                rowvec = jnp.where(low_half[0:1, :], at2[2 * pq:2 * pq + 1, :], at2[2 * pq + 1:2 * pq + 2, :])
                seg = jnp.exp(jnp.where(causal, ae[:, pl_] - rowvec, -jnp.inf))
                wmat = (cb2 * seg).astype(BF16)
                xp = xdt_b[:, pl_]
                zero = jnp.zeros_like(xp)
                x2 = stack_pair(jnp.where(low_half, xp, zero), jnp.where(low_half, zero, xp))
                pairs.append(_dot(wmat, x2))
            y_diag = jnp.concatenate(pairs, axis=1)
            y_s[rows, gl] = y_diag + y_off + dexp[:, gl] * xc[:, gl]
            st[g] = s_g * decay[:, gl] + _dot_tn(b_g, xw[:, gl])
        return carry

    lax.fori_loop(0, n_chunks, chunk_body, 0)

    gb = min(GATE_ROWS, tb)

    def gate_rows(r, carry):
        rows = pl.ds(pl.multiple_of(r * gb, gb), gb)
        yz = y_s[rows, :] * _silu(z_ref[rows, :])
        yn_s[rows, :] = _rms(yz, ng_ref[...]).astype(BF16)
        return carry

    lax.fori_loop(0, tb // gb, gate_rows, 0)
    o_ref[...] = h_ref[...] + _dot(yn_s[...], wout_ref[...])

    @pl.when(t == n_t - 1)
    def _():
        for g in range(SSM_GROUPS):
            sout_ref[0, g] = st[g].T


def _mamba(z, xbc, dt, h, conv_state8, ssm_state, convw, convb, dtb, alog, dexp, ng, wout, batch, seq):
    m, d_inner = z.shape
    conv_dim = xbc.shape[1]
    heads = dt.shape[1]
    d = h.shape[1]
    tb = min(MAMBA_TILE, seq)
    chunk = min(CHUNK, seq)
    assert seq % tb == 0 and tb % chunk == 0 and chunk % SUBLANES == 0 and LANES // 2 % chunk == 0
    n_t = seq // tb
    n_chunks = tb // chunk
    gw = d_inner // SSM_GROUPS

    r = np.arange(tb)
    blt = ((r[:, None] // chunk == r[None, :] // chunk) & (r[None, :] <= r[:, None])).astype(np.float32)
    e = (np.arange(d_inner)[None, :] // SSM_HEAD_DIM == np.arange(heads)[:, None]).astype(np.float32)
    e = np.concatenate([e, e], axis=0)
    col = np.arange(n_chunks * LANES)
    i2 = ((col[None, :] // LANES == r[:, None] // chunk)
          & (col[None, :] % (LANES // 2) == r[:, None] % chunk)).astype(np.float32)
    blt, e, i2 = (jnp.asarray(v, BF16) for v in (blt, e, i2))

    row_map = lambda b, t: (b * n_t + t, 0)
    const = lambda v: _const_spec(v.shape)
    return pl.pallas_call(
        functools.partial(_mamba_kernel, chunk),
        grid=(batch, n_t),
        in_specs=[pl.BlockSpec((tb, d_inner), row_map), pl.BlockSpec((tb, conv_dim), row_map),
                  pl.BlockSpec((tb, heads), row_map), pl.BlockSpec((tb, d), row_map),
                  pl.BlockSpec((1, SUBLANES, conv_dim), lambda b, t: (b, 0, 0)),
                  pl.BlockSpec((1, SSM_GROUPS, gw, D_STATE), lambda b, t: (b, 0, 0, 0)),
                  const(convw), const(convb), const(dtb), const(alog), const(dexp), const(ng), const(wout),
                  const(blt), const(e), const(i2)],
        out_specs=[pl.BlockSpec((tb, d), row_map),
                   pl.BlockSpec((1, SUBLANES, conv_dim), lambda b, t: (b, 0, 0)),
                   pl.BlockSpec((1, SSM_GROUPS, gw, D_STATE), lambda b, t: (b, 0, 0, 0))],
        out_shape=[jax.ShapeDtypeStruct((m, d), F32),
                   jax.ShapeDtypeStruct((batch, SUBLANES, conv_dim), F32),
                   jax.ShapeDtypeStruct((batch, SSM_GROUPS, gw, D_STATE), F32)],
        scratch_shapes=[pltpu.VMEM((tb + SUBLANES, conv_dim), F32),
                        pltpu.VMEM((SSM_GROUPS, D_STATE, gw), F32),
                        pltpu.VMEM((tb, d_inner), F32),
                        pltpu.VMEM((tb, conv_dim - d_inner), F32),
                        pltpu.VMEM((tb, d_inner), F32),
                        pltpu.VMEM((tb, d_inner), F32),
                        pltpu.VMEM((tb, d_inner), F32),
                        pltpu.VMEM((n_chunks, heads, LANES), F32),
                        pltpu.VMEM((tb, d_inner), BF16)],
        compiler_params=_cparams(2),
    )(z, xbc, dt, h, conv_state8, ssm_state, convw, convb, dtb, alog, dexp, ng, wout, blt, e, i2)


def _table_spec(tm, seq, width):
    if seq <= tm:
        return pl.BlockSpec((tm, width), lambda i: (0, 0))
    per = seq // tm
    return pl.BlockSpec((tm, width), lambda i: (i % per, 0))


def _tile_table(tab, tm, seq):
    return jnp.tile(tab, (tm // seq, 1)) if seq < tm else tab


def _kvside_kernel(h_ref, g_ref, wc_ref, wr_ref, wrs_ref, gc_ref, gr_ref, grs_ref, cos_ref, sin_ref,
                   ckv_ref, kr_ref):
    u = _rms(h_ref[...], g_ref[...]).astype(BF16)
    ckv_ref[...] = _rms(_dot(u, wc_ref[...]), gc_ref[...])
    r = _dot(u, wr_ref[...])
    rs = _dot(u, wrs_ref[...])
    inv = lax.rsqrt(jnp.mean(r * r, axis=-1, keepdims=True) + EPS)
    kr_ref[...] = r * inv * gr_ref[...] * cos_ref[...] + rs * inv * grs_ref[...] * sin_ref[...]


def _kvside(h, g, wc, wr, wrs, gc, gr, grs, cos32, sin32, seq):
    m, d = h.shape
    tm = _row_tile(m)
    assert tm % seq == 0 or seq % tm == 0
    cos32, sin32 = _tile_table(cos32, tm, seq), _tile_table(sin32, tm, seq)
    consts = (g, wc, wr, wrs, gc, gr, grs)
    return pl.pallas_call(
        _kvside_kernel,
        grid=(m // tm,),
        in_specs=[_row_spec(tm, d)] + [_const_spec(v.shape) for v in consts]
                 + [_table_spec(tm, seq, QK_ROPE)] * 2,
        out_specs=[_row_spec(tm, wc.shape[1]), _row_spec(tm, QK_ROPE)],
        out_shape=[jax.ShapeDtypeStruct((m, wc.shape[1]), F32), jax.ShapeDtypeStruct((m, QK_ROPE), F32)],
        compiler_params=_cparams(1),
    )(h, *consts, cos32, sin32)


def _group_mean_sq(x, bd_ref):
    bd = bd_ref[...]
    sq = (x * x).astype(BF16)
    return jnp.concatenate([_dot(sq[:, s:s + MXU_DIM], bd) for s in range(0, x.shape[1], MXU_DIM)], axis=1)


def _expand_kernel(ckv_ref, kr_ref, wk_ref, wv_ref, bd_ref, gk_ref, place_ref, one_ref, k_ref, v_ref):
    c = ckv_ref[...].astype(BF16)
    k = _dot(c, wk_ref[...])
    kn = k * lax.rsqrt(_group_mean_sq(k, bd_ref) + EPS) * gk_ref[...]
    kr = _dot(kr_ref[...].astype(BF16), place_ref[...])
    for hh in range(MLA_HEADS):
        hl = slice(hh * LANES, (hh + 1) * LANES)
        k_ref[:, hl] = (kn[:, hl] + kr).astype(BF16)
    v_ref[...] = (_dot(c, wv_ref[...]) + one_ref[...]).astype(BF16)


def _expand(ckv, kr, wk, wv, bd, gk, place, one):
    m = ckv.shape[0]
    tm = _row_tile(m)
    consts = (wk, wv, bd, gk, place, one)
    width = MLA_HEADS * LANES
    return pl.pallas_call(
        _expand_kernel,
        grid=(m // tm,),
        in_specs=[_row_spec(tm, ckv.shape[1]), _row_spec(tm, kr.shape[1])]
                 + [_const_spec(v.shape) for v in consts],
        out_specs=[_row_spec(tm, width)] * 2,
        out_shape=[jax.ShapeDtypeStruct((m, width), BF16)] * 2,
        compiler_params=_cparams(1),
    )(ckv, kr, *consts)


def _q_kernel(h_ref, g_ref, wdq_ref, gq_ref, wq_ref, wqs_ref, bd_ref, gh_ref, ghs_ref, c_ref, s_ref, q_ref):
    u = _rms(h_ref[...], g_ref[...]).astype(BF16)
    cq = _rms(_dot(u, wdq_ref[...]), gq_ref[...]).astype(BF16)
    q = _dot(cq, wq_ref[...])
    qs = _dot(cq, wqs_ref[...])
    inv = lax.rsqrt(_group_mean_sq(q, bd_ref) + EPS)
    qn = q * inv * gh_ref[...]
    qsn = qs * inv * ghs_ref[...]
    c = c_ref[...]
    s = s_ref[...]
    for hh in range(MLA_HEADS):
        hl = slice(hh * LANES, (hh + 1) * LANES)
        q_ref[:, hl] = (qn[:, hl] * c + qsn[:, hl] * s).astype(BF16)


def _queries(h, g, wdq, gq, wq, wqs, bd, gh, ghs, c128, s128, seq):
    m, d = h.shape
    tm = _row_tile(m)
    assert tm % seq == 0 or seq % tm == 0
    c128, s128 = _tile_table(c128, tm, seq), _tile_table(s128, tm, seq)
    consts = (g, wdq, gq, wq, wqs, bd, gh, ghs)
    width = MLA_HEADS * LANES
    return pl.pallas_call(
        _q_kernel,
        grid=(m // tm,),
        in_specs=[_row_spec(tm, d)] + [_const_spec(v.shape) for v in consts]
                 + [_table_spec(tm, seq, LANES)] * 2,
        out_specs=_row_spec(tm, width),
        out_shape=jax.ShapeDtypeStruct((m, width), BF16),
        compiler_params=_cparams(1),
    )(h, *consts, c128, s128)


def _q_flat_kernel(h_ref, g_ref, wdq_ref, gq_ref, wn_ref, wr_ref, wrs_ref, bdn_ref, bdr_ref,
                   gn_ref, gr_ref, grs_ref, cos_ref, sin_ref, qn_ref, qr_ref):
    u = _rms(h_ref[...], g_ref[...]).astype(BF16)
    cq = _rms(_dot(u, wdq_ref[...]), gq_ref[...]).astype(BF16)
    qn = _dot(cq, wn_ref[...])
    qn_ref[...] = (qn * lax.rsqrt(_group_mean_sq(qn, bdn_ref) + EPS) * gn_ref[...]).astype(BF16)
    r = _dot(cq, wr_ref[...])
    rs = _dot(cq, wrs_ref[...])
    inv = lax.rsqrt(_group_mean_sq(r, bdr_ref) + EPS)
    qr_ref[...] = (r * inv * gr_ref[...] * cos_ref[...] + rs * inv * grs_ref[...] * sin_ref[...]).astype(BF16)


def _queries_flat(h, g, wdq, gq, wn, wr, wrs, bdn, bdr, gn, gr, grs, cos, sin, seq):
    m, d = h.shape
    tm = _row_tile(m)
    assert tm % seq == 0 or seq % tm == 0
    cos, sin = _tile_table(cos, tm, seq), _tile_table(sin, tm, seq)
    consts = (g, wdq, gq, wn, wr, wrs, bdn, bdr, gn, gr, grs)
    widths = (wn.shape[1], wr.shape[1])
    return pl.pallas_call(
        _q_flat_kernel,
        grid=(m // tm,),
        in_specs=[_row_spec(tm, d)] + [_const_spec(v.shape) for v in consts]
                 + [_table_spec(tm, seq, widths[1])] * 2,
        out_specs=[_row_spec(tm, w) for w in widths],
        out_shape=[jax.ShapeDtypeStruct((m, w), BF16) for w in widths],
        compiler_params=_cparams(1),
    )(h, *consts, cos, sin)


def _cache_attn_kernel(kb, qn_ref, qr_ref, ckv_ref, kr_ref, ckvn_ref, krn_ref, h_ref,
                       wk_ref, wv_ref, bd_ref, gk_ref, wo_ref, o_ref, s_s, v_s, acc_s):
    tq = qn_ref.shape[1]
    past = ckv_ref.shape[1]
    cols = MLA_HEADS * tq
    exp2_scale = ATTN_SCALE * float(np.log2(np.e))
    chunk_shift = CHUNK.bit_length() - 1
    assert CHUNK == 1 << chunk_shift and past % kb == 0

    qn = jnp.concatenate([qn_ref[0]] * MLA_HEADS, axis=0)
    tq_shift, nope_shift = tq.bit_length() - 1, QK_NOPE.bit_length() - 1
    assert tq == 1 << tq_shift and QK_NOPE == 1 << nope_shift
    r_head = lax.broadcasted_iota(jnp.int32, qn.shape, 0) >> tq_shift
    c_head = lax.broadcasted_iota(jnp.int32, qn.shape, 1) >> nope_shift
    q_bd = jnp.where(r_head == c_head, qn, jnp.zeros_like(qn))
    qr_all = qr_ref[0]
    q_rope = jnp.concatenate([qr_all[:, hh * QK_ROPE:(hh + 1) * QK_ROPE] for hh in range(MLA_HEADS)], axis=0)

    def expand(c, kr, rows):
        c = c.astype(BF16)
        k = _dot(c, wk_ref[...])
        kn = (k * lax.rsqrt(_group_mean_sq(k, bd_ref) + EPS) * gk_ref[...]).astype(BF16)
        s = _dot_nt(kn, q_bd) + _dot_nt(kr.astype(BF16), q_rope)
        v = _dot(c, wv_ref[...]).astype(BF16)
        v_s[rows, :] = jnp.concatenate([v, jnp.ones((v.shape[0], LANES), BF16)], axis=1)
        return s

    def expand_block(j, m_run):
        rows = pl.ds(pl.multiple_of(j * kb, kb), kb)
        s = expand(ckv_ref[0, rows, :], kr_ref[0, rows, :], rows)
        s_s[rows, :] = s
        return jnp.maximum(m_run, jnp.max(s, axis=0, keepdims=True))

    m_run = lax.fori_loop(0, past // kb, expand_block, jnp.full((1, cols), -jnp.inf, F32))
    new_rows = pl.ds(past, tq)
    s_new = expand(ckvn_ref[0], krn_ref[0], new_rows)
    k_chunk = (past + lax.broadcasted_iota(jnp.int32, (tq, cols), 0)) >> chunk_shift
    q_chunk = (past + (lax.broadcasted_iota(jnp.int32, (tq, cols), 1) & (tq - 1))) >> chunk_shift
    s_new = jnp.where(k_chunk <= q_chunk, s_new, -jnp.inf)
    m_all = jnp.maximum(m_run, jnp.max(s_new, axis=0, keepdims=True))

    acc_s[...] = _dot_tn(jnp.exp2((s_new - m_all) * exp2_scale).astype(BF16), v_s[new_rows, :])

    def pv_block(j, carry):
        rows = pl.ds(pl.multiple_of(j * kb, kb), kb)
        p = jnp.exp2((s_s[rows, :] - m_all) * exp2_scale).astype(BF16)
        acc_s[...] += _dot_tn(p, v_s[rows, :])
        return carry

    lax.fori_loop(0, past // kb, pv_block, 0)

    width = MLA_HEADS * V_HEAD
    low_half = lax.broadcasted_iota(jnp.int32, (tq, LANES), 1) < V_HEAD
    tiles = []
    for c in range(width // LANES):
        ev = acc_s[2 * c * tq:(2 * c + 1) * tq, :]
        od = acc_s[(2 * c + 1) * tq:(2 * c + 2) * tq, :]
        cl = slice(c * LANES, (c + 1) * LANES)
        tiles.append(jnp.where(low_half, ev[:, cl] / ev[:, width:], od[:, cl] / od[:, width:]))
    o = jnp.concatenate(tiles, axis=1).astype(BF16)
    o_ref[0] = h_ref[0] + _dot(o, wo_ref[...])


def _cache_attention(qn, qr, ckv, kr, ckv_new, kr_new, h, wk, wv, bd, gk, wo):
    batch, tq, _ = qn.shape
    past = ckv.shape[1]
    d = h.shape[2]
    kb = min(ROW_TILE, past)
    assert past % kb == 0 and tq % 16 == 0 and (MLA_HEADS * tq) % LANES == 0
    per = lambda a: pl.BlockSpec((1,) + a.shape[1:], lambda b: (b, 0, 0))
    consts = (wk, wv, bd, gk, wo)
    return pl.pallas_call(
        functools.partial(_cache_attn_kernel, kb),
        grid=(batch,),
        in_specs=[per(qn), per(qr), per(ckv), per(kr), per(ckv_new), per(kr_new), per(h)]
                 + [_const_spec(v.shape) for v in consts],
        out_specs=per(h),
        out_shape=jax.ShapeDtypeStruct(h.shape, F32),
        scratch_shapes=[pltpu.VMEM((past + tq, MLA_HEADS * tq), F32),
                        pltpu.VMEM((past + tq, MLA_HEADS * V_HEAD + LANES), BF16),
                        pltpu.VMEM((MLA_HEADS * tq, MLA_HEADS * V_HEAD + LANES), F32)],
        compiler_params=_cparams(1),
    )(qn, qr, ckv, kr, ckv_new, kr_new, h, *consts)


def _self_attn_kernel(q_ref, k_ref, v_ref, h_ref, wo_ref, o_ref, m_s, acc_s, o_s):
    i = pl.program_id(1)
    tq = q_ref.shape[1]
    assert tq % LANES == 0
    chunk_shift = CHUNK.bit_length() - 1
    assert CHUNK == 1 << chunk_shift
    exp2_scale = ATTN_SCALE * float(np.log2(np.e))
    visible = ((lax.broadcasted_iota(jnp.int32, (tq, tq), 1) >> chunk_shift)
               <= (lax.broadcasted_iota(jnp.int32, (tq, tq), 0) >> chunk_shift))

    m_s[...] = jnp.full(m_s.shape, -jnp.inf, F32)
    acc_s[...] = jnp.zeros(acc_s.shape, F32)

    def step(j, masked):
        k0 = pl.multiple_of(j * tq, tq)
        for hh in range(MLA_HEADS):
            hl = slice(hh * LANES, (hh + 1) * LANES)
            s = _dot_nt(q_ref[0, :, hl], k_ref[0, pl.ds(k0, tq), hl])
            if masked:
                s = jnp.where(visible, s, -jnp.inf)
            tiles = [s[:, c:c + LANES] for c in range(0, tq, LANES)]
            m_old = m_s[hh]
            m_new = jnp.maximum(m_old, jnp.max(functools.reduce(jnp.maximum, tiles), axis=-1, keepdims=True))
            alpha = jnp.exp2((m_old - m_new) * exp2_scale)
            p = jnp.concatenate([jnp.exp2((c - m_new) * exp2_scale).astype(BF16) for c in tiles], axis=1)
            acc_s[hh] = alpha * acc_s[hh] + _dot(p, v_ref[0, pl.ds(k0, tq), hl])
            m_s[hh] = m_new

    def full_step(j, carry):
        step(j, False)
        return carry

    lax.fori_loop(0, i, full_step, 0)
    step(i, True)

    for hh in range(MLA_HEADS):
        acc = acc_s[hh]
        o_s[:, hh * LANES:(hh + 1) * LANES] = (acc / acc[:, V_HEAD:V_HEAD + 1]).astype(BF16)
    o_ref[0] = h_ref[0] + _dot(o_s[...], wo_ref[...])


def _self_attention(q, k, v, h, wo, batch):
    width = q.shape[2]
    seq = q.shape[1]
    d = h.shape[2]
    tq = min(ATTN_TILE, seq)
    assert seq % tq == 0 and tq % CHUNK == 0
    kv_spec = pl.BlockSpec((1, seq, width), lambda b, i: (b, 0, 0))
    return pl.pallas_call(
        _self_attn_kernel,
        grid=(batch, seq // tq),
        in_specs=[pl.BlockSpec((1, tq, width), lambda b, i: (b, i, 0)), kv_spec, kv_spec,
                  pl.BlockSpec((1, tq, d), lambda b, i: (b, i, 0)),
                  _const_spec(wo.shape)],
        out_specs=pl.BlockSpec((1, tq, d), lambda b, i: (b, i, 0)),
        out_shape=jax.ShapeDtypeStruct((batch, seq, d), F32),
        scratch_shapes=[pltpu.VMEM((MLA_HEADS, tq, LANES), F32), pltpu.VMEM((MLA_HEADS, tq, LANES), F32),
                        pltpu.VMEM((tq, width), BF16)],
        compiler_params=_cparams(2),
    )(q, k, v, h, wo)


def _head_pad_cols(w, per_head, start, count, offset=0):
    k = w.shape[0]
    w3 = w.reshape(k, MLA_HEADS, per_head)[:, :, start:start + count]
    out = jnp.zeros((k, MLA_HEADS, LANES), w.dtype)
    out = out.at[:, :, offset:offset + count].set(w3)
    return out.reshape(k, MLA_HEADS * LANES)


def _head_pad_vec(pieces):
    blk = jnp.zeros((LANES,), F32)
    for off, vec in pieces:
        blk = blk.at[off:off + vec.shape[0]].set(vec.astype(F32))
    return jnp.tile(blk, MLA_HEADS)[None, :]


def _block_diag_mean(groups):
    bd = np.zeros((MXU_DIM, MXU_DIM), np.float32)
    for tile in range(MXU_DIM // LANES):
        for off, size in groups:
            lo = tile * LANES + off
            bd[lo:lo + size, lo:lo + size] = 1.0 / size
    return jnp.asarray(bd, BF16)


def _rope_tables(pos):
    half = QK_ROPE // 2
    inv = ROPE_BASE ** (-jnp.arange(half, dtype=F32) / half)
    ang = pos.astype(F32)[:, None] * inv[None, :]
    return jnp.cos(ang), jnp.sin(ang)


def kernel(x_prompt, x_sample, p_prompt, p_sample, state_conv, state_ssm, cache_kv_latent, cache_k_rope, ln_ffn1, w1_gate, w1_up, w1_down, ln_mix, ln_ffn2, w2_gate, w2_up, w2_down, ln_ple, w_ple_in, w_ple_gate, a_in_proj, a_conv_w, a_conv_b, a_dt_bias, a_A_log, a_D, a_norm, a_out_proj, ln_kv, w_kv_a, kv_norm, kr_norm, w_kv_b, k_norm, b_w_dq, b_q_norm, b_w_q_b, b_qn_nope, b_qn_rope, b_w_o):
    depth = ln_ffn1.shape[0]
    n_a = a_in_proj.shape[0]
    assert depth == 2 and n_a == 1 and b_w_dq.shape[0] == 1
    d_model = x_prompt.shape[-1]
    heads = a_dt_bias.shape[-1]
    d_inner = heads * SSM_HEAD_DIM
    conv_dim = a_conv_w.shape[-1]
    kv_lora = kv_norm.shape[0]
    half = QK_ROPE // 2
    bf = lambda w: w.astype(BF16)
    row = lambda v: v.astype(F32)[None, :]

    ffn1 = [(row(ln_ffn1[i]), bf(w1_gate[i]), bf(w1_up[i]), bf(w1_down[i])) for i in range(depth)]
    ffn2 = [(row(ln_ffn2[i]), bf(w2_gate[i]), bf(w2_up[i]), bf(w2_down[i]),
             row(ln_ple[i]), bf(w_ple_in[i]), bf(w_ple_gate[i])) for i in range(depth)]
    w_in = a_in_proj[0]
    mamba_w = dict(
        g=row(ln_mix[0]), wz=bf(w_in[:, :d_inner]), wx=bf(w_in[:, d_inner:d_inner + conv_dim]),
        wdt=bf(w_in[:, d_inner + conv_dim:]), convw=a_conv_w[0], convb=row(a_conv_b[0]),
        dtb=row(a_dt_bias[0]), alog=row(a_A_log[0]), dexp=row(jnp.repeat(a_D[0], SSM_HEAD_DIM)),
        ng=row(a_norm[0]), wout=bf(a_out_proj[0]))

    swap = np.concatenate([np.arange(half, QK_ROPE), np.arange(half)])
    w_r = w_kv_a[:, kv_lora:]
    kv_w = dict(g=row(ln_kv), wc=bf(w_kv_a[:, :kv_lora]), wr=bf(w_r), wrs=bf(w_r[:, swap]),
                gc=row(kv_norm), gr=row(kr_norm), grs=row(kr_norm[swap]))
    per_kv = QK_NOPE + V_HEAD
    ex_w = dict(wk=bf(_head_pad_cols(w_kv_b, per_kv, 0, QK_NOPE)),
                wv=bf(_head_pad_cols(w_kv_b, per_kv, QK_NOPE, V_HEAD)),
                bd=_block_diag_mean([(0, QK_NOPE)]),
                gk=_head_pad_vec([(0, k_norm)]),
                place=jnp.asarray(np.eye(QK_ROPE, LANES, k=QK_NOPE), BF16),
                one=_head_pad_vec([(V_HEAD, jnp.ones((1,), F32))]))
    per_q = QK_NOPE + QK_ROPE
    wqb = b_w_q_b[0]
    wq = _head_pad_cols(wqb, per_q, 0, per_q)
    wqs = (_head_pad_cols(wqb, per_q, QK_NOPE + half, half, offset=QK_NOPE)
           + _head_pad_cols(wqb, per_q, QK_NOPE, half, offset=QK_NOPE + half))
    gq_rope = b_qn_rope[0]
    q_w = dict(g=row(ln_mix[1]), wdq=bf(b_w_dq[0]), gq=row(b_q_norm[0]), wq=bf(wq), wqs=bf(wqs),
               bd=_block_diag_mean([(0, QK_NOPE), (QK_NOPE, QK_ROPE)]),
               gh=_head_pad_vec([(0, b_qn_nope[0]), (QK_NOPE, gq_rope)]),
               ghs=_head_pad_vec([(QK_NOPE, gq_rope[swap])]))
    wo = b_w_o[0].reshape(MLA_HEADS, V_HEAD, d_model)
    wo = bf(jnp.zeros((MLA_HEADS, LANES, d_model), F32).at[:, :V_HEAD].set(wo).reshape(MLA_HEADS * LANES, d_model))

    wkv3 = w_kv_b.reshape(kv_lora, MLA_HEADS, per_kv)
    wq3 = wqb.reshape(wqb.shape[0], MLA_HEADS, per_q)
    flat = lambda w3: bf(w3.reshape(w3.shape[0], -1))
    per_head = lambda v: jnp.tile(v.astype(F32), MLA_HEADS)[None, :]
    flat_w = dict(
        wk=flat(wkv3[:, :, :QK_NOPE]), wv=flat(wkv3[:, :, QK_NOPE:]), gk=per_head(k_norm),
        wn=flat(wq3[:, :, :QK_NOPE]), wr=flat(wq3[:, :, QK_NOPE:]), wrs=flat(wq3[:, :, QK_NOPE:][:, :, swap]),
        bdn=_block_diag_mean([(o, QK_NOPE) for o in range(0, LANES, QK_NOPE)]),
        bdr=_block_diag_mean([(o, QK_ROPE) for o in range(0, LANES, QK_ROPE)]),
        gn=per_head(b_qn_nope[0]), gr=per_head(gq_rope), grs=per_head(gq_rope[swap]), wo=bf(b_w_o[0]))

    def rope_tabs(pos):
        cos, sin = _rope_tables(pos)
        seq = pos.shape[0]
        cos32 = jnp.concatenate([cos, cos], axis=1)
        sin32 = jnp.concatenate([-sin, sin], axis=1)
        c128 = jnp.concatenate([jnp.ones((seq, QK_NOPE), F32), cos32,
                                jnp.zeros((seq, LANES - per_q), F32)], axis=1)
        s128 = jnp.concatenate([jnp.zeros((seq, QK_NOPE), F32), sin32,
                                jnp.zeros((seq, LANES - per_q), F32)], axis=1)
        return cos32, sin32, c128, s128

    def run_group(x, p, pos0, conv_in, ssm_in, past_ckv, past_kr):
        batch, seq, _ = x.shape
        m = batch * seq
        pos = pos0 + jnp.arange(seq, dtype=jnp.int32)
        cos32, sin32, c128, s128 = rope_tabs(pos)
        h = x.reshape(m, d_model)
        p2 = p.reshape(depth, m, p.shape[-1])

        h = _ffn(h, *ffn1[0])
        mw = mamba_w
        z, xbc, dt = _inproj(h, mw["g"], mw["wz"], mw["wx"], mw["wdt"])
        conv8 = jnp.pad(conv_in, ((0, 0), (SUBLANES - (CONV_W - 1), 0), (0, 0)))
        ssm4 = ssm_in.reshape(batch, SSM_GROUPS, d_inner // SSM_GROUPS, D_STATE)
        h, conv_out, ssm_out = _mamba(z, xbc, dt, h, conv8, ssm4, mw["convw"], mw["convb"], mw["dtb"],
                                      mw["alog"], mw["dexp"], mw["ng"], mw["wout"], batch, seq)
        conv_out = conv_out[None, :, SUBLANES - (CONV_W - 1):, :]
        ssm_out = ssm_out.reshape(1, batch, heads, SSM_HEAD_DIM, D_STATE)
        h = _ffn_ple(h, p2[0], *ffn2[0])

        kw = kv_w
        ckv_new, kr_new = _kvside(h, kw["g"], kw["wc"], kw["wr"], kw["wrs"], kw["gc"], kw["gr"], kw["grs"],
                                  cos32, sin32, seq)

        h = _ffn(h, *ffn1[1])
        qw = q_w
        if past_ckv is None:
            ew = ex_w
            k_all, v_all = _expand(ckv_new, kr_new, ew["wk"], ew["wv"], ew["bd"], ew["gk"], ew["place"], ew["one"])
            q = _queries(h, qw["g"], qw["wdq"], qw["gq"], qw["wq"], qw["wqs"], qw["bd"], qw["gh"], qw["ghs"],
                         c128, s128, seq)
            width = MLA_HEADS * LANES
            h = _self_attention(q.reshape(batch, seq, width), k_all.reshape(batch, seq, width),
                                v_all.reshape(batch, seq, width), h.reshape(batch, seq, d_model), wo, batch)
        else:
            fw = flat_w
            cos_f, sin_f = jnp.tile(cos32, (1, MLA_HEADS)), jnp.tile(sin32, (1, MLA_HEADS))
            qn, qr = _queries_flat(h, qw["g"], qw["wdq"], qw["gq"], fw["wn"], fw["wr"], fw["wrs"], fw["bdn"],
                                   fw["bdr"], fw["gn"], fw["gr"], fw["grs"], cos_f, sin_f, seq)
            h = _cache_attention(qn.reshape(batch, seq, -1), qr.reshape(batch, seq, -1), past_ckv, past_kr,
                                 ckv_new.reshape(batch, seq, kv_lora), kr_new.reshape(batch, seq, QK_ROPE),
                                 h.reshape(batch, seq, d_model), fw["wk"], fw["wv"], fw["bdn"], fw["gk"], fw["wo"])
        h = _ffn_ple(h.reshape(m, d_model), p2[1], *ffn2[1])
        return (h.reshape(batch, seq, d_model), conv_out, ssm_out,
                ckv_new.reshape(batch, seq, kv_lora), kr_new.reshape(batch, seq, QK_ROPE))

    b_p = x_prompt.shape[0]
    conv0 = jnp.zeros((b_p, CONV_W - 1, conv_dim), F32)
    ssm0 = jnp.zeros((b_p, heads, SSM_HEAD_DIM, D_STATE), F32)
    y_p, conv_p, ssm_p, kv_p, kr_p = run_group(x_prompt, p_prompt, 0, conv0, ssm0, None, None)
    y_s, conv_s, ssm_s, kv_s, kr_s = run_group(x_sample, p_sample, cache_kv_latent.shape[1],
                                               state_conv[0], state_ssm[0], cache_kv_latent, cache_k_rope)
    return (y_p, y_s, conv_p, ssm_p, kv_p, kr_p, conv_s, ssm_s, kv_s, kr_s)
```

```python
import functools

import numpy as np
import jax
import jax.numpy as jnp
from jax import lax
from jax.experimental import pallas as pl
from jax.experimental.pallas import tpu as pltpu

F32 = jnp.float32
BF16 = jnp.bfloat16

EPS = 1e-6
CHUNK = 64
SSM_HEAD_DIM = 64
SSM_GROUPS = 4
D_STATE = 128
CONV_W = 4
MLA_HEADS = 16
QK_NOPE = 64
QK_ROPE = 32
V_HEAD = 64
ROPE_BASE = 10000.0
ATTN_SCALE = (QK_NOPE + QK_ROPE) ** -0.5
EXP2_SCALE = ATTN_SCALE * float(np.log2(np.e))

LANES = 128
VT_ROWS = V_HEAD + 16
SUBLANES = 8
MXU_DIM = 256
VMEM_LIMIT_BYTES = 56 * 1024 * 1024
ROW_TILE = 512
ATTN_TILE = 256
MAMBA_TILE = 256
CONV_ROWS = 32
GATE_ROWS = 64


def _cparams(n_grid):
    return pltpu.CompilerParams(dimension_semantics=("arbitrary",) * n_grid,
                                vmem_limit_bytes=VMEM_LIMIT_BYTES)


def _const_spec(shape):
    nd = len(shape)
    return pl.BlockSpec(shape, lambda *_: (0,) * nd, pipeline_mode=pl.Buffered(1))


def _row_spec(tm, width):
    return pl.BlockSpec((tm, width), lambda i: (i, 0))


def _row_tile(m):
    tm = min(ROW_TILE, m)
    assert m % tm == 0 and tm % SUBLANES == 0
    return tm


def _dot(a, b):
    return jnp.dot(a, b, preferred_element_type=F32)


def _dot_nt(a, b):
    return lax.dot_general(a, b, (((1,), (1,)), ((), ())), preferred_element_type=F32)


def _dot_tn(a, b):
    return lax.dot_general(a, b, (((0,), (0,)), ((), ())), preferred_element_type=F32)


def _rms(x, g):
    ms = jnp.mean(x * x, axis=-1, keepdims=True)
    return x * lax.rsqrt(ms + EPS) * g


def _silu(x):
    return x * jax.nn.sigmoid(x)


def _softplus(x):
    return jnp.maximum(x, 0.0) + jnp.log1p(jnp.exp(-jnp.abs(x)))


def _split_bf16(v, n):
    pieces = []
    r = v
    for _ in range(n):
        p = r.astype(BF16)
        pieces.append(p)
        r = r - p.astype(F32)
    return pieces


def _ffn_body(h, g, wg_ref, wu_ref, wd_ref):
    u = _rms(h, g).astype(BF16)
    a = _dot(u, wg_ref[...])
    b = _dot(u, wu_ref[...])
    act = (_silu(a) * b).astype(BF16)
    return h + 0.5 * _dot(act, wd_ref[...])


def _ffn_kernel(h_ref, g_ref, wg_ref, wu_ref, wd_ref, o_ref):
    o_ref[...] = _ffn_body(h_ref[...], g_ref[...], wg_ref, wu_ref, wd_ref)


def _ffn_ple_kernel(h_ref, p_ref, g_ref, wg_ref, wu_ref, wd_ref, gp_ref, wpi_ref, wpg_ref, o_ref):
    h2 = _ffn_body(h_ref[...], g_ref[...], wg_ref, wu_ref, wd_ref)
    gate = jax.nn.sigmoid(_dot(_rms(h2, gp_ref[...]).astype(BF16), wpg_ref[...]))
    o_ref[...] = h2 + _dot(p_ref[...].astype(BF16), wpi_ref[...]) * gate


def _ffn(h, g, wg, wu, wd):
    m, d = h.shape
    tm = _row_tile(m)
    return pl.pallas_call(
        _ffn_kernel,
        grid=(m // tm,),
        in_specs=[_row_spec(tm, d), _const_spec(g.shape), _const_spec(wg.shape),
                  _const_spec(wu.shape), _const_spec(wd.shape)],
        out_specs=_row_spec(tm, d),
        out_shape=jax.ShapeDtypeStruct((m, d), F32),
        compiler_params=_cparams(1),
    )(h, g, wg, wu, wd)


def _ffn_ple(h, p, g, wg, wu, wd, gp, wpi, wpg):
    m, d = h.shape
    tm = _row_tile(m)
    return pl.pallas_call(
        _ffn_ple_kernel,
        grid=(m // tm,),
        in_specs=[_row_spec(tm, d), _row_spec(tm, p.shape[1]), _const_spec(g.shape),
                  _const_spec(wg.shape), _const_spec(wu.shape), _const_spec(wd.shape),
                  _const_spec(gp.shape), _const_spec(wpi.shape), _const_spec(wpg.shape)],
        out_specs=_row_spec(tm, d),
        out_shape=jax.ShapeDtypeStruct((m, d), F32),
        compiler_params=_cparams(1),
    )(h, p, g, wg, wu, wd, gp, wpi, wpg)


def _inproj_kernel(h_ref, g_ref, wz_ref, wx_ref, wdt_ref, z_ref, xbc_ref, dt_ref):
    u = _rms(h_ref[...], g_ref[...]).astype(BF16)
    z_ref[...] = _dot(u, wz_ref[...])
    xbc_ref[...] = _dot(u, wx_ref[...])
    dt_ref[...] = _dot(u, wdt_ref[...])


def _inproj(h, g, wz, wx, wdt):
    m, d = h.shape
    tm = _row_tile(m)
    widths = (wz.shape[1], wx.shape[1], wdt.shape[1])
    return pl.pallas_call(
        _inproj_kernel,
        grid=(m // tm,),
        in_specs=[_row_spec(tm, d), _const_spec(g.shape), _const_spec(wz.shape),
                  _const_spec(wx.shape), _const_spec(wdt.shape)],
        out_specs=[_row_spec(tm, w) for w in widths],
        out_shape=[jax.ShapeDtypeStruct((m, w), F32) for w in widths],
        compiler_params=_cparams(1),
    )(h, g, wz, wx, wdt)


def _mamba_kernel(chunk, z_ref, xbc_ref, dt_ref, h_ref, cst_ref, sst_ref,
                  convw_ref, convb_ref, dtb_ref, alog_ref, dexp_ref, ng_ref, wout_ref,
                  blt_ref, e2_ref, i2_ref,
                  o_ref, cout_ref, sout_ref,
                  xpad, st, x_s, bc_s, ae_s, de_s, y_s, at_s, yn_s):
    t = pl.program_id(1)
    n_t = pl.num_programs(1)
    tb = xbc_ref.shape[0]
    d_inner = x_s.shape[1]
    gn = SSM_GROUPS * D_STATE
    n_chunks = tb // chunk
    gw = d_inner // SSM_GROUPS
    half = LANES // 2
    pad_rows = half - chunk

    @pl.when(t == 0)
    def _():
        xpad[0:SUBLANES, :] = cst_ref[0]
        for g in range(SSM_GROUPS):
            st[g] = sst_ref[0, g].T

    xpad[SUBLANES:SUBLANES + tb, :] = xbc_ref[...]
    w = convw_ref[...]
    base = SUBLANES - (CONV_W - 1)
    rb = min(CONV_ROWS, tb)
    row8 = lax.broadcasted_iota(jnp.int32, (SUBLANES, xpad.shape[1]), 0)

    def conv_rows(r, carry):
        r0 = pl.multiple_of(r * rb, rb)
        ext = xpad[pl.ds(r0, rb + SUBLANES), :]
        slabs = [ext[s:s + SUBLANES, :] for s in range(0, rb + SUBLANES, SUBLANES)]
        ys = [convb_ref[...] + sl * w[CONV_W - 1:CONV_W, :] for sl in slabs[1:]]
        for k in range(CONV_W - 1):
            back = CONV_W - 1 - k
            rolled = [pltpu.roll(sl, back, axis=0) for sl in slabs]
            for n in range(len(ys)):
                ys[n] = ys[n] + jnp.where(row8 < back, rolled[n], rolled[n + 1]) * w[k:k + 1, :]
        y = jnp.concatenate(ys, axis=0)
        act = _silu(y)
        x_s[pl.ds(r0, rb), :] = act[:, :d_inner]
        bc_s[pl.ds(r0, rb), :] = act[:, d_inner:]
        return carry

    lax.fori_loop(0, tb // rb, conv_rows, 0)
    tail = xpad[tb:tb + SUBLANES, :]
    cout_ref[0] = tail
    xpad[0:SUBLANES, :] = tail

    dt = _softplus(dt_ref[...] + dtb_ref[...])
    a = dt * (-jnp.exp(alog_ref[...]))
    blt = blt_ref[...]
    acs = sum(_dot(blt, p) for p in _split_bf16(a, 3))
    e2 = e2_ref[...]
    ae_s[...] = _dot(jnp.concatenate(_split_bf16(acs, 2), axis=1), e2)
    de_s[...] = _dot(jnp.concatenate(_split_bf16(dt, 2), axis=1), e2)
    i2 = i2_ref[...]
    at_all = sum(_dot_tn(p, i2) for p in _split_bf16(acs, 3))
    for c in range(n_chunks):
        at_s[c] = at_all[:, c * LANES:(c + 1) * LANES]

    lane = lax.broadcasted_iota(jnp.int32, (chunk, LANES), 1)
    row = lax.broadcasted_iota(jnp.int32, (chunk, LANES), 0)
    low_half = lane < half
    src = lane & (half - 1)
    causal = (src <= row) & (src < chunk)
    dexp = dexp_ref[...]

    def stack_pair(top, bot):
        if pad_rows == 0:
            return jnp.concatenate([top, bot], axis=0)
        zeros = jnp.zeros((pad_rows, LANES), top.dtype)
        return jnp.concatenate([top, zeros, bot, zeros], axis=0)

    def chunk_body(c, carry):
        rows = pl.ds(c * chunk, chunk)
        ae = ae_s[rows, :]
        xc = x_s[rows, :]
        last = ae[chunk - 1:chunk, :]
        xdt = xc * de_s[rows, :]
        xw = (xdt * jnp.exp(last - ae)).astype(BF16)
        xdt_b = xdt.astype(BF16)
        e_acs = jnp.exp(ae)
        decay = jnp.exp(last)
        bb = bc_s[rows, 0:gn].astype(BF16)
        cb = bc_s[rows, gn:2 * gn].astype(BF16)
        at2 = at_s[c]
        for g in range(SSM_GROUPS):
            gl = slice(g * gw, (g + 1) * gw)
            c_g = cb[:, g * D_STATE:(g + 1) * D_STATE]
            b_g = bb[:, g * D_STATE:(g + 1) * D_STATE]
            cb2 = _dot_nt(c_g, stack_pair(b_g, b_g))
            s_g = st[g]
            y_off = _dot(c_g, s_g.astype(BF16)) * e_acs[:, gl]
            pairs = []
            for q in range(gw // LANES):
                pq = g * (gw // LANES) + q
                pl_ = slice(pq * LANES, (pq + 1) * LANES)
                rowvec = jnp.where(low_half[0:1, :], at2[2 * pq:2 * pq + 1, :], at2[2 * pq + 1:2 * pq + 2, :])
                seg = jnp.exp(jnp.where(causal, ae[:, pl_] - rowvec, -jnp.inf))
                wmat = (cb2 * seg).astype(BF16)
                xp = xdt_b[:, pl_]
                zero = jnp.zeros_like(xp)
                x2 = stack_pair(jnp.where(low_half, xp, zero), jnp.where(low_half, zero, xp))
                pairs.append(_dot(wmat, x2))
            y_diag = jnp.concatenate(pairs, axis=1)
            y_s[rows, gl] = y_diag + y_off + dexp[:, gl] * xc[:, gl]
            st[g] = s_g * decay[:, gl] + _dot_tn(b_g, xw[:, gl])
        return carry

    for c in range(n_chunks):
        chunk_body(c, 0)

    gb = min(GATE_ROWS, tb)

    def gate_rows(r, carry):
        rows = pl.ds(pl.multiple_of(r * gb, gb), gb)
        yz = y_s[rows, :] * _silu(z_ref[rows, :])
        yn_s[rows, :] = _rms(yz, ng_ref[...]).astype(BF16)
        return carry

    lax.fori_loop(0, tb // gb, gate_rows, 0)
    o_ref[...] = h_ref[...] + _dot(yn_s[...], wout_ref[...])

    @pl.when(t == n_t - 1)
    def _():
        for g in range(SSM_GROUPS):
            sout_ref[0, g] = st[g].T


def _mamba(z, xbc, dt, h, conv_state8, ssm_state, convw, convb, dtb, alog, dexp, ng, wout, batch, seq):
    m, d_inner = z.shape
    conv_dim = xbc.shape[1]
    heads = dt.shape[1]
    d = h.shape[1]
    tb = min(MAMBA_TILE, seq)
    chunk = min(CHUNK, seq)
    assert seq % tb == 0 and tb % chunk == 0 and chunk % SUBLANES == 0 and LANES // 2 % chunk == 0
    n_t = seq // tb
    n_chunks = tb // chunk
    gw = d_inner // SSM_GROUPS

    r = np.arange(tb)
    blt = ((r[:, None] // chunk == r[None, :] // chunk) & (r[None, :] <= r[:, None])).astype(np.float32)
    e = (np.arange(d_inner)[None, :] // SSM_HEAD_DIM == np.arange(heads)[:, None]).astype(np.float32)
    e = np.concatenate([e, e], axis=0)
    col = np.arange(n_chunks * LANES)
    i2 = ((col[None, :] // LANES == r[:, None] // chunk)
          & (col[None, :] % (LANES // 2) == r[:, None] % chunk)).astype(np.float32)
    blt, e, i2 = (jnp.asarray(v, BF16) for v in (blt, e, i2))

    row_map = lambda b, t: (b * n_t + t, 0)
    const = lambda v: _const_spec(v.shape)
    return pl.pallas_call(
        functools.partial(_mamba_kernel, chunk),
        grid=(batch, n_t),
        in_specs=[pl.BlockSpec((tb, d_inner), row_map), pl.BlockSpec((tb, conv_dim), row_map),
                  pl.BlockSpec((tb, heads), row_map), pl.BlockSpec((tb, d), row_map),
                  pl.BlockSpec((1, SUBLANES, conv_dim), lambda b, t: (b, 0, 0)),
                  pl.BlockSpec((1, SSM_GROUPS, gw, D_STATE), lambda b, t: (b, 0, 0, 0)),
                  const(convw), const(convb), const(dtb), const(alog), const(dexp), const(ng), const(wout),
                  const(blt), const(e), const(i2)],
        out_specs=[pl.BlockSpec((tb, d), row_map),
                   pl.BlockSpec((1, SUBLANES, conv_dim), lambda b, t: (b, 0, 0)),
                   pl.BlockSpec((1, SSM_GROUPS, gw, D_STATE), lambda b, t: (b, 0, 0, 0))],
        out_shape=[jax.ShapeDtypeStruct((m, d), F32),
                   jax.ShapeDtypeStruct((batch, SUBLANES, conv_dim), F32),
                   jax.ShapeDtypeStruct((batch, SSM_GROUPS, gw, D_STATE), F32)],
        scratch_shapes=[pltpu.VMEM((tb + SUBLANES, conv_dim), F32),
                        pltpu.VMEM((SSM_GROUPS, D_STATE, gw), F32),
                        pltpu.VMEM((tb, d_inner), F32),
                        pltpu.VMEM((tb, conv_dim - d_inner), F32),
                        pltpu.VMEM((tb, d_inner), F32),
                        pltpu.VMEM((tb, d_inner), F32),
                        pltpu.VMEM((tb, d_inner), F32),
                        pltpu.VMEM((n_chunks, heads, LANES), F32),
                        pltpu.VMEM((tb, d_inner), BF16)],
        compiler_params=_cparams(2),
    )(z, xbc, dt, h, conv_state8, ssm_state, convw, convb, dtb, alog, dexp, ng, wout, blt, e, i2)


def _table_spec(tm, seq, width):
    if seq <= tm:
        return pl.BlockSpec((tm, width), lambda i: (0, 0))
    per = seq // tm
    return pl.BlockSpec((tm, width), lambda i: (i % per, 0))


def _tile_table(tab, tm, seq):
    return jnp.tile(tab, (tm // seq, 1)) if seq < tm else tab


def _kvside_kernel(h_ref, g_ref, wc_ref, wr_ref, wrs_ref, gc_ref, gr_ref, grs_ref, cos_ref, sin_ref,
                   ckv_ref, kr_ref):
    u = _rms(h_ref[...], g_ref[...]).astype(BF16)
    ckv_ref[...] = _rms(_dot(u, wc_ref[...]), gc_ref[...])
    r = _dot(u, wr_ref[...])
    rs = _dot(u, wrs_ref[...])
    inv = lax.rsqrt(jnp.mean(r * r, axis=-1, keepdims=True) + EPS)
    kr_ref[...] = r * inv * gr_ref[...] * cos_ref[...] + rs * inv * grs_ref[...] * sin_ref[...]


def _kvside(h, g, wc, wr, wrs, gc, gr, grs, cos32, sin32, seq):
    m, d = h.shape
    tm = _row_tile(m)
    assert tm % seq == 0 or seq % tm == 0
    cos32, sin32 = _tile_table(cos32, tm, seq), _tile_table(sin32, tm, seq)
    consts = (g, wc, wr, wrs, gc, gr, grs)
    return pl.pallas_call(
        _kvside_kernel,
        grid=(m // tm,),
        in_specs=[_row_spec(tm, d)] + [_const_spec(v.shape) for v in consts]
                 + [_table_spec(tm, seq, QK_ROPE)] * 2,
        out_specs=[_row_spec(tm, wc.shape[1]), _row_spec(tm, QK_ROPE)],
        out_shape=[jax.ShapeDtypeStruct((m, wc.shape[1]), F32), jax.ShapeDtypeStruct((m, QK_ROPE), F32)],
        compiler_params=_cparams(1),
    )(h, *consts, cos32, sin32)


def _group_mean_sq(x, bd_ref):
    bd = bd_ref[...]
    sq = (x * x).astype(BF16)
    return jnp.concatenate([_dot(sq[:, s:s + MXU_DIM], bd) for s in range(0, x.shape[1], MXU_DIM)], axis=1)


def _expand_kernel(ckv_ref, kr_ref, wk_ref, wvt_ref, bd_ref, gk_ref, place_ref, one_ref, k_ref, vt_ref):
    c = ckv_ref[...].astype(BF16)
    k = _dot(c, wk_ref[...])
    kn = k * lax.rsqrt(_group_mean_sq(k, bd_ref) + EPS) * gk_ref[...]
    kr = _dot(kr_ref[...].astype(BF16), place_ref[...])
    for hh in range(MLA_HEADS):
        hl = slice(hh * LANES, (hh + 1) * LANES)
        k_ref[:, hl] = (kn[:, hl] + kr).astype(BF16)
    vt = _dot_nt(wvt_ref[...], c)
    one = one_ref[...]
    tk = vt_ref.shape[3]
    for j in range(vt_ref.shape[1]):
        for t in range(0, tk, LANES):
            vt_ref[0, j, :, t:t + LANES] = (vt[:, j * tk + t:j * tk + t + LANES] + one).astype(BF16)


def _expand(ckv, kr, wk, wvt, bd, gk, place, one, seq, tk):
    m = ckv.shape[0]
    tm = min(_row_tile(m), seq)
    assert seq % tm == 0 and tm % tk == 0
    per = seq // tm
    consts = (wk, wvt, bd, gk, place, one)
    width = MLA_HEADS * LANES
    return pl.pallas_call(
        _expand_kernel,
        grid=(m // tm,),
        in_specs=[_row_spec(tm, ckv.shape[1]), _row_spec(tm, kr.shape[1])]
                 + [_const_spec(v.shape) for v in consts],
        out_specs=[_row_spec(tm, width),
                   pl.BlockSpec((1, tm // tk, wvt.shape[0], tk), lambda i: (i // per, i % per, 0, 0))],
        out_shape=[jax.ShapeDtypeStruct((m, width), BF16),
                   jax.ShapeDtypeStruct((m // seq, seq // tk, wvt.shape[0], tk), BF16)],
        compiler_params=_cparams(1),
    )(ckv, kr, *consts)


def _q_kernel(h_ref, g_ref, wdq_ref, gq_ref, wq_ref, wqs_ref, bd_ref, gh_ref, ghs_ref, c_ref, s_ref, q_ref):
    u = _rms(h_ref[...], g_ref[...]).astype(BF16)
    cq = _rms(_dot(u, wdq_ref[...]), gq_ref[...]).astype(BF16)
    q = _dot(cq, wq_ref[...])
    qs = _dot(cq, wqs_ref[...])
    inv = lax.rsqrt(_group_mean_sq(q, bd_ref) + EPS)
    qn = q * inv * gh_ref[...]
    qsn = qs * inv * ghs_ref[...]
    c = c_ref[...]
    s = s_ref[...]
    for hh in range(MLA_HEADS):
        hl = slice(hh * LANES, (hh + 1) * LANES)
        q_ref[:, hl] = (qn[:, hl] * c + qsn[:, hl] * s).astype(BF16)


def _queries(h, g, wdq, gq, wq, wqs, bd, gh, ghs, c128, s128, seq):
    m, d = h.shape
    tm = _row_tile(m)
    assert tm % seq == 0 or seq % tm == 0
    c128, s128 = _tile_table(c128, tm, seq), _tile_table(s128, tm, seq)
    consts = (g, wdq, gq, wq, wqs, bd, gh, ghs)
    width = MLA_HEADS * LANES
    return pl.pallas_call(
        _q_kernel,
        grid=(m // tm,),
        in_specs=[_row_spec(tm, d)] + [_const_spec(v.shape) for v in consts]
                 + [_table_spec(tm, seq, LANES)] * 2,
        out_specs=_row_spec(tm, width),
        out_shape=jax.ShapeDtypeStruct((m, width), BF16),
        compiler_params=_cparams(1),
    )(h, *consts, c128, s128)


def _q_flat_kernel(h_ref, g_ref, wdq_ref, gq_ref, wn_ref, wr_ref, wrs_ref, bdn_ref, bdr_ref,
                   gn_ref, gr_ref, grs_ref, cos_ref, sin_ref, qn_ref, qr_ref):
    u = _rms(h_ref[...], g_ref[...]).astype(BF16)
    cq = _rms(_dot(u, wdq_ref[...]), gq_ref[...]).astype(BF16)
    qn = _dot(cq, wn_ref[...])
    qn_ref[...] = (qn * lax.rsqrt(_group_mean_sq(qn, bdn_ref) + EPS) * gn_ref[...]).astype(BF16)
    r = _dot(cq, wr_ref[...])
    rs = _dot(cq, wrs_ref[...])
    inv = lax.rsqrt(_group_mean_sq(r, bdr_ref) + EPS)
    qr_ref[...] = (r * inv * gr_ref[...] * cos_ref[...] + rs * inv * grs_ref[...] * sin_ref[...]).astype(BF16)


def _queries_flat(h, g, wdq, gq, wn, wr, wrs, bdn, bdr, gn, gr, grs, cos, sin, seq):
    m, d = h.shape
    tm = _row_tile(m)
    assert tm % seq == 0 or seq % tm == 0
    cos, sin = _tile_table(cos, tm, seq), _tile_table(sin, tm, seq)
    consts = (g, wdq, gq, wn, wr, wrs, bdn, bdr, gn, gr, grs)
    widths = (wn.shape[1], wr.shape[1])
    return pl.pallas_call(
        _q_flat_kernel,
        grid=(m // tm,),
        in_specs=[_row_spec(tm, d)] + [_const_spec(v.shape) for v in consts]
                 + [_table_spec(tm, seq, widths[1])] * 2,
        out_specs=[_row_spec(tm, w) for w in widths],
        out_shape=[jax.ShapeDtypeStruct((m, w), BF16) for w in widths],
        compiler_params=_cparams(1),
    )(h, *consts, cos, sin)


def _cache_attn_kernel(kb, qn_ref, qr_ref, ckv_ref, kr_ref, ckvn_ref, krn_ref, h_ref,
                       wk_ref, wv_ref, bd_ref, gk_ref, wo_ref, o_ref, s_s, v_s, acc_s):
    tq = qn_ref.shape[1]
    past = ckv_ref.shape[1]
    cols = MLA_HEADS * tq
    chunk_shift = CHUNK.bit_length() - 1
    assert CHUNK == 1 << chunk_shift and past % kb == 0

    qn = jnp.concatenate([qn_ref[0]] * MLA_HEADS, axis=0)
    tq_shift, nope_shift = tq.bit_length() - 1, QK_NOPE.bit_length() - 1
    assert tq == 1 << tq_shift and QK_NOPE == 1 << nope_shift
    r_head = lax.broadcasted_iota(jnp.int32, qn.shape, 0) >> tq_shift
    c_head = lax.broadcasted_iota(jnp.int32, qn.shape, 1) >> nope_shift
    q_bd = jnp.where(r_head == c_head, qn, jnp.zeros_like(qn))
    qr_all = qr_ref[0]
    q_rope = jnp.concatenate([qr_all[:, hh * QK_ROPE:(hh + 1) * QK_ROPE] for hh in range(MLA_HEADS)], axis=0)

    def expand(c, kr, rows):
        c = c.astype(BF16)
        k = _dot(c, wk_ref[...])
        kn = (k * lax.rsqrt(_group_mean_sq(k, bd_ref) + EPS) * gk_ref[...]).astype(BF16)
        s = _dot_nt(kn, q_bd) + _dot_nt(kr.astype(BF16), q_rope)
        v = _dot(c, wv_ref[...]).astype(BF16)
        v_s[rows, :] = jnp.concatenate([v, jnp.ones((v.shape[0], LANES), BF16)], axis=1)
        return s

    def expand_block(j, m_run):
        rows = pl.ds(pl.multiple_of(j * kb, kb), kb)
        s = expand(ckv_ref[0, rows, :], kr_ref[0, rows, :], rows)
        s_s[rows, :] = s
        return jnp.maximum(m_run, jnp.max(s, axis=0, keepdims=True))

    m_run = lax.fori_loop(0, past // kb, expand_block, jnp.full((1, cols), -jnp.inf, F32))
    new_rows = pl.ds(past, tq)
    s_new = expand(ckvn_ref[0], krn_ref[0], new_rows)
    k_chunk = (past + lax.broadcasted_iota(jnp.int32, (tq, cols), 0)) >> chunk_shift
    q_chunk = (past + (lax.broadcasted_iota(jnp.int32, (tq, cols), 1) & (tq - 1))) >> chunk_shift
    s_new = jnp.where(k_chunk <= q_chunk, s_new, -jnp.inf)
    m_all = jnp.maximum(m_run, jnp.max(s_new, axis=0, keepdims=True))

    acc_s[...] = _dot_tn(jnp.exp2(s_new - m_all).astype(BF16), v_s[new_rows, :])

    def pv_block(j, carry):
        rows = pl.ds(pl.multiple_of(j * kb, kb), kb)
        p = jnp.exp2(s_s[rows, :] - m_all).astype(BF16)
        acc_s[...] += _dot_tn(p, v_s[rows, :])
        return carry

    lax.fori_loop(0, past // kb, pv_block, 0)

    width = MLA_HEADS * V_HEAD
    low_half = lax.broadcasted_iota(jnp.int32, (tq, LANES), 1) < V_HEAD
    tiles = []
    for c in range(width // LANES):
        ev = acc_s[2 * c * tq:(2 * c + 1) * tq, :]
        od = acc_s[(2 * c + 1) * tq:(2 * c + 2) * tq, :]
        cl = slice(c * LANES, (c + 1) * LANES)
        tiles.append(jnp.where(low_half, ev[:, cl] / ev[:, width:], od[:, cl] / od[:, width:]))
    o = jnp.concatenate(tiles, axis=1).astype(BF16)
    o_ref[0] = h_ref[0] + _dot(o, wo_ref[...])


def _cache_attention(qn, qr, ckv, kr, ckv_new, kr_new, h, wk, wv, bd, gk, wo):
    batch, tq, _ = qn.shape
    past = ckv.shape[1]
    d = h.shape[2]
    kb = min(ROW_TILE, past)
    assert past % kb == 0 and tq % 16 == 0 and (MLA_HEADS * tq) % LANES == 0
    per = lambda a: pl.BlockSpec((1,) + a.shape[1:], lambda b: (b, 0, 0))
    consts = (wk, wv, bd, gk, wo)
    return pl.pallas_call(
        functools.partial(_cache_attn_kernel, kb),
        grid=(batch,),
        in_specs=[per(qn), per(qr), per(ckv), per(kr), per(ckv_new), per(kr_new), per(h)]
                 + [_const_spec(v.shape) for v in consts],
        out_specs=per(h),
        out_shape=jax.ShapeDtypeStruct(h.shape, F32),
        scratch_shapes=[pltpu.VMEM((past + tq, MLA_HEADS * tq), F32),
                        pltpu.VMEM((past + tq, MLA_HEADS * V_HEAD + LANES), BF16),
                        pltpu.VMEM((MLA_HEADS * tq, MLA_HEADS * V_HEAD + LANES), F32)],
        compiler_params=_cparams(1),
    )(qn, qr, ckv, kr, ckv_new, kr_new, h, *consts)


def _self_attn_kernel(q_ref, k_ref, vt_ref, h_ref, wo_ref, o_ref, m_s, acc_s, ot_s, st_s):
    i = pl.program_id(1)
    tq = q_ref.shape[1]
    chunk_shift = CHUNK.bit_length() - 1
    assert CHUNK == 1 << chunk_shift
    visible = ((lax.broadcasted_iota(jnp.int32, (tq, tq), 0) >> chunk_shift)
               <= (lax.broadcasted_iota(jnp.int32, (tq, tq), 1) >> chunk_shift))

    m_s[...] = jnp.full(m_s.shape, -jnp.inf, F32)
    acc_s[...] = jnp.zeros(acc_s.shape, F32)

    def step(j, masked):
        k0 = pl.multiple_of(j * tq, tq)
        for hh in range(MLA_HEADS):
            hl = slice(hh * LANES, (hh + 1) * LANES)
            st_s[hh] = _dot_nt(k_ref[0, pl.ds(k0, tq), hl], q_ref[0, :, hl])
        for hh in range(MLA_HEADS):
            st = st_s[hh]
            if masked:
                st = jnp.where(visible, st, -jnp.inf)
            m_old = m_s[hh]
            m_new = jnp.maximum(m_old, jnp.max(st, axis=0, keepdims=True))
            alpha = jnp.exp2(m_old - m_new)
            pt = jnp.exp2(st - m_new).astype(BF16)
            vt = vt_ref[0, j, hh * VT_ROWS:(hh + 1) * VT_ROWS, :]
            acc_s[hh] = alpha * acc_s[hh] + _dot(vt, pt)
            m_s[hh] = m_new

    def full_step(j, carry):
        step(j, False)
        return carry

    lax.fori_loop(0, i, full_step, 0)
    step(i, True)

    for hh in range(MLA_HEADS):
        acc = acc_s[hh]
        ot_s[hh * VT_ROWS:(hh + 1) * VT_ROWS, :] = (acc / acc[V_HEAD:V_HEAD + 1, :]).astype(BF16)
    o_ref[0] = h_ref[0] + _dot_tn(ot_s[...], wo_ref[...])


def _self_attention(q, k, vt, h, wo, batch):
    width = q.shape[2]
    seq = q.shape[1]
    d = h.shape[2]
    tq = vt.shape[3]
    vrows = vt.shape[2]
    assert seq % tq == 0 and tq % CHUNK == 0 and vt.shape[1] * tq == seq and vrows == MLA_HEADS * VT_ROWS
    return pl.pallas_call(
        _self_attn_kernel,
        grid=(batch, seq // tq),
        in_specs=[pl.BlockSpec((1, tq, width), lambda b, i: (b, i, 0)),
                  pl.BlockSpec((1, seq, width), lambda b, i: (b, 0, 0)),
                  pl.BlockSpec((1, seq // tq, vrows, tq), lambda b, i: (b, 0, 0, 0)),
                  pl.BlockSpec((1, tq, d), lambda b, i: (b, i, 0)),
                  _const_spec(wo.shape)],
        out_specs=pl.BlockSpec((1, tq, d), lambda b, i: (b, i, 0)),
        out_shape=jax.ShapeDtypeStruct((batch, seq, d), F32),
        scratch_shapes=[pltpu.VMEM((MLA_HEADS, 1, tq), F32), pltpu.VMEM((MLA_HEADS, VT_ROWS, tq), F32),
                        pltpu.VMEM((vrows, tq), BF16), pltpu.VMEM((MLA_HEADS, tq, tq), F32)],
        compiler_params=_cparams(2),
    )(q, k, vt, h, wo)


def _head_pad_cols(w, per_head, start, count, offset=0):
    k = w.shape[0]
    w3 = w.reshape(k, MLA_HEADS, per_head)[:, :, start:start + count]
    out = jnp.zeros((k, MLA_HEADS, LANES), w.dtype)
    out = out.at[:, :, offset:offset + count].set(w3)
    return out.reshape(k, MLA_HEADS * LANES)


def _head_rows(w3):
    h, r, n = w3.shape
    return jnp.zeros((h, VT_ROWS, n), w3.dtype).at[:, :r].set(w3).reshape(h * VT_ROWS, n)


def _head_pad_vec(pieces):
    blk = jnp.zeros((LANES,), F32)
    for off, vec in pieces:
        blk = blk.at[off:off + vec.shape[0]].set(vec.astype(F32))
    return jnp.tile(blk, MLA_HEADS)[None, :]


def _block_diag_mean(groups):
    bd = np.zeros((MXU_DIM, MXU_DIM), np.float32)
    for tile in range(MXU_DIM // LANES):
        for off, size in groups:
            lo = tile * LANES + off
            bd[lo:lo + size, lo:lo + size] = 1.0 / size
    return jnp.asarray(bd, BF16)


def _rope_tables(pos):
    half = QK_ROPE // 2
    inv = ROPE_BASE ** (-jnp.arange(half, dtype=F32) / half)
    ang = pos.astype(F32)[:, None] * inv[None, :]
    return jnp.cos(ang), jnp.sin(ang)


def kernel(x_prompt, x_sample, p_prompt, p_sample, state_conv, state_ssm, cache_kv_latent, cache_k_rope, ln_ffn1, w1_gate, w1_up, w1_down, ln_mix, ln_ffn2, w2_gate, w2_up, w2_down, ln_ple, w_ple_in, w_ple_gate, a_in_proj, a_conv_w, a_conv_b, a_dt_bias, a_A_log, a_D, a_norm, a_out_proj, ln_kv, w_kv_a, kv_norm, kr_norm, w_kv_b, k_norm, b_w_dq, b_q_norm, b_w_q_b, b_qn_nope, b_qn_rope, b_w_o):
    depth = ln_ffn1.shape[0]
    n_a = a_in_proj.shape[0]
    assert depth == 2 and n_a == 1 and b_w_dq.shape[0] == 1
    d_model = x_prompt.shape[-1]
    heads = a_dt_bias.shape[-1]
    d_inner = heads * SSM_HEAD_DIM
    conv_dim = a_conv_w.shape[-1]
    kv_lora = kv_norm.shape[0]
    half = QK_ROPE // 2
    bf = lambda w: w.astype(BF16)
    row = lambda v: v.astype(F32)[None, :]

    ffn1 = [(row(ln_ffn1[i]), bf(w1_gate[i]), bf(w1_up[i]), bf(w1_down[i])) for i in range(depth)]
    ffn2 = [(row(ln_ffn2[i]), bf(w2_gate[i]), bf(w2_up[i]), bf(w2_down[i]),
             row(ln_ple[i]), bf(w_ple_in[i]), bf(w_ple_gate[i])) for i in range(depth)]
    w_in = a_in_proj[0]
    mamba_w = dict(
        g=row(ln_mix[0]), wz=bf(w_in[:, :d_inner]), wx=bf(w_in[:, d_inner:d_inner + conv_dim]),
        wdt=bf(w_in[:, d_inner + conv_dim:]), convw=a_conv_w[0], convb=row(a_conv_b[0]),
        dtb=row(a_dt_bias[0]), alog=row(a_A_log[0]), dexp=row(jnp.repeat(a_D[0], SSM_HEAD_DIM)),
        ng=row(a_norm[0]), wout=bf(a_out_proj[0]))

    swap = np.concatenate([np.arange(half, QK_ROPE), np.arange(half)])
    w_r = w_kv_a[:, kv_lora:]
    kv_w = dict(g=row(ln_kv), wc=bf(w_kv_a[:, :kv_lora]), wr=bf(w_r), wrs=bf(w_r[:, swap]),
                gc=row(kv_norm), gr=row(kr_norm), grs=row(kr_norm[swap]))
    per_kv = QK_NOPE + V_HEAD
    ex_w = dict(wk=bf(_head_pad_cols(w_kv_b, per_kv, 0, QK_NOPE)),
                wvt=bf(_head_rows(w_kv_b.reshape(kv_lora, MLA_HEADS, per_kv)[:, :, QK_NOPE:].transpose(1, 2, 0))),
                bd=_block_diag_mean([(0, QK_NOPE)]),
                gk=_head_pad_vec([(0, k_norm)]),
                place=jnp.asarray(np.eye(QK_ROPE, LANES, k=QK_NOPE), BF16),
                one=jnp.asarray(np.tile((np.arange(VT_ROWS) == V_HEAD).astype(np.float32)[:, None],
                                        (MLA_HEADS, LANES))))
    per_q = QK_NOPE + QK_ROPE
    wqb = b_w_q_b[0]
    wq = _head_pad_cols(wqb, per_q, 0, per_q)
    wqs = (_head_pad_cols(wqb, per_q, QK_NOPE + half, half, offset=QK_NOPE)
           + _head_pad_cols(wqb, per_q, QK_NOPE, half, offset=QK_NOPE + half))
    gq_rope = b_qn_rope[0]
    q_w = dict(g=row(ln_mix[1]), wdq=bf(b_w_dq[0]), gq=row(b_q_norm[0]), wq=bf(wq), wqs=bf(wqs),
               bd=_block_diag_mean([(0, QK_NOPE), (QK_NOPE, QK_ROPE)]),
               gh=_head_pad_vec([(0, b_qn_nope[0]), (QK_NOPE, gq_rope)]) * EXP2_SCALE,
               ghs=_head_pad_vec([(QK_NOPE, gq_rope[swap])]) * EXP2_SCALE)
    wo = bf(_head_rows(b_w_o[0].reshape(MLA_HEADS, V_HEAD, d_model)))

    wkv3 = w_kv_b.reshape(kv_lora, MLA_HEADS, per_kv)
    wq3 = wqb.reshape(wqb.shape[0], MLA_HEADS, per_q)
    flat = lambda w3: bf(w3.reshape(w3.shape[0], -1))
    per_head = lambda v: jnp.tile(v.astype(F32), MLA_HEADS)[None, :]
    flat_w = dict(
        wk=flat(wkv3[:, :, :QK_NOPE]), wv=flat(wkv3[:, :, QK_NOPE:]), gk=per_head(k_norm),
        wn=flat(wq3[:, :, :QK_NOPE]), wr=flat(wq3[:, :, QK_NOPE:]), wrs=flat(wq3[:, :, QK_NOPE:][:, :, swap]),
        bdn=_block_diag_mean([(o, QK_NOPE) for o in range(0, LANES, QK_NOPE)]),
        bdr=_block_diag_mean([(o, QK_ROPE) for o in range(0, LANES, QK_ROPE)]),
        gn=per_head(b_qn_nope[0]) * EXP2_SCALE, gr=per_head(gq_rope) * EXP2_SCALE,
        grs=per_head(gq_rope[swap]) * EXP2_SCALE, wo=bf(b_w_o[0]))

    def rope_tabs(pos):
        cos, sin = _rope_tables(pos)
        seq = pos.shape[0]
        cos32 = jnp.concatenate([cos, cos], axis=1)
        sin32 = jnp.concatenate([-sin, sin], axis=1)
        c128 = jnp.concatenate([jnp.ones((seq, QK_NOPE), F32), cos32,
                                jnp.zeros((seq, LANES - per_q), F32)], axis=1)
        s128 = jnp.concatenate([jnp.zeros((seq, QK_NOPE), F32), sin32,
                                jnp.zeros((seq, LANES - per_q), F32)], axis=1)
        return cos32, sin32, c128, s128

    def run_group(x, p, pos0, conv_in, ssm_in, past_ckv, past_kr):
        batch, seq, _ = x.shape
        m = batch * seq
        pos = pos0 + jnp.arange(seq, dtype=jnp.int32)
        cos32, sin32, c128, s128 = rope_tabs(pos)
        h = x.reshape(m, d_model)
        p2 = p.reshape(depth, m, p.shape[-1])

        h = _ffn(h, *ffn1[0])
        mw = mamba_w
        z, xbc, dt = _inproj(h, mw["g"], mw["wz"], mw["wx"], mw["wdt"])
        conv8 = jnp.pad(conv_in, ((0, 0), (SUBLANES - (CONV_W - 1), 0), (0, 0)))
        ssm4 = ssm_in.reshape(batch, SSM_GROUPS, d_inner // SSM_GROUPS, D_STATE)
        h, conv_out, ssm_out = _mamba(z, xbc, dt, h, conv8, ssm4, mw["convw"], mw["convb"], mw["dtb"],
                                      mw["alog"], mw["dexp"], mw["ng"], mw["wout"], batch, seq)
        conv_out = conv_out[None, :, SUBLANES - (CONV_W - 1):, :]
        ssm_out = ssm_out.reshape(1, batch, heads, SSM_HEAD_DIM, D_STATE)
        h = _ffn_ple(h, p2[0], *ffn2[0])

        kw = kv_w
        ckv_new, kr_new = _kvside(h, kw["g"], kw["wc"], kw["wr"], kw["wrs"], kw["gc"], kw["gr"], kw["grs"],
                                  cos32, sin32, seq)

        h = _ffn(h, *ffn1[1])
        qw = q_w
        if past_ckv is None:
            ew = ex_w
            k_all, vt_all = _expand(ckv_new, kr_new, ew["wk"], ew["wvt"], ew["bd"], ew["gk"], ew["place"],
                                    ew["one"], seq, min(ATTN_TILE, seq))
            q = _queries(h, qw["g"], qw["wdq"], qw["gq"], qw["wq"], qw["wqs"], qw["bd"], qw["gh"], qw["ghs"],
                         c128, s128, seq)
            width = MLA_HEADS * LANES
            h = _self_attention(q.reshape(batch, seq, width), k_all.reshape(batch, seq, width), vt_all,
                                h.reshape(batch, seq, d_model), wo, batch)
        else:
            fw = flat_w
            cos_f, sin_f = jnp.tile(cos32, (1, MLA_HEADS)), jnp.tile(sin32, (1, MLA_HEADS))
            qn, qr = _queries_flat(h, qw["g"], qw["wdq"], qw["gq"], fw["wn"], fw["wr"], fw["wrs"], fw["bdn"],
                                   fw["bdr"], fw["gn"], fw["gr"], fw["grs"], cos_f, sin_f, seq)
            h = _cache_attention(qn.reshape(batch, seq, -1), qr.reshape(batch, seq, -1), past_ckv, past_kr,
                                 ckv_new.reshape(batch, seq, kv_lora), kr_new.reshape(batch, seq, QK_ROPE),
                                 h.reshape(batch, seq, d_model), fw["wk"], fw["wv"], fw["bdn"], fw["gk"], fw["wo"])
        h = _ffn_ple(h.reshape(m, d_model), p2[1], *ffn2[1])
        return (h.reshape(batch, seq, d_model), conv_out, ssm_out,
                ckv_new.reshape(batch, seq, kv_lora), kr_new.reshape(batch, seq, QK_ROPE))

    b_p = x_prompt.shape[0]
    conv0 = jnp.zeros((b_p, CONV_W - 1, conv_dim), F32)
    ssm0 = jnp.zeros((b_p, heads, SSM_HEAD_DIM, D_STATE), F32)
    y_p, conv_p, ssm_p, kv_p, kr_p = run_group(x_prompt, p_prompt, 0, conv0, ssm0, None, None)
    y_s, conv_s, ssm_s, kv_s, kr_s = run_group(x_sample, p_sample, cache_kv_latent.shape[1],
                                               state_conv[0], state_ssm[0], cache_kv_latent, cache_k_rope)
    return (y_p, y_s, conv_p, ssm_p, kv_p, kr_p, conv_s, ssm_s, kv_s, kr_s)
```

```python
import functools

import numpy as np
import jax
import jax.numpy as jnp
from jax import lax
from jax.experimental import pallas as pl
from jax.experimental.pallas import tpu as pltpu

F32 = jnp.float32
BF16 = jnp.bfloat16

EPS = 1e-6
CHUNK = 64
SSM_HEAD_DIM = 64
SSM_GROUPS = 4
D_STATE = 128
CONV_W = 4
MLA_HEADS = 16
QK_NOPE = 64
QK_ROPE = 32
V_HEAD = 64
ROPE_BASE = 10000.0
ATTN_SCALE = (QK_NOPE + QK_ROPE) ** -0.5
EXP2_SCALE = ATTN_SCALE * float(np.log2(np.e))

LANES = 128
VT_ROWS = V_HEAD + 16
SUBLANES = 8
MXU_DIM = 256
VMEM_LIMIT_BYTES = 56 * 1024 * 1024
ROW_TILE = 512
ATTN_TILE = 256
MAMBA_TILE = 256
CONV_ROWS = 32


def _cparams(n_grid):
    return pltpu.CompilerParams(dimension_semantics=("arbitrary",) * n_grid,
                                vmem_limit_bytes=VMEM_LIMIT_BYTES)


def _const_spec(shape):
    nd = len(shape)
    return pl.BlockSpec(shape, lambda *_: (0,) * nd, pipeline_mode=pl.Buffered(1))


def _row_spec(tm, width):
    return pl.BlockSpec((tm, width), lambda i: (i, 0))


def _row_tile(m):
    tm = min(ROW_TILE, m)
    assert m % tm == 0 and tm % SUBLANES == 0
    return tm


def _dot(a, b):
    return jnp.dot(a, b, preferred_element_type=F32)


def _dot_nt(a, b):
    return lax.dot_general(a, b, (((1,), (1,)), ((), ())), preferred_element_type=F32)


def _dot_tn(a, b):
    return lax.dot_general(a, b, (((0,), (0,)), ((), ())), preferred_element_type=F32)


def _rms(x, g):
    ms = jnp.mean(x * x, axis=-1, keepdims=True)
    return x * lax.rsqrt(ms + EPS) * g


def _silu(x):
    return x * jax.nn.sigmoid(x)


def _softplus(x):
    return jnp.maximum(x, 0.0) + jnp.log1p(jnp.exp(-jnp.abs(x)))


def _split_bf16(v, n):
    pieces = []
    r = v
    for _ in range(n):
        p = r.astype(BF16)
        pieces.append(p)
        r = r - p.astype(F32)
    return pieces


def _ffn_body(h, g, wg_ref, wu_ref, wd_ref):
    u = _rms(h, g).astype(BF16)
    a = _dot(u, wg_ref[...])
    b = _dot(u, wu_ref[...])
    act = (_silu(a) * b).astype(BF16)
    return h + 0.5 * _dot(act, wd_ref[...])


def _ffn_kernel(h_ref, g_ref, wg_ref, wu_ref, wd_ref, o_ref):
    o_ref[...] = _ffn_body(h_ref[...], g_ref[...], wg_ref, wu_ref, wd_ref)


def _ffn_ple_kernel(h_ref, p_ref, g_ref, wg_ref, wu_ref, wd_ref, gp_ref, wpi_ref, wpg_ref, o_ref):
    h2 = _ffn_body(h_ref[...], g_ref[...], wg_ref, wu_ref, wd_ref)
    gate = jax.nn.sigmoid(_dot(_rms(h2, gp_ref[...]).astype(BF16), wpg_ref[...]))
    o_ref[...] = h2 + _dot(p_ref[...].astype(BF16), wpi_ref[...]) * gate


def _layer_spec(w, layer):
    return pl.BlockSpec((None,) + w.shape[1:], lambda *_: (layer, 0, 0), pipeline_mode=pl.Buffered(1))


def _ffn(h, layer, g, wg, wu, wd):
    m, d = h.shape
    tm = _row_tile(m)
    return pl.pallas_call(
        _ffn_kernel,
        grid=(m // tm,),
        in_specs=[_row_spec(tm, d)] + [_layer_spec(w, layer) for w in (g, wg, wu, wd)],
        out_specs=_row_spec(tm, d),
        out_shape=jax.ShapeDtypeStruct((m, d), F32),
        compiler_params=_cparams(1),
    )(h, g, wg, wu, wd)


def _ffn_ple(h, p, layer, g, wg, wu, wd, gp, wpi, wpg):
    m, d = h.shape
    tm = _row_tile(m)
    params = (g, wg, wu, wd, gp, wpi, wpg)
    return pl.pallas_call(
        _ffn_ple_kernel,
        grid=(m // tm,),
        in_specs=[_row_spec(tm, d), pl.BlockSpec((None, tm, p.shape[2]), lambda i: (layer, i, 0))]
                 + [_layer_spec(w, layer) for w in params],
        out_specs=_row_spec(tm, d),
        out_shape=jax.ShapeDtypeStruct((m, d), F32),
        compiler_params=_cparams(1),
    )(h, p, *params)


def _inproj_kernel(h_ref, g_ref, w_ref, z_ref, xbc_ref, dt_ref):
    u = _rms(h_ref[...], g_ref[...]).astype(BF16)
    lo = 0
    for out in (z_ref, xbc_ref, dt_ref):
        hi = lo + out.shape[1]
        out[...] = _dot(u, w_ref[:, lo:hi])
        lo = hi


def _inproj(h, g, w, widths):
    m, d = h.shape
    tm = _row_tile(m)
    assert sum(widths) == w.shape[1] and all(x % LANES == 0 for x in widths[:-1])
    return pl.pallas_call(
        _inproj_kernel,
        grid=(m // tm,),
        in_specs=[_row_spec(tm, d), _const_spec(g.shape), _const_spec(w.shape)],
        out_specs=[_row_spec(tm, x) for x in widths],
        out_shape=[jax.ShapeDtypeStruct((m, x), F32) for x in widths],
        compiler_params=_cparams(1),
    )(h, g, w)


def _mamba_kernel(chunk, z_ref, xbc_ref, dt_ref, h_ref, cst_ref, sst_ref,
                  convw_ref, convb_ref, dtb_ref, alog_ref, dexp_ref, ng_ref, wout_ref,
                  blt_ref, e2_ref, i2_ref,
                  o_ref, cout_ref, sout_ref,
                  xpad, st, x_s, bc_s, ae_s, de_s, y_s, at_s, yn_s):
    t = pl.program_id(1)
    n_t = pl.num_programs(1)
    tb = xbc_ref.shape[0]
    d_inner = x_s.shape[1]
    gn = SSM_GROUPS * D_STATE
    n_chunks = tb // chunk
    gw = d_inner // SSM_GROUPS
    half = LANES // 2
    pad_rows = half - chunk

    @pl.when(t == 0)
    def _():
        xpad[0:SUBLANES, :] = cst_ref[0]
        for g in range(SSM_GROUPS):
            st[g] = sst_ref[0, g].T

    xpad[SUBLANES:SUBLANES + tb, :] = xbc_ref[...]
    w = convw_ref[...]
    base = SUBLANES - (CONV_W - 1)
    rb = min(CONV_ROWS, chunk)
    row8 = lax.broadcasted_iota(jnp.int32, (SUBLANES, xpad.shape[1]), 0)

    def conv_rows(r0):
        ext = xpad[r0:r0 + rb + SUBLANES, :]
        slabs = [ext[s:s + SUBLANES, :] for s in range(0, rb + SUBLANES, SUBLANES)]
        ys = [convb_ref[...] + sl * w[CONV_W - 1:CONV_W, :] for sl in slabs[1:]]
        for k in range(CONV_W - 1):
            back = CONV_W - 1 - k
            rolled = [pltpu.roll(sl, back, axis=0) for sl in slabs]
            for n in range(len(ys)):
                ys[n] = ys[n] + jnp.where(row8 < back, rolled[n], rolled[n + 1]) * w[k:k + 1, :]
        y = jnp.concatenate(ys, axis=0)
        act = _silu(y)
        x_s[r0:r0 + rb, :] = act[:, :d_inner]
        bc_s[r0:r0 + rb, :] = act[:, d_inner:]

    dt =_softplus(dt_ref[...] + dtb_ref[...])
    a = dt * (-jnp.exp(alog_ref[...]))
    blt = blt_ref[...]
    acs = sum(_dot(blt, p) for p in _split_bf16(a, 3))
    e2 = e2_ref[...]
    ae_s[...] = _dot(jnp.concatenate(_split_bf16(acs, 2), axis=1), e2)
    de_s[...] = _dot(jnp.concatenate(_split_bf16(dt, 2), axis=1), e2)
    i2 = i2_ref[...]
    at_all = sum(_dot_tn(p, i2) for p in _split_bf16(acs, 3))
    for c in range(n_chunks):
        at_s[c] = at_all[:, c * LANES:(c + 1) * LANES]

    lane = lax.broadcasted_iota(jnp.int32, (chunk, LANES), 1)
    row = lax.broadcasted_iota(jnp.int32, (chunk, LANES), 0)
    low_half = lane < half
    src = lane & (half - 1)
    causal = (src <= row) & (src < chunk)
    dexp = dexp_ref[...]

    def stack_pair(top, bot):
        if pad_rows == 0:
            return jnp.concatenate([top, bot], axis=0)
        zeros = jnp.zeros((pad_rows, LANES), top.dtype)
        return jnp.concatenate([top, zeros, bot, zeros], axis=0)

    def chunk_body(c, carry):
        rows = pl.ds(c * chunk, chunk)
        ae = ae_s[rows, :]
        xc = x_s[rows, :]
        last = ae[chunk - 1:chunk, :]
        xdt = xc * de_s[rows, :]
        xw = (xdt * jnp.exp(last - ae)).astype(BF16)
        xdt_b = xdt.astype(BF16)
        e_acs = jnp.exp(ae)
        decay = jnp.exp(last)
        bb = bc_s[rows, 0:gn].astype(BF16)
        cb = bc_s[rows, gn:2 * gn].astype(BF16)
        at2 = at_s[c]
        for g in range(SSM_GROUPS):
            gl = slice(g * gw, (g + 1) * gw)
            c_g = cb[:, g * D_STATE:(g + 1) * D_STATE]
            b_g = bb[:, g * D_STATE:(g + 1) * D_STATE]
            cb2 = _dot_nt(c_g, stack_pair(b_g, b_g))
            s_g = st[g]
            y_off = _dot(c_g, s_g.astype(BF16)) * e_acs[:, gl]
            pairs = []
            for q in range(gw // LANES):
                pq = g * (gw // LANES) + q
                pl_ = slice(pq * LANES, (pq + 1) * LANES)
                rowvec = jnp.where(low_half[0:1, :], at2[2 * pq:2 * pq + 1, :], at2[2 * pq + 1:2 * pq + 2, :])
                seg = jnp.exp(jnp.where(causal, ae[:, pl_] - rowvec, -jnp.inf))
                wmat = (cb2 * seg).astype(BF16)
                xp = xdt_b[:, pl_]
                zero = jnp.zeros_like(xp)
                x2 = stack_pair(jnp.where(low_half, xp, zero), jnp.where(low_half, zero, xp))
                pairs.append(_dot(wmat, x2))
            y_diag = jnp.concatenate(pairs, axis=1)
            y_s[rows, gl] = y_diag + y_off + dexp[:, gl] * xc[:, gl]
            st[g] = s_g * decay[:, gl] + _dot_tn(b_g, xw[:, gl])
        return carry

    for c in range(n_chunks):
        for r0 in range(c * chunk, (c + 1) * chunk, rb):
            conv_rows(r0)
        chunk_body(c, 0)
        rows = pl.ds(c * chunk, chunk)
        yz = y_s[rows, :] * _silu(z_ref[rows, :])
        yn_s[rows, :] = _rms(yz, ng_ref[...]).astype(BF16)
    tail = xpad[tb:tb + SUBLANES, :]
    cout_ref[0] = tail
    xpad[0:SUBLANES, :] = tail
    o_ref[...] = h_ref[...] + _dot(yn_s[...], wout_ref[...])

    @pl.when(t == n_t - 1)
    def _():
        for g in range(SSM_GROUPS):
            sout_ref[0, g] = st[g].T


def _mamba(z, xbc, dt, h, conv_state8, ssm_state, convw, convb, dtb, alog, dexp, ng, wout, batch, seq):
    m, d_inner = z.shape
    conv_dim = xbc.shape[1]
    heads = dt.shape[1]
    d = h.shape[1]
    tb = min(MAMBA_TILE, seq)
    chunk = min(CHUNK, seq)
    assert seq % tb == 0 and tb % chunk == 0 and chunk % SUBLANES == 0 and LANES // 2 % chunk == 0
    n_t = seq // tb
    n_chunks = tb // chunk
    gw = d_inner // SSM_GROUPS

    r = np.arange(tb)
    blt = ((r[:, None] // chunk == r[None, :] // chunk) & (r[None, :] <= r[:, None])).astype(np.float32)
    e = (np.arange(d_inner)[None, :] // SSM_HEAD_DIM == np.arange(heads)[:, None]).astype(np.float32)
    e = np.concatenate([e, e], axis=0)
    col = np.arange(n_chunks * LANES)
    i2 = ((col[None, :] // LANES == r[:, None] // chunk)
          & (col[None, :] % (LANES // 2) == r[:, None] % chunk)).astype(np.float32)
    blt, e, i2 = (jnp.asarray(v, BF16) for v in (blt, e, i2))

    row_map = lambda b, t: (b * n_t + t, 0)
    const = lambda v: _const_spec(v.shape)
    return pl.pallas_call(
        functools.partial(_mamba_kernel, chunk),
        grid=(batch, n_t),
        in_specs=[pl.BlockSpec((tb, d_inner), row_map), pl.BlockSpec((tb, conv_dim), row_map),
                  pl.BlockSpec((tb, heads), row_map), pl.BlockSpec((tb, d), row_map),
                  pl.BlockSpec((1, SUBLANES, conv_dim), lambda b, t: (b, 0, 0)),
                  pl.BlockSpec((1, SSM_GROUPS, gw, D_STATE), lambda b, t: (b, 0, 0, 0)),
                  const(convw), const(convb), const(dtb), const(alog), const(dexp), const(ng), const(wout),
                  const(blt), const(e), const(i2)],
        out_specs=[pl.BlockSpec((tb, d), row_map),
                   pl.BlockSpec((1, SUBLANES, conv_dim), lambda b, t: (b, 0, 0)),
                   pl.BlockSpec((1, SSM_GROUPS, gw, D_STATE), lambda b, t: (b, 0, 0, 0))],
        out_shape=[jax.ShapeDtypeStruct((m, d), F32),
                   jax.ShapeDtypeStruct((batch, SUBLANES, conv_dim), F32),
                   jax.ShapeDtypeStruct((batch, SSM_GROUPS, gw, D_STATE), F32)],
        scratch_shapes=[pltpu.VMEM((tb + SUBLANES, conv_dim), F32),
                        pltpu.VMEM((SSM_GROUPS, D_STATE, gw), F32),
                        pltpu.VMEM((tb, d_inner), F32),
                        pltpu.VMEM((tb, conv_dim - d_inner), F32),
                        pltpu.VMEM((tb, d_inner), F32),
                        pltpu.VMEM((tb, d_inner), F32),
                        pltpu.VMEM((tb, d_inner), F32),
                        pltpu.VMEM((n_chunks, heads, LANES), F32),
                        pltpu.VMEM((tb, d_inner), BF16)],
        compiler_params=_cparams(2),
    )(z, xbc, dt, h, conv_state8, ssm_state, convw, convb, dtb, alog, dexp, ng, wout, blt, e, i2)


def _table_spec(tm, seq, width):
    if seq <= tm:
        return pl.BlockSpec((tm, width), lambda i: (0, 0))
    per = seq // tm
    return pl.BlockSpec((tm, width), lambda i: (i % per, 0))


def _tile_table(tab, tm, seq):
    return jnp.tile(tab, (tm // seq, 1)) if seq < tm else tab


def _kvside_kernel(h_ref, g_ref, wc_ref, wr_ref, wrs_ref, gc_ref, gr_ref, grs_ref, cos_ref, sin_ref,
                   ckv_ref, kr_ref):
    u = _rms(h_ref[...], g_ref[...]).astype(BF16)
    ckv_ref[...] = _rms(_dot(u, wc_ref[...]), gc_ref[...])
    r = _dot(u, wr_ref[...])
    rs = _dot(u, wrs_ref[...])
    inv = lax.rsqrt(jnp.mean(r * r, axis=-1, keepdims=True) + EPS)
    kr_ref[...] = r * inv * gr_ref[...] * cos_ref[...] + rs * inv * grs_ref[...] * sin_ref[...]


def _kvside(h, g, wc, wr, wrs, gc, gr, grs, cos32, sin32, seq):
    m, d = h.shape
    tm = _row_tile(m)
    assert tm % seq == 0 or seq % tm == 0
    cos32, sin32 = _tile_table(cos32, tm, seq), _tile_table(sin32, tm, seq)
    consts = (g, wc, wr, wrs, gc, gr, grs)
    return pl.pallas_call(
        _kvside_kernel,
        grid=(m // tm,),
        in_specs=[_row_spec(tm, d)] + [_const_spec(v.shape) for v in consts]
                 + [_table_spec(tm, seq, QK_ROPE)] * 2,
        out_specs=[_row_spec(tm, wc.shape[1]), _row_spec(tm, QK_ROPE)],
        out_shape=[jax.ShapeDtypeStruct((m, wc.shape[1]), F32), jax.ShapeDtypeStruct((m, QK_ROPE), F32)],
        compiler_params=_cparams(1),
    )(h, *consts, cos32, sin32)


def _group_mean_sq(x, bd_ref):
    bd = bd_ref[...]
    sq = (x * x).astype(BF16)
    return jnp.concatenate([_dot(sq[:, s:s + MXU_DIM], bd) for s in range(0, x.shape[1], MXU_DIM)], axis=1)


def _expand_kernel(ckv_ref, kr_ref, wk_ref, wvt_ref, bd_ref, gk_ref, place_ref, one_ref, k_ref, vt_ref):
    c = ckv_ref[...].astype(BF16)
    k = _dot(c, wk_ref[...])
    kn = k * lax.rsqrt(_group_mean_sq(k, bd_ref) + EPS) * gk_ref[...]
    kr = _dot(kr_ref[...].astype(BF16), place_ref[...])
    for hh in range(MLA_HEADS):
        hl = slice(hh * LANES, (hh + 1) * LANES)
        k_ref[:, hl] = (kn[:, hl] + kr).astype(BF16)
    vt = _dot_nt(wvt_ref[...], c)
    one = one_ref[...]
    tk = vt_ref.shape[3]
    for j in range(vt_ref.shape[1]):
        for t in range(0, tk, LANES):
            vt_ref[0, j, :, t:t + LANES] = (vt[:, j * tk + t:j * tk + t + LANES] + one).astype(BF16)


def _expand(ckv, kr, wk, wvt, bd, gk, place, one, seq, tk):
    m = ckv.shape[0]
    tm = min(_row_tile(m), seq)
    assert seq % tm == 0 and tm % tk == 0
    per = seq // tm
    consts = (wk, wvt, bd, gk, place, one)
    width = MLA_HEADS * LANES
    return pl.pallas_call(
        _expand_kernel,
        grid=(m // tm,),
        in_specs=[_row_spec(tm, ckv.shape[1]), _row_spec(tm, kr.shape[1])]
                 + [_const_spec(v.shape) for v in consts],
        out_specs=[_row_spec(tm, width),
                   pl.BlockSpec((1, tm // tk, wvt.shape[0], tk), lambda i: (i // per, i % per, 0, 0))],
        out_shape=[jax.ShapeDtypeStruct((m, width), BF16),
                   jax.ShapeDtypeStruct((m // seq, seq // tk, wvt.shape[0], tk), BF16)],
        compiler_params=_cparams(1),
    )(ckv, kr, *consts)


def _q_kernel(h_ref, g_ref, wdq_ref, gq_ref, wq_ref, wqs_ref, bd_ref, gh_ref, ghs_ref, c_ref, s_ref, q_ref):
    u = _rms(h_ref[...], g_ref[...]).astype(BF16)
    cq = _rms(_dot(u, wdq_ref[...]), gq_ref[...]).astype(BF16)
    q = _dot(cq, wq_ref[...])
    qs = _dot(cq, wqs_ref[...])
    inv = lax.rsqrt(_group_mean_sq(q, bd_ref) + EPS)
    qn = q * inv * gh_ref[...]
    qsn = qs * inv * ghs_ref[...]
    c = c_ref[...]
    s = s_ref[...]
    for hh in range(MLA_HEADS):
        hl = slice(hh * LANES, (hh + 1) * LANES)
        q_ref[:, hl] = (qn[:, hl] * c + qsn[:, hl] * s).astype(BF16)


def _queries(h, g, wdq, gq, wq, wqs, bd, gh, ghs, c128, s128, seq):
    m, d = h.shape
    tm = _row_tile(m)
    assert tm % seq == 0 or seq % tm == 0
    c128, s128 = _tile_table(c128, tm, seq), _tile_table(s128, tm, seq)
    consts = (g, wdq, gq, wq, wqs, bd, gh, ghs)
    width = MLA_HEADS * LANES
    return pl.pallas_call(
        _q_kernel,
        grid=(m // tm,),
        in_specs=[_row_spec(tm, d)] + [_const_spec(v.shape) for v in consts]
                 + [_table_spec(tm, seq, LANES)] * 2,
        out_specs=_row_spec(tm, width),
        out_shape=jax.ShapeDtypeStruct((m, width), BF16),
        compiler_params=_cparams(1),
    )(h, *consts, c128, s128)


def _q_flat_kernel(h_ref, g_ref, wdq_ref, gq_ref, wn_ref, wr_ref, wrs_ref, bdn_ref, bdr_ref,
                   gn_ref, gr_ref, grs_ref, cos_ref, sin_ref, qn_ref, qr_ref):
    u = _rms(h_ref[...], g_ref[...]).astype(BF16)
    cq = _rms(_dot(u, wdq_ref[...]), gq_ref[...]).astype(BF16)
    qn = _dot(cq, wn_ref[...])
    qn_ref[...] = (qn * lax.rsqrt(_group_mean_sq(qn, bdn_ref) + EPS) * gn_ref[...]).astype(BF16)
    r = _dot(cq, wr_ref[...])
    rs = _dot(cq, wrs_ref[...])
    inv = lax.rsqrt(_group_mean_sq(r, bdr_ref) + EPS)
    qr_ref[...] = (r * inv * gr_ref[...] * cos_ref[...] + rs * inv * grs_ref[...] * sin_ref[...]).astype(BF16)


def _queries_flat(h, g, wdq, gq, wn, wr, wrs, bdn, bdr, gn, gr, grs, cos, sin, seq):
    m, d = h.shape
    tm = _row_tile(m)
    assert tm % seq == 0 or seq % tm == 0
    cos, sin = _tile_table(cos, tm, seq), _tile_table(sin, tm, seq)
    consts = (g, wdq, gq, wn, wr, wrs, bdn, bdr, gn, gr, grs)
    widths = (wn.shape[1], wr.shape[1])
    return pl.pallas_call(
        _q_flat_kernel,
        grid=(m // tm,),
        in_specs=[_row_spec(tm, d)] + [_const_spec(v.shape) for v in consts]
                 + [_table_spec(tm, seq, widths[1])] * 2,
        out_specs=[_row_spec(tm, w) for w in widths],
        out_shape=[jax.ShapeDtypeStruct((m, w), BF16) for w in widths],
        compiler_params=_cparams(1),
    )(h, *consts, cos, sin)


def _cache_attn_kernel(kb, qn_ref, qr_ref, ckv_ref, kr_ref, ckvn_ref, krn_ref, h_ref,
                       wk_ref, wv_ref, bd_ref, gk_ref, wo_ref, o_ref, s_s, v_s, acc_s):
    tq = qn_ref.shape[1]
    past = ckv_ref.shape[1]
    cols = MLA_HEADS * tq
    chunk_shift = CHUNK.bit_length() - 1
    assert CHUNK == 1 << chunk_shift and past % kb == 0

    qn = jnp.concatenate([qn_ref[0]] * MLA_HEADS, axis=0)
    tq_shift, nope_shift = tq.bit_length() - 1, QK_NOPE.bit_length() - 1
    assert tq == 1 << tq_shift and QK_NOPE == 1 << nope_shift
    r_head = lax.broadcasted_iota(jnp.int32, qn.shape, 0) >> tq_shift
    c_head = lax.broadcasted_iota(jnp.int32, qn.shape, 1) >> nope_shift
    q_bd = jnp.where(r_head == c_head, qn, jnp.zeros_like(qn))
    qr_all = qr_ref[0]
    q_rope = jnp.concatenate([qr_all[:, hh * QK_ROPE:(hh + 1) * QK_ROPE] for hh in range(MLA_HEADS)], axis=0)

    def expand(c, kr, rows):
        c = c.astype(BF16)
        k = _dot(c, wk_ref[...])
        kn = (k * lax.rsqrt(_group_mean_sq(k, bd_ref) + EPS) * gk_ref[...]).astype(BF16)
        s = _dot_nt(kn, q_bd) + _dot_nt(kr.astype(BF16), q_rope)
        v = _dot(c, wv_ref[...]).astype(BF16)
        v_s[rows, :] = jnp.concatenate([v, jnp.ones((v.shape[0], LANES), BF16)], axis=1)
        return s

    def expand_block(j, m_run):
        rows = pl.ds(pl.multiple_of(j * kb, kb), kb)
        s = expand(ckv_ref[0, rows, :], kr_ref[0, rows, :], rows)
        s_s[rows, :] = s
        return jnp.maximum(m_run, jnp.max(s, axis=0, keepdims=True))

    m_run = lax.fori_loop(0, past // kb, expand_block, jnp.full((1, cols), -jnp.inf, F32))
    new_rows = pl.ds(past, tq)
    s_new = expand(ckvn_ref[0], krn_ref[0], new_rows)
    k_chunk = (past + lax.broadcasted_iota(jnp.int32, (tq, cols), 0)) >> chunk_shift
    q_chunk = (past + (lax.broadcasted_iota(jnp.int32, (tq, cols), 1) & (tq - 1))) >> chunk_shift
    s_new = jnp.where(k_chunk <= q_chunk, s_new, -jnp.inf)
    m_all = jnp.maximum(m_run, jnp.max(s_new, axis=0, keepdims=True))

    acc_s[...] = _dot_tn(jnp.exp2(s_new - m_all).astype(BF16), v_s[new_rows, :])

    def pv_block(j, carry):
        rows = pl.ds(pl.multiple_of(j * kb, kb), kb)
        p = jnp.exp2(s_s[rows, :] - m_all).astype(BF16)
        acc_s[...] += _dot_tn(p, v_s[rows, :])
        return carry

    lax.fori_loop(0, past // kb, pv_block, 0)

    width = MLA_HEADS * V_HEAD
    low_half = lax.broadcasted_iota(jnp.int32, (tq, LANES), 1) < V_HEAD
    tiles = []
    for c in range(width // LANES):
        ev = acc_s[2 * c * tq:(2 * c + 1) * tq, :]
        od = acc_s[(2 * c + 1) * tq:(2 * c + 2) * tq, :]
        cl = slice(c * LANES, (c + 1) * LANES)
        tiles.append(jnp.where(low_half, ev[:, cl] / ev[:, width:], od[:, cl] / od[:, width:]))
    o = jnp.concatenate(tiles, axis=1).astype(BF16)
    o_ref[0] = h_ref[0] + _dot(o, wo_ref[...])


def _cache_attention(qn, qr, ckv, kr, ckv_new, kr_new, h, wk, wv, bd, gk, wo):
    batch, tq, _ = qn.shape
    past = ckv.shape[1]
    d = h.shape[2]
    kb = min(ROW_TILE, past)
    assert past % kb == 0 and tq % 16 == 0 and (MLA_HEADS * tq) % LANES == 0
    per = lambda a: pl.BlockSpec((1,) + a.shape[1:], lambda b: (b, 0, 0))
    consts = (wk, wv, bd, gk, wo)
    return pl.pallas_call(
        functools.partial(_cache_attn_kernel, kb),
        grid=(batch,),
        in_specs=[per(qn), per(qr), per(ckv), per(kr), per(ckv_new), per(kr_new), per(h)]
                 + [_const_spec(v.shape) for v in consts],
        out_specs=per(h),
        out_shape=jax.ShapeDtypeStruct(h.shape, F32),
        scratch_shapes=[pltpu.VMEM((past + tq, MLA_HEADS * tq), F32),
                        pltpu.VMEM((past + tq, MLA_HEADS * V_HEAD + LANES), BF16),
                        pltpu.VMEM((MLA_HEADS * tq, MLA_HEADS * V_HEAD + LANES), F32)],
        compiler_params=_cparams(1),
    )(qn, qr, ckv, kr, ckv_new, kr_new, h, *consts)


def _self_attn_kernel(q_ref, k_ref, vt_ref, h_ref, wo_ref, o_ref, m_s, acc_s, ot_s, st_s):
    i = pl.program_id(1)
    tq = q_ref.shape[1]
    chunk_shift = CHUNK.bit_length() - 1
    assert CHUNK == 1 << chunk_shift
    visible = ((lax.broadcasted_iota(jnp.int32, (tq, tq), 0) >> chunk_shift)
               <= (lax.broadcasted_iota(jnp.int32, (tq, tq), 1) >> chunk_shift))

    m_s[...] = jnp.full(m_s.shape, -jnp.inf, F32)
    acc_s[...] = jnp.zeros(acc_s.shape, F32)

    def step(j, masked):
        k0 = pl.multiple_of(j * tq, tq)
        for hh in range(MLA_HEADS):
            hl = slice(hh * LANES, (hh + 1) * LANES)
            st_s[hh] = _dot_nt(k_ref[0, pl.ds(k0, tq), hl], q_ref[0, :, hl])
        for hh in range(MLA_HEADS):
            st = st_s[hh]
            if masked:
                st = jnp.where(visible, st, -jnp.inf)
            m_old = m_s[hh]
            m_new = jnp.maximum(m_old, jnp.max(st, axis=0, keepdims=True))
            alpha = jnp.exp2(m_old - m_new)
            pt = jnp.exp2(st - m_new).astype(BF16)
            vt = vt_ref[0, j, hh * VT_ROWS:(hh + 1) * VT_ROWS, :]
            acc_s[hh] = alpha * acc_s[hh] + _dot(vt, pt)
            m_s[hh] = m_new

    def full_step(j, carry):
        step(j, False)
        return carry

    lax.fori_loop(0, i, full_step, 0)
    step(i, True)

    for hh in range(MLA_HEADS):
        acc = acc_s[hh]
        ot_s[hh * VT_ROWS:(hh + 1) * VT_ROWS, :] = (acc / acc[V_HEAD:V_HEAD + 1, :]).astype(BF16)
    o_ref[0] = h_ref[0] + _dot_tn(ot_s[...], wo_ref[...])


def _self_attention(q, k, vt, h, wo, batch):
    width = q.shape[2]
    seq = q.shape[1]
    d = h.shape[2]
    tq = vt.shape[3]
    vrows = vt.shape[2]
    assert seq % tq == 0 and tq % CHUNK == 0 and vt.shape[1] * tq == seq and vrows == MLA_HEADS * VT_ROWS
    return pl.pallas_call(
        _self_attn_kernel,
        grid=(batch, seq // tq),
        in_specs=[pl.BlockSpec((1, tq, width), lambda b, i: (b, i, 0)),
                  pl.BlockSpec((1, seq, width), lambda b, i: (b, 0, 0)),
                  pl.BlockSpec((1, seq // tq, vrows, tq), lambda b, i: (b, 0, 0, 0)),
                  pl.BlockSpec((1, tq, d), lambda b, i: (b, i, 0)),
                  _const_spec(wo.shape)],
        out_specs=pl.BlockSpec((1, tq, d), lambda b, i: (b, i, 0)),
        out_shape=jax.ShapeDtypeStruct((batch, seq, d), F32),
        scratch_shapes=[pltpu.VMEM((MLA_HEADS, 1, tq), F32), pltpu.VMEM((MLA_HEADS, VT_ROWS, tq), F32),
                        pltpu.VMEM((vrows, tq), BF16), pltpu.VMEM((MLA_HEADS, tq, tq), F32)],
        compiler_params=_cparams(2),
    )(q, k, vt, h, wo)


def _head_pad_cols(w, per_head, start, count, offset=0):
    k = w.shape[0]
    w3 = w.reshape(k, MLA_HEADS, per_head)[:, :, start:start + count]
    out = jnp.zeros((k, MLA_HEADS, LANES), w.dtype)
    out = out.at[:, :, offset:offset + count].set(w3)
    return out.reshape(k, MLA_HEADS * LANES)


def _head_rows(w3):
    h, r, n = w3.shape
    return jnp.zeros((h, VT_ROWS, n), w3.dtype).at[:, :r].set(w3).reshape(h * VT_ROWS, n)


def _head_pad_vec(pieces):
    blk = jnp.zeros((LANES,), F32)
    for off, vec in pieces:
        blk = blk.at[off:off + vec.shape[0]].set(vec.astype(F32))
    return jnp.tile(blk, MLA_HEADS)[None, :]


def _block_diag_mean(groups):
    bd = np.zeros((MXU_DIM, MXU_DIM), np.float32)
    for tile in range(MXU_DIM // LANES):
        for off, size in groups:
            lo = tile * LANES + off
            bd[lo:lo + size, lo:lo + size] = 1.0 / size
    return jnp.asarray(bd, BF16)


def _rope_tables(pos):
    half = QK_ROPE // 2
    inv = ROPE_BASE ** (-jnp.arange(half, dtype=F32) / half)
    ang = pos.astype(F32)[:, None] * inv[None, :]
    return jnp.cos(ang), jnp.sin(ang)


def kernel(x_prompt, x_sample, p_prompt, p_sample, state_conv, state_ssm, cache_kv_latent, cache_k_rope, ln_ffn1, w1_gate, w1_up, w1_down, ln_mix, ln_ffn2, w2_gate, w2_up, w2_down, ln_ple, w_ple_in, w_ple_gate, a_in_proj, a_conv_w, a_conv_b, a_dt_bias, a_A_log, a_D, a_norm, a_out_proj, ln_kv, w_kv_a, kv_norm, kr_norm, w_kv_b, k_norm, b_w_dq, b_q_norm, b_w_q_b, b_qn_nope, b_qn_rope, b_w_o):
    depth = ln_ffn1.shape[0]
    n_a = a_in_proj.shape[0]
    assert depth == 2 and n_a == 1 and b_w_dq.shape[0] == 1
    d_model = x_prompt.shape[-1]
    heads = a_dt_bias.shape[-1]
    d_inner = heads * SSM_HEAD_DIM
    conv_dim = a_conv_w.shape[-1]
    kv_lora = kv_norm.shape[0]
    half = QK_ROPE // 2
    bf = lambda w: w.astype(BF16)
    row = lambda v: v.astype(F32)[None, :]

    rows = lambda v: v.astype(F32)[:, None, :]
    ffn1 = (rows(ln_ffn1), bf(w1_gate), bf(w1_up), bf(w1_down))
    ffn2 = (rows(ln_ffn2), bf(w2_gate), bf(w2_up), bf(w2_down), rows(ln_ple), bf(w_ple_in), bf(w_ple_gate))
    mamba_w = dict(
        g=row(ln_mix[0]), w_in=bf(a_in_proj[0]), convw=a_conv_w[0], convb=row(a_conv_b[0]),
        dtb=row(a_dt_bias[0]), alog=row(a_A_log[0]), dexp=row(jnp.repeat(a_D[0], SSM_HEAD_DIM)),
        ng=row(a_norm[0]), wout=bf(a_out_proj[0]))

    swap = np.concatenate([np.arange(half, QK_ROPE), np.arange(half)])
    w_r = w_kv_a[:, kv_lora:]
    kv_w = dict(g=row(ln_kv), wc=bf(w_kv_a[:, :kv_lora]), wr=bf(w_r), wrs=bf(w_r[:, swap]),
                gc=row(kv_norm), gr=row(kr_norm), grs=row(kr_norm[swap]))
    per_kv = QK_NOPE + V_HEAD
    ex_w = dict(wk=bf(_head_pad_cols(w_kv_b, per_kv, 0, QK_NOPE)),
                wvt=bf(_head_rows(w_kv_b.reshape(kv_lora, MLA_HEADS, per_kv)[:, :, QK_NOPE:].transpose(1, 2, 0))),
                bd=_block_diag_mean([(0, QK_NOPE)]),
                gk=_head_pad_vec([(0, k_norm)]),
                place=jnp.asarray(np.eye(QK_ROPE, LANES, k=QK_NOPE), BF16),
                one=jnp.asarray(np.tile((np.arange(VT_ROWS) == V_HEAD).astype(np.float32)[:, None],
                                        (MLA_HEADS, LANES))))
    per_q = QK_NOPE + QK_ROPE
    wqb = b_w_q_b[0]
    wq = _head_pad_cols(wqb, per_q, 0, per_q)
    wqs = (_head_pad_cols(wqb, per_q, QK_NOPE + half, half, offset=QK_NOPE)
           + _head_pad_cols(wqb, per_q, QK_NOPE, half, offset=QK_NOPE + half))
    gq_rope = b_qn_rope[0]
    q_w = dict(g=row(ln_mix[1]), wdq=bf(b_w_dq[0]), gq=row(b_q_norm[0]), wq=bf(wq), wqs=bf(wqs),
               bd=_block_diag_mean([(0, QK_NOPE), (QK_NOPE, QK_ROPE)]),
               gh=_head_pad_vec([(0, b_qn_nope[0]), (QK_NOPE, gq_rope)]) * EXP2_SCALE,
               ghs=_head_pad_vec([(QK_NOPE, gq_rope[swap])]) * EXP2_SCALE)
    wo = bf(_head_rows(b_w_o[0].reshape(MLA_HEADS, V_HEAD, d_model)))

    wkv3 = w_kv_b.reshape(kv_lora, MLA_HEADS, per_kv)
    wq3 = wqb.reshape(wqb.shape[0], MLA_HEADS, per_q)
    flat = lambda w3: bf(w3.reshape(w3.shape[0], -1))
    per_head = lambda v: jnp.tile(v.astype(F32), MLA_HEADS)[None, :]
    flat_w = dict(
        wk=flat(wkv3[:, :, :QK_NOPE]), wv=flat(wkv3[:, :, QK_NOPE:]), gk=per_head(k_norm),
        wn=flat(wq3[:, :, :QK_NOPE]), wr=flat(wq3[:, :, QK_NOPE:]), wrs=flat(wq3[:, :, QK_NOPE:][:, :, swap]),
        bdn=_block_diag_mean([(o, QK_NOPE) for o in range(0, LANES, QK_NOPE)]),
        bdr=_block_diag_mean([(o, QK_ROPE) for o in range(0, LANES, QK_ROPE)]),
        gn=per_head(b_qn_nope[0]) * EXP2_SCALE, gr=per_head(gq_rope) * EXP2_SCALE,
        grs=per_head(gq_rope[swap]) * EXP2_SCALE, wo=bf(b_w_o[0]))

    def rope_tabs(pos):
        cos, sin = _rope_tables(pos)
        seq = pos.shape[0]
        cos32 = jnp.concatenate([cos, cos], axis=1)
        sin32 = jnp.concatenate([-sin, sin], axis=1)
        c128 = jnp.concatenate([jnp.ones((seq, QK_NOPE), F32), cos32,
                                jnp.zeros((seq, LANES - per_q), F32)], axis=1)
        s128 = jnp.concatenate([jnp.zeros((seq, QK_NOPE), F32), sin32,
                                jnp.zeros((seq, LANES - per_q), F32)], axis=1)
        return cos32, sin32, c128, s128

    def run_group(x, p, pos0, conv_in, ssm_in, past_ckv, past_kr):
        batch, seq, _ = x.shape
        m = batch * seq
        pos = pos0 + jnp.arange(seq, dtype=jnp.int32)
        cos32, sin32, c128, s128 = rope_tabs(pos)
        h = x.reshape(m, d_model)
        p2 = p.reshape(depth, m, p.shape[-1])

        h = _ffn(h, 0, *ffn1)
        mw = mamba_w
        z, xbc, dt = _inproj(h, mw["g"], mw["w_in"], (d_inner, conv_dim, heads))
        conv8 = jnp.pad(conv_in, ((0, 0), (SUBLANES - (CONV_W - 1), 0), (0, 0)))
        ssm4 = ssm_in.reshape(batch, SSM_GROUPS, d_inner // SSM_GROUPS, D_STATE)
        h, conv_out, ssm_out = _mamba(z, xbc, dt, h, conv8, ssm4, mw["convw"], mw["convb"], mw["dtb"],
                                      mw["alog"], mw["dexp"], mw["ng"], mw["wout"], batch, seq)
        conv_out = conv_out[None, :, SUBLANES - (CONV_W - 1):, :]
        ssm_out = ssm_out.reshape(1, batch, heads, SSM_HEAD_DIM, D_STATE)
        h = _ffn_ple(h, p2, 0, *ffn2)

        kw = kv_w
        ckv_new, kr_new = _kvside(h, kw["g"], kw["wc"], kw["wr"], kw["wrs"], kw["gc"], kw["gr"], kw["grs"],
                                  cos32, sin32, seq)

        h = _ffn(h, 1, *ffn1)
        qw = q_w
        if past_ckv is None:
            ew = ex_w
            k_all, vt_all = _expand(ckv_new, kr_new, ew["wk"], ew["wvt"], ew["bd"], ew["gk"], ew["place"],
                                    ew["one"], seq, min(ATTN_TILE, seq))
            q = _queries(h, qw["g"], qw["wdq"], qw["gq"], qw["wq"], qw["wqs"], qw["bd"], qw["gh"], qw["ghs"],
                         c128, s128, seq)
            width = MLA_HEADS * LANES
            h = _self_attention(q.reshape(batch, seq, width), k_all.reshape(batch, seq, width), vt_all,
                                h.reshape(batch, seq, d_model), wo, batch)
        else:
            fw = flat_w
            cos_f, sin_f = jnp.tile(cos32, (1, MLA_HEADS)), jnp.tile(sin32, (1, MLA_HEADS))
            qn, qr = _queries_flat(h, qw["g"], qw["wdq"], qw["gq"], fw["wn"], fw["wr"], fw["wrs"], fw["bdn"],
                                   fw["bdr"], fw["gn"], fw["gr"], fw["grs"], cos_f, sin_f, seq)
            h = _cache_attention(qn.reshape(batch, seq, -1), qr.reshape(batch, seq, -1), past_ckv, past_kr,
                                 ckv_new.reshape(batch, seq, kv_lora), kr_new.reshape(batch, seq, QK_ROPE),
                                 h.reshape(batch, seq, d_model), fw["wk"], fw["wv"], fw["bdn"], fw["gk"], fw["wo"])
        h = _ffn_ple(h.reshape(m, d_model), p2, 1, *ffn2)
        return (h.reshape(batch, seq, d_model), conv_out, ssm_out,
                ckv_new.reshape(batch, seq, kv_lora), kr_new.reshape(batch, seq, QK_ROPE))

    b_p = x_prompt.shape[0]
    conv0 = jnp.zeros((b_p, CONV_W - 1, conv_dim), F32)
    ssm0 = jnp.zeros((b_p, heads, SSM_HEAD_DIM, D_STATE), F32)
    y_p, conv_p, ssm_p, kv_p, kr_p = run_group(x_prompt, p_prompt, 0, conv0, ssm0, None, None)
    y_s, conv_s, ssm_s, kv_s, kr_s = run_group(x_sample, p_sample, cache_kv_latent.shape[1],
                                               state_conv[0], state_ssm[0], cache_kv_latent, cache_k_rope)
    return (y_p, y_s, conv_p, ssm_p, kv_p, kr_p, conv_s, ssm_s, kv_s, kr_s)
```

```python
import functools

import numpy as np
import jax
import jax.numpy as jnp
from jax import lax
from jax.experimental import pallas as pl
from jax.experimental.pallas import tpu as pltpu

F32 = jnp.float32
BF16 = jnp.bfloat16

EPS = 1e-6
CHUNK = 64
SSM_HEAD_DIM = 64
SSM_GROUPS = 4
D_STATE = 128
CONV_W = 4
MLA_HEADS = 16
QK_NOPE = 64
QK_ROPE = 32
V_HEAD = 64
ROPE_BASE = 10000.0
ATTN_SCALE = (QK_NOPE + QK_ROPE) ** -0.5
EXP2_SCALE = ATTN_SCALE * float(np.log2(np.e))

LANES = 128
VT_ROWS = V_HEAD + 16
SUBLANES = 8
MXU_DIM = 256
VMEM_LIMIT_BYTES = 56 * 1024 * 1024
ROW_TILE = 512
ATTN_TILE = 256
MAMBA_TILE = 256
CONV_ROWS = 32


def _cparams(n_grid):
    return pltpu.CompilerParams(dimension_semantics=("arbitrary",) * n_grid,
                                vmem_limit_bytes=VMEM_LIMIT_BYTES)


def _const_spec(shape):
    nd = len(shape)
    return pl.BlockSpec(shape, lambda *_: (0,) * nd, pipeline_mode=pl.Buffered(1))


def _row_spec(tm, width):
    return pl.BlockSpec((tm, width), lambda i: (i, 0))


def _row_tile(m):
    tm = min(ROW_TILE, m)
    assert m % tm == 0 and tm % SUBLANES == 0
    return tm


def _dot(a, b):
    return jnp.dot(a, b, preferred_element_type=F32)


def _dot_nt(a, b):
    return lax.dot_general(a, b, (((1,), (1,)), ((), ())), preferred_element_type=F32)


def _dot_tn(a, b):
    return lax.dot_general(a, b, (((0,), (0,)), ((), ())), preferred_element_type=F32)


def _rms(x, g):
    ms = jnp.mean(x * x, axis=-1, keepdims=True)
    return x * lax.rsqrt(ms + EPS) * g


def _silu(x):
    return x * jax.nn.sigmoid(x)


def _softplus(x):
    return jnp.maximum(x, 0.0) + jnp.log1p(jnp.exp(-jnp.abs(x)))


def _split_bf16(v, n):
    pieces = []
    r = v
    for _ in range(n):
        p = r.astype(BF16)
        pieces.append(p)
        r = r - p.astype(F32)
    return pieces


def _ffn_body(h, g, wg_ref, wu_ref, wd_ref):
    u = _rms(h, g).astype(BF16)
    a = _dot(u, wg_ref[...])
    b = _dot(u, wu_ref[...])
    act = (_silu(a) * b).astype(BF16)
    return h + 0.5 * _dot(act, wd_ref[...])


def _ffn_kernel(h_ref, g_ref, wg_ref, wu_ref, wd_ref, o_ref):
    o_ref[...] = _ffn_body(h_ref[...], g_ref[...], wg_ref, wu_ref, wd_ref)


def _ffn_ple_kernel(h_ref, p_ref, g_ref, wg_ref, wu_ref, wd_ref, gp_ref, wpi_ref, wpg_ref, o_ref):
    h2 = _ffn_body(h_ref[...], g_ref[...], wg_ref, wu_ref, wd_ref)
    gate = jax.nn.sigmoid(_dot(_rms(h2, gp_ref[...]).astype(BF16), wpg_ref[...]))
    o_ref[...] = h2 + _dot(p_ref[...].astype(BF16), wpi_ref[...]) * gate


def _layer_spec(w, layer):
    return pl.BlockSpec((None,) + w.shape[1:], lambda *_: (layer, 0, 0), pipeline_mode=pl.Buffered(1))


def _ffn(h, layer, g, wg, wu, wd):
    m, d = h.shape
    tm = _row_tile(m)
    return pl.pallas_call(
        _ffn_kernel,
        grid=(m // tm,),
        in_specs=[_row_spec(tm, d)] + [_layer_spec(w, layer) for w in (g, wg, wu, wd)],
        out_specs=_row_spec(tm, d),
        out_shape=jax.ShapeDtypeStruct((m, d), F32),
        compiler_params=_cparams(1),
    )(h, g, wg, wu, wd)


def _ffn_ple(h, p, layer, g, wg, wu, wd, gp, wpi, wpg):
    m, d = h.shape
    tm = _row_tile(m)
    params = (g, wg, wu, wd, gp, wpi, wpg)
    return pl.pallas_call(
        _ffn_ple_kernel,
        grid=(m // tm,),
        in_specs=[_row_spec(tm, d), pl.BlockSpec((None, tm, p.shape[2]), lambda i: (layer, i, 0))]
                 + [_layer_spec(w, layer) for w in params],
        out_specs=_row_spec(tm, d),
        out_shape=jax.ShapeDtypeStruct((m, d), F32),
        compiler_params=_cparams(1),
    )(h, p, *params)


def _inproj_kernel(h_ref, g_ref, w_ref, z_ref, xbc_ref, dt_ref):
    u = _rms(h_ref[...], g_ref[...]).astype(BF16)
    lo = 0
    for out in (z_ref, xbc_ref, dt_ref):
        hi = lo + out.shape[1]
        out[...] = _dot(u, w_ref[:, lo:hi])
        lo = hi


def _inproj(h, g, w, widths):
    m, d = h.shape
    tm = _row_tile(m)
    assert sum(widths) == w.shape[1] and all(x % LANES == 0 for x in widths[:-1])
    return pl.pallas_call(
        _inproj_kernel,
        grid=(m // tm,),
        in_specs=[_row_spec(tm, d), _const_spec(g.shape), _const_spec(w.shape)],
        out_specs=[_row_spec(tm, x) for x in widths],
        out_shape=[jax.ShapeDtypeStruct((m, x), F32) for x in widths],
        compiler_params=_cparams(1),
    )(h, g, w)


def _mamba_kernel(chunk, z_ref, xbc_ref, dt_ref, h_ref, cst_ref, sst_ref,
                  convw_ref, convb_ref, dtb_ref, alog_ref, dexp_ref, ng_ref, wout_ref,
                  blt_ref, e2_ref, i2_ref,
                  o_ref, cout_ref, sout_ref,
                  xpad, st, x_s, bc_s, ae_s, de_s, y_s, at_s, yn_s):
    t = pl.program_id(1)
    n_t = pl.num_programs(1)
    tb = xbc_ref.shape[0]
    d_inner = x_s.shape[1]
    gn = SSM_GROUPS * D_STATE
    n_chunks = tb // chunk
    gw = d_inner // SSM_GROUPS
    half = LANES // 2
    pad_rows = half - chunk

    @pl.when(t == 0)
    def _():
        xpad[0:SUBLANES, :] = cst_ref[0]
        for g in range(SSM_GROUPS):
            st[g] = sst_ref[0, g].T

    xpad[SUBLANES:SUBLANES + tb, :] = xbc_ref[...]
    w = convw_ref[...]
    base = SUBLANES - (CONV_W - 1)
    rb = min(CONV_ROWS, chunk)
    row8 = lax.broadcasted_iota(jnp.int32, (SUBLANES, xpad.shape[1]), 0)

    def conv_rows(r0):
        ext = xpad[r0:r0 + rb + SUBLANES, :]
        slabs = [ext[s:s + SUBLANES, :] for s in range(0, rb + SUBLANES, SUBLANES)]
        ys = [convb_ref[...] + sl * w[CONV_W - 1:CONV_W, :] for sl in slabs[1:]]
        for k in range(CONV_W - 1):
            back = CONV_W - 1 - k
            rolled = [pltpu.roll(sl, back, axis=0) for sl in slabs]
            for n in range(len(ys)):
                ys[n] = ys[n] + jnp.where(row8 < back, rolled[n], rolled[n + 1]) * w[k:k + 1, :]
        y = jnp.concatenate(ys, axis=0)
        act = _silu(y)
        x_s[r0:r0 + rb, :] = act[:, :d_inner]
        bc_s[r0:r0 + rb, :] = act[:, d_inner:]

    dt =_softplus(dt_ref[...] + dtb_ref[...])
    a = dt * (-jnp.exp(alog_ref[...]))
    blt = blt_ref[...]
    acs = sum(_dot(blt, p) for p in _split_bf16(a, 3))
    e2 = e2_ref[...]
    ae_s[...] = _dot(jnp.concatenate(_split_bf16(acs, 2), axis=1), e2)
    de_s[...] = _dot(jnp.concatenate(_split_bf16(dt, 2), axis=1), e2)
    i2 = i2_ref[...]
    at_all = sum(_dot_tn(p, i2) for p in _split_bf16(acs, 3))
    for c in range(n_chunks):
        at_s[c] = at_all[:, c * LANES:(c + 1) * LANES]

    lane = lax.broadcasted_iota(jnp.int32, (chunk, LANES), 1)
    row = lax.broadcasted_iota(jnp.int32, (chunk, LANES), 0)
    low_half = lane < half
    src = lane & (half - 1)
    causal = (src <= row) & (src < chunk)
    dexp = dexp_ref[...]

    def stack_pair(top, bot):
        if pad_rows == 0:
            return jnp.concatenate([top, bot], axis=0)
        zeros = jnp.zeros((pad_rows, LANES), top.dtype)
        return jnp.concatenate([top, zeros, bot, zeros], axis=0)

    def chunk_body(c, carry):
        rows = pl.ds(c * chunk, chunk)
        ae = ae_s[rows, :]
        xc = x_s[rows, :]
        last = ae[chunk - 1:chunk, :]
        xdt = xc * de_s[rows, :]
        xw = (xdt * jnp.exp(last - ae)).astype(BF16)
        xdt_b = xdt.astype(BF16)
        e_acs = jnp.exp(ae)
        decay = jnp.exp(last)
        bb = bc_s[rows, 0:gn].astype(BF16)
        cb = bc_s[rows, gn:2 * gn].astype(BF16)
        at2 = at_s[c]
        for g in range(SSM_GROUPS):
            gl = slice(g * gw, (g + 1) * gw)
            c_g = cb[:, g * D_STATE:(g + 1) * D_STATE]
            b_g = bb[:, g * D_STATE:(g + 1) * D_STATE]
            cb2 = _dot_nt(c_g, stack_pair(b_g, b_g))
            s_g = st[g]
            y_off = _dot(c_g, s_g.astype(BF16)) * e_acs[:, gl]
            pairs = []
            for q in range(gw // LANES):
                pq = g * (gw // LANES) + q
                pl_ = slice(pq * LANES, (pq + 1) * LANES)
                rowvec = jnp.where(low_half[0:1, :], at2[2 * pq:2 * pq + 1, :], at2[2 * pq + 1:2 * pq + 2, :])
                seg = jnp.exp(jnp.where(causal, ae[:, pl_] - rowvec, -jnp.inf))
                wmat = (cb2 * seg).astype(BF16)
                xp = xdt_b[:, pl_]
                zero = jnp.zeros_like(xp)
                x2 = stack_pair(jnp.where(low_half, xp, zero), jnp.where(low_half, zero, xp))
                pairs.append(_dot(wmat, x2))
            y_diag = jnp.concatenate(pairs, axis=1)
            y_s[rows, gl] = y_diag + y_off + dexp[:, gl] * xc[:, gl]
            st[g] = s_g * decay[:, gl] + _dot_tn(b_g, xw[:, gl])
        return carry

    for c in range(n_chunks):
        for r0 in range(c * chunk, (c + 1) * chunk, rb):
            conv_rows(r0)
        chunk_body(c, 0)
        rows = pl.ds(c * chunk, chunk)
        yz = y_s[rows, :] * _silu(z_ref[rows, :])
        yn_s[rows, :] = _rms(yz, ng_ref[...]).astype(BF16)
    tail = xpad[tb:tb + SUBLANES, :]
    cout_ref[0] = tail
    xpad[0:SUBLANES, :] = tail
    o_ref[...] = h_ref[...] + _dot(yn_s[...], wout_ref[...])

    @pl.when(t == n_t - 1)
    def _():
        for g in range(SSM_GROUPS):
            sout_ref[0, g] = st[g].T


def _mamba(z, xbc, dt, h, conv_state8, ssm_state, convw, convb, dtb, alog, dexp, ng, wout, batch, seq):
    m, d_inner = z.shape
    conv_dim = xbc.shape[1]
    heads = dt.shape[1]
    d = h.shape[1]
    tb = min(MAMBA_TILE, seq)
    chunk = min(CHUNK, seq)
    assert seq % tb == 0 and tb % chunk == 0 and chunk % SUBLANES == 0 and LANES // 2 % chunk == 0
    n_t = seq // tb
    n_chunks = tb // chunk
    gw = d_inner // SSM_GROUPS

    r = np.arange(tb)
    blt = ((r[:, None] // chunk == r[None, :] // chunk) & (r[None, :] <= r[:, None])).astype(np.float32)
    e = (np.arange(d_inner)[None, :] // SSM_HEAD_DIM == np.arange(heads)[:, None]).astype(np.float32)
    e = np.concatenate([e, e], axis=0)
    col = np.arange(n_chunks * LANES)
    i2 = ((col[None, :] // LANES == r[:, None] // chunk)
          & (col[None, :] % (LANES // 2) == r[:, None] % chunk)).astype(np.float32)
    blt, e, i2 = (jnp.asarray(v, BF16) for v in (blt, e, i2))

    row_map = lambda b, t: (b * n_t + t, 0)
    const = lambda v: _const_spec(v.shape)
    return pl.pallas_call(
        functools.partial(_mamba_kernel, chunk),
        grid=(batch, n_t),
        in_specs=[pl.BlockSpec((tb, d_inner), row_map), pl.BlockSpec((tb, conv_dim), row_map),
                  pl.BlockSpec((tb, heads), row_map), pl.BlockSpec((tb, d), row_map),
                  pl.BlockSpec((1, SUBLANES, conv_dim), lambda b, t: (b, 0, 0)),
                  pl.BlockSpec((1, SSM_GROUPS, gw, D_STATE), lambda b, t: (b, 0, 0, 0)),
                  const(convw), const(convb), const(dtb), const(alog), const(dexp), const(ng), const(wout),
                  const(blt), const(e), const(i2)],
        out_specs=[pl.BlockSpec((tb, d), row_map),
                   pl.BlockSpec((1, SUBLANES, conv_dim), lambda b, t: (b, 0, 0)),
                   pl.BlockSpec((1, SSM_GROUPS, gw, D_STATE), lambda b, t: (b, 0, 0, 0))],
        out_shape=[jax.ShapeDtypeStruct((m, d), F32),
                   jax.ShapeDtypeStruct((batch, SUBLANES, conv_dim), F32),
                   jax.ShapeDtypeStruct((batch, SSM_GROUPS, gw, D_STATE), F32)],
        scratch_shapes=[pltpu.VMEM((tb + SUBLANES, conv_dim), F32),
                        pltpu.VMEM((SSM_GROUPS, D_STATE, gw), F32),
                        pltpu.VMEM((tb, d_inner), F32),
                        pltpu.VMEM((tb, conv_dim - d_inner), F32),
                        pltpu.VMEM((tb, d_inner), F32),
                        pltpu.VMEM((tb, d_inner), F32),
                        pltpu.VMEM((tb, d_inner), F32),
                        pltpu.VMEM((n_chunks, heads, LANES), F32),
                        pltpu.VMEM((tb, d_inner), BF16)],
        compiler_params=_cparams(2),
    )(z, xbc, dt, h, conv_state8, ssm_state, convw, convb, dtb, alog, dexp, ng, wout, blt, e, i2)


def _table_spec(tm, seq, width):
    if seq <= tm:
        return pl.BlockSpec((tm, width), lambda i: (0, 0))
    per = seq // tm
    return pl.BlockSpec((tm, width), lambda i: (i % per, 0))


def _tile_table(tab, tm, seq):
    return jnp.tile(tab, (tm // seq, 1)) if seq < tm else tab


def _kvside_kernel(h_ref, g_ref, wc_ref, wr_ref, wrs_ref, gc_ref, gr_ref, grs_ref, cos_ref, sin_ref,
                   ckv_ref, kr_ref):
    u = _rms(h_ref[...], g_ref[...]).astype(BF16)
    ckv_ref[...] = _rms(_dot(u, wc_ref[...]), gc_ref[...])
    r = _dot(u, wr_ref[...])
    rs = _dot(u, wrs_ref[...])
    inv = lax.rsqrt(jnp.mean(r * r, axis=-1, keepdims=True) + EPS)
    kr_ref[...] = r * inv * gr_ref[...] * cos_ref[...] + rs * inv * grs_ref[...] * sin_ref[...]


def _kvside(h, g, wc, wr, wrs, gc, gr, grs, cos32, sin32, seq):
    m, d = h.shape
    tm = _row_tile(m)
    assert tm % seq == 0 or seq % tm == 0
    cos32, sin32 = _tile_table(cos32, tm, seq), _tile_table(sin32, tm, seq)
    consts = (g, wc, wr, wrs, gc, gr, grs)
    return pl.pallas_call(
        _kvside_kernel,
        grid=(m // tm,),
        in_specs=[_row_spec(tm, d)] + [_const_spec(v.shape) for v in consts]
                 + [_table_spec(tm, seq, QK_ROPE)] * 2,
        out_specs=[_row_spec(tm, wc.shape[1]), _row_spec(tm, QK_ROPE)],
        out_shape=[jax.ShapeDtypeStruct((m, wc.shape[1]), F32), jax.ShapeDtypeStruct((m, QK_ROPE), F32)],
        compiler_params=_cparams(1),
    )(h, *consts, cos32, sin32)


def _group_mean_sq(x, bd_ref):
    bd = bd_ref[...]
    sq = (x * x).astype(BF16)
    return jnp.concatenate([_dot(sq[:, s:s + MXU_DIM], bd) for s in range(0, x.shape[1], MXU_DIM)], axis=1)


def _expand_kernel(ckv_ref, kr_ref, wk_ref, wvt_ref, bd_ref, gk_ref, place_ref, one_ref, k_ref, vt_ref):
    c = ckv_ref[...].astype(BF16)
    k = _dot(c, wk_ref[...])
    kn = k * lax.rsqrt(_group_mean_sq(k, bd_ref) + EPS) * gk_ref[...]
    kr = _dot(kr_ref[...].astype(BF16), place_ref[...])
    for hh in range(MLA_HEADS):
        hl = slice(hh * LANES, (hh + 1) * LANES)
        k_ref[:, hl] = (kn[:, hl] + kr).astype(BF16)
    vt = _dot_nt(wvt_ref[...], c)
    one = one_ref[...]
    tk = vt_ref.shape[3]
    for j in range(vt_ref.shape[1]):
        for t in range(0, tk, LANES):
            vt_ref[0, j, :, t:t + LANES] = (vt[:, j * tk + t:j * tk + t + LANES] + one).astype(BF16)


def _expand(ckv, kr, wk, wvt, bd, gk, place, one, seq, tk):
    m = ckv.shape[0]
    tm = min(_row_tile(m), seq)
    assert seq % tm == 0 and tm % tk == 0
    per = seq // tm
    consts = (wk, wvt, bd, gk, place, one)
    width = MLA_HEADS * LANES
    return pl.pallas_call(
        _expand_kernel,
        grid=(m // tm,),
        in_specs=[_row_spec(tm, ckv.shape[1]), _row_spec(tm, kr.shape[1])]
                 + [_const_spec(v.shape) for v in consts],
        out_specs=[_row_spec(tm, width),
                   pl.BlockSpec((1, tm // tk, wvt.shape[0], tk), lambda i: (i // per, i % per, 0, 0))],
        out_shape=[jax.ShapeDtypeStruct((m, width), BF16),
                   jax.ShapeDtypeStruct((m // seq, seq // tk, wvt.shape[0], tk), BF16)],
        compiler_params=_cparams(1),
    )(ckv, kr, *consts)


def _q_kernel(h_ref, g_ref, wdq_ref, gq_ref, wq_ref, bd_ref, gh_ref, tab_ref, q_ref):
    u = _rms(h_ref[...], g_ref[...]).astype(BF16)
    cq = _rms(_dot(u, wdq_ref[...]), gq_ref[...]).astype(BF16)
    q = _dot(cq, wq_ref[...])
    qn = q * lax.rsqrt(_group_mean_sq(q, bd_ref) + EPS) * gh_ref[...]
    tab = tab_ref[...]
    for hh in range(MLA_HEADS):
        hl = slice(hh * LANES, (hh + 1) * LANES)
        q_ref[:, hl] = (qn[:, hl] * tab).astype(BF16)


def _queries(h, g, wdq, gq, wq, bd, gh, tab, seq):
    m, d = h.shape
    tm = _row_tile(m)
    assert tm % seq == 0 or seq % tm == 0
    tab = _tile_table(tab, tm, seq)
    consts = (g, wdq, gq, wq, bd, gh)
    width = MLA_HEADS * LANES
    return pl.pallas_call(
        _q_kernel,
        grid=(m // tm,),
        in_specs=[_row_spec(tm, d)] + [_const_spec(v.shape) for v in consts]
                 + [_table_spec(tm, seq, LANES)],
        out_specs=_row_spec(tm, width),
        out_shape=jax.ShapeDtypeStruct((m, width), BF16),
        compiler_params=_cparams(1),
    )(h, *consts, tab)


def _q_flat_kernel(h_ref, g_ref, wdq_ref, gq_ref, wn_ref, wr_ref, wrs_ref, bdn_ref, bdr_ref,
                   gn_ref, gr_ref, grs_ref, cos_ref, sin_ref, qn_ref, qr_ref):
    u = _rms(h_ref[...], g_ref[...]).astype(BF16)
    cq = _rms(_dot(u, wdq_ref[...]), gq_ref[...]).astype(BF16)
    qn = _dot(cq, wn_ref[...])
    qn_ref[...] = (qn * lax.rsqrt(_group_mean_sq(qn, bdn_ref) + EPS) * gn_ref[...]).astype(BF16)
    r = _dot(cq, wr_ref[...])
    rs = _dot(cq, wrs_ref[...])
    inv = lax.rsqrt(_group_mean_sq(r, bdr_ref) + EPS)
    qr_ref[...] = (r * inv * gr_ref[...] * cos_ref[...] + rs * inv * grs_ref[...] * sin_ref[...]).astype(BF16)


def _queries_flat(h, g, wdq, gq, wn, wr, wrs, bdn, bdr, gn, gr, grs, cos, sin, seq):
    m, d = h.shape
    tm = _row_tile(m)
    assert tm % seq == 0 or seq % tm == 0
    cos, sin = _tile_table(cos, tm, seq), _tile_table(sin, tm, seq)
    consts = (g, wdq, gq, wn, wr, wrs, bdn, bdr, gn, gr, grs)
    widths = (wn.shape[1], wr.shape[1])
    return pl.pallas_call(
        _q_flat_kernel,
        grid=(m // tm,),
        in_specs=[_row_spec(tm, d)] + [_const_spec(v.shape) for v in consts]
                 + [_table_spec(tm, seq, widths[1])] * 2,
        out_specs=[_row_spec(tm, w) for w in widths],
        out_shape=[jax.ShapeDtypeStruct((m, w), BF16) for w in widths],
        compiler_params=_cparams(1),
    )(h, *consts, cos, sin)


def _cache_attn_kernel(kb, qn_ref, qr_ref, ckv_ref, kr_ref, ckvn_ref, krn_ref, h_ref,
                       wk_ref, wv_ref, bd_ref, gk_ref, wo_ref, o_ref, s_s, v_s, acc_s):
    tq = qn_ref.shape[1]
    past = ckv_ref.shape[1]
    cols = MLA_HEADS * tq
    chunk_shift = CHUNK.bit_length() - 1
    assert CHUNK == 1 << chunk_shift and past % kb == 0

    qn = jnp.concatenate([qn_ref[0]] * MLA_HEADS, axis=0)
    tq_shift, nope_shift = tq.bit_length() - 1, QK_NOPE.bit_length() - 1
    assert tq == 1 << tq_shift and QK_NOPE == 1 << nope_shift
    r_head = lax.broadcasted_iota(jnp.int32, qn.shape, 0) >> tq_shift
    c_head = lax.broadcasted_iota(jnp.int32, qn.shape, 1) >> nope_shift
    q_bd = jnp.where(r_head == c_head, qn, jnp.zeros_like(qn))
    qr_all = qr_ref[0]
    q_rope = jnp.concatenate([qr_all[:, hh * QK_ROPE:(hh + 1) * QK_ROPE] for hh in range(MLA_HEADS)], axis=0)

    def expand(c, kr, rows):
        c = c.astype(BF16)
        k = _dot(c, wk_ref[...])
        kn = (k * lax.rsqrt(_group_mean_sq(k, bd_ref) + EPS) * gk_ref[...]).astype(BF16)
        s = _dot_nt(kn, q_bd) + _dot_nt(kr.astype(BF16), q_rope)
        v = _dot(c, wv_ref[...]).astype(BF16)
        v_s[rows, :] = jnp.concatenate([v, jnp.ones((v.shape[0], LANES), BF16)], axis=1)
        return s

    def expand_block(j, m_run):
        rows = pl.ds(pl.multiple_of(j * kb, kb), kb)
        s = expand(ckv_ref[0, rows, :], kr_ref[0, rows, :], rows)
        s_s[rows, :] = s
        return jnp.maximum(m_run, jnp.max(s, axis=0, keepdims=True))

    m_run = lax.fori_loop(0, past // kb, expand_block, jnp.full((1, cols), -jnp.inf, F32))
    new_rows = pl.ds(past, tq)
    s_new = expand(ckvn_ref[0], krn_ref[0], new_rows)
    k_chunk = (past + lax.broadcasted_iota(jnp.int32, (tq, cols), 0)) >> chunk_shift
    q_chunk = (past + (lax.broadcasted_iota(jnp.int32, (tq, cols), 1) & (tq - 1))) >> chunk_shift
    s_new = jnp.where(k_chunk <= q_chunk, s_new, -jnp.inf)
    m_all = jnp.maximum(m_run, jnp.max(s_new, axis=0, keepdims=True))

    acc_s[...] = _dot_tn(jnp.exp2(s_new - m_all).astype(BF16), v_s[new_rows, :])

    def pv_block(j, carry):
        rows = pl.ds(pl.multiple_of(j * kb, kb), kb)
        p = jnp.exp2(s_s[rows, :] - m_all).astype(BF16)
        acc_s[...] += _dot_tn(p, v_s[rows, :])
        return carry

    lax.fori_loop(0, past // kb, pv_block, 0)

    width = MLA_HEADS * V_HEAD
    low_half = lax.broadcasted_iota(jnp.int32, (tq, LANES), 1) < V_HEAD
    tiles = []
    for c in range(width // LANES):
        ev = acc_s[2 * c * tq:(2 * c + 1) * tq, :]
        od = acc_s[(2 * c + 1) * tq:(2 * c + 2) * tq, :]
        cl = slice(c * LANES, (c + 1) * LANES)
        tiles.append(jnp.where(low_half, ev[:, cl] / ev[:, width:], od[:, cl] / od[:, width:]))
    o = jnp.concatenate(tiles, axis=1).astype(BF16)
    o_ref[0] = h_ref[0] + _dot(o, wo_ref[...])


def _cache_attention(qn, qr, ckv, kr, ckv_new, kr_new, h, wk, wv, bd, gk, wo):
    batch, tq, _ = qn.shape
    past = ckv.shape[1]
    d = h.shape[2]
    kb = min(ROW_TILE, past)
    assert past % kb == 0 and tq % 16 == 0 and (MLA_HEADS * tq) % LANES == 0
    per = lambda a: pl.BlockSpec((1,) + a.shape[1:], lambda b: (b, 0, 0))
    consts = (wk, wv, bd, gk, wo)
    return pl.pallas_call(
        functools.partial(_cache_attn_kernel, kb),
        grid=(batch,),
        in_specs=[per(qn), per(qr), per(ckv), per(kr), per(ckv_new), per(kr_new), per(h)]
                 + [_const_spec(v.shape) for v in consts],
        out_specs=per(h),
        out_shape=jax.ShapeDtypeStruct(h.shape, F32),
        scratch_shapes=[pltpu.VMEM((past + tq, MLA_HEADS * tq), F32),
                        pltpu.VMEM((past + tq, MLA_HEADS * V_HEAD + LANES), BF16),
                        pltpu.VMEM((MLA_HEADS * tq, MLA_HEADS * V_HEAD + LANES), F32)],
        compiler_params=_cparams(1),
    )(qn, qr, ckv, kr, ckv_new, kr_new, h, *consts)


def _self_attn_kernel(q_ref, k_ref, vt_ref, h_ref, wo_ref, o_ref, m_s, acc_s, ot_s, st_s):
    i = pl.program_id(1)
    tq = q_ref.shape[1]
    chunk_shift = CHUNK.bit_length() - 1
    assert CHUNK == 1 << chunk_shift
    visible = ((lax.broadcasted_iota(jnp.int32, (tq, tq), 0) >> chunk_shift)
               <= (lax.broadcasted_iota(jnp.int32, (tq, tq), 1) >> chunk_shift))

    m_s[...] = jnp.full(m_s.shape, -jnp.inf, F32)
    acc_s[...] = jnp.zeros(acc_s.shape, F32)

    def step(j, masked):
        k0 = pl.multiple_of(j * tq, tq)
        for hh in range(MLA_HEADS):
            hl = slice(hh * LANES, (hh + 1) * LANES)
            st_s[hh] = _dot_nt(k_ref[0, pl.ds(k0, tq), hl], q_ref[0, :, hl])
        for hh in range(MLA_HEADS):
            st = st_s[hh]
            if masked:
                st = jnp.where(visible, st, -jnp.inf)
            m_old = m_s[hh]
            m_new = jnp.maximum(m_old, jnp.max(st, axis=0, keepdims=True))
            alpha = jnp.exp2(m_old - m_new)
            pt = jnp.exp2(st - m_new).astype(BF16)
            vt = vt_ref[0, j, hh * VT_ROWS:(hh + 1) * VT_ROWS, :]
            acc_s[hh] = alpha * acc_s[hh] + _dot(vt, pt)
            m_s[hh] = m_new

    def full_step(j, carry):
        step(j, False)
        return carry

    lax.fori_loop(0, i, full_step, 0)
    step(i, True)

    for hh in range(MLA_HEADS):
        acc = acc_s[hh]
        ot_s[hh * VT_ROWS:(hh + 1) * VT_ROWS, :] = (acc / acc[V_HEAD:V_HEAD + 1, :]).astype(BF16)
    o_ref[0] = h_ref[0] + _dot_tn(ot_s[...], wo_ref[...])


def _self_attention(q, k, vt, h, wo, batch):
    width = q.shape[2]
    seq = q.shape[1]
    d = h.shape[2]
    tq = vt.shape[3]
    vrows = vt.shape[2]
    assert seq % tq == 0 and tq % CHUNK == 0 and vt.shape[1] * tq == seq and vrows == MLA_HEADS * VT_ROWS
    return pl.pallas_call(
        _self_attn_kernel,
        grid=(batch, seq // tq),
        in_specs=[pl.BlockSpec((1, tq, width), lambda b, i: (b, i, 0)),
                  pl.BlockSpec((1, seq, width), lambda b, i: (b, 0, 0)),
                  pl.BlockSpec((1, seq // tq, vrows, tq), lambda b, i: (b, 0, 0, 0)),
                  pl.BlockSpec((1, tq, d), lambda b, i: (b, i, 0)),
                  _const_spec(wo.shape)],
        out_specs=pl.BlockSpec((1, tq, d), lambda b, i: (b, i, 0)),
        out_shape=jax.ShapeDtypeStruct((batch, seq, d), F32),
        scratch_shapes=[pltpu.VMEM((MLA_HEADS, 1, tq), F32), pltpu.VMEM((MLA_HEADS, VT_ROWS, tq), F32),
                        pltpu.VMEM((vrows, tq), BF16), pltpu.VMEM((MLA_HEADS, tq, tq), F32)],
        compiler_params=_cparams(2),
    )(q, k, vt, h, wo)


def _head_pad_cols(w, per_head, start, count, offset=0):
    k = w.shape[0]
    w3 = w.reshape(k, MLA_HEADS, per_head)[:, :, start:start + count]
    out = jnp.zeros((k, MLA_HEADS, LANES), w.dtype)
    out = out.at[:, :, offset:offset + count].set(w3)
    return out.reshape(k, MLA_HEADS * LANES)


def _head_rows(w3):
    h, r, n = w3.shape
    return jnp.zeros((h, VT_ROWS, n), w3.dtype).at[:, :r].set(w3).reshape(h * VT_ROWS, n)


def _head_pad_vec(pieces):
    blk = jnp.zeros((LANES,), F32)
    for off, vec in pieces:
        blk = blk.at[off:off + vec.shape[0]].set(vec.astype(F32))
    return jnp.tile(blk, MLA_HEADS)[None, :]


def _block_diag_mean(groups):
    bd = np.zeros((MXU_DIM, MXU_DIM), np.float32)
    for tile in range(MXU_DIM // LANES):
        for off, size, *spread in groups:
            lo = tile * LANES + off
            bd[lo:lo + size, lo:lo + (spread[0] if spread else size)] = 1.0 / size
    return jnp.asarray(bd, BF16)


def _rope_tables(pos):
    half = QK_ROPE // 2
    inv = ROPE_BASE ** (-jnp.arange(half, dtype=F32) / half)
    ang = pos.astype(F32)[:, None] * inv[None, :]
    return jnp.cos(ang), jnp.sin(ang)


def kernel(x_prompt, x_sample, p_prompt, p_sample, state_conv, state_ssm, cache_kv_latent, cache_k_rope, ln_ffn1, w1_gate, w1_up, w1_down, ln_mix, ln_ffn2, w2_gate, w2_up, w2_down, ln_ple, w_ple_in, w_ple_gate, a_in_proj, a_conv_w, a_conv_b, a_dt_bias, a_A_log, a_D, a_norm, a_out_proj, ln_kv, w_kv_a, kv_norm, kr_norm, w_kv_b, k_norm, b_w_dq, b_q_norm, b_w_q_b, b_qn_nope, b_qn_rope, b_w_o):
    depth = ln_ffn1.shape[0]
    n_a = a_in_proj.shape[0]
    assert depth == 2 and n_a == 1 and b_w_dq.shape[0] == 1
    d_model = x_prompt.shape[-1]
    heads = a_dt_bias.shape[-1]
    d_inner = heads * SSM_HEAD_DIM
    conv_dim = a_conv_w.shape[-1]
    kv_lora = kv_norm.shape[0]
    half = QK_ROPE // 2
    bf = lambda w: w.astype(BF16)
    row = lambda v: v.astype(F32)[None, :]

    rows = lambda v: v.astype(F32)[:, None, :]
    ffn1 = (rows(ln_ffn1), bf(w1_gate), bf(w1_up), bf(w1_down))
    ffn2 = (rows(ln_ffn2), bf(w2_gate), bf(w2_up), bf(w2_down), rows(ln_ple), bf(w_ple_in), bf(w_ple_gate))
    mamba_w = dict(
        g=row(ln_mix[0]), w_in=bf(a_in_proj[0]), convw=a_conv_w[0], convb=row(a_conv_b[0]),
        dtb=row(a_dt_bias[0]), alog=row(a_A_log[0]), dexp=row(jnp.repeat(a_D[0], SSM_HEAD_DIM)),
        ng=row(a_norm[0]), wout=bf(a_out_proj[0]))

    swap = np.concatenate([np.arange(half, QK_ROPE), np.arange(half)])
    w_r = w_kv_a[:, kv_lora:]
    kv_w = dict(g=row(ln_kv), wc=bf(w_kv_a[:, :kv_lora]), wr=bf(w_r), wrs=bf(w_r[:, swap]),
                gc=row(kv_norm), gr=row(kr_norm), grs=row(kr_norm[swap]))
    per_kv = QK_NOPE + V_HEAD
    ex_w = dict(wk=bf(_head_pad_cols(w_kv_b, per_kv, 0, QK_NOPE)),
                wvt=bf(_head_rows(w_kv_b.reshape(kv_lora, MLA_HEADS, per_kv)[:, :, QK_NOPE:].transpose(1, 2, 0))),
                bd=_block_diag_mean([(0, QK_NOPE)]),
                gk=_head_pad_vec([(0, k_norm)]),
                place=jnp.asarray(np.eye(QK_ROPE, LANES, k=QK_NOPE) + np.eye(QK_ROPE, LANES, k=QK_NOPE + QK_ROPE), BF16),
                one=jnp.asarray(np.tile((np.arange(VT_ROWS) == V_HEAD).astype(np.float32)[:, None],
                                        (MLA_HEADS, LANES))))
    per_q = QK_NOPE + QK_ROPE
    wqb = b_w_q_b[0]
    wq = (_head_pad_cols(wqb, per_q, 0, per_q)
          + _head_pad_cols(wqb, per_q, QK_NOPE + half, half, offset=per_q)
          + _head_pad_cols(wqb, per_q, QK_NOPE, half, offset=per_q + half))
    gq_rope = b_qn_rope[0]
    q_w = dict(g=row(ln_mix[1]), wdq=bf(b_w_dq[0]), gq=row(b_q_norm[0]), wq=bf(wq),
               bd=_block_diag_mean([(0, QK_NOPE), (QK_NOPE, QK_ROPE, 2 * QK_ROPE)]),
               gh=_head_pad_vec([(0, b_qn_nope[0]), (QK_NOPE, gq_rope), (per_q, gq_rope[swap])]) * EXP2_SCALE)
    wo = bf(_head_rows(b_w_o[0].reshape(MLA_HEADS, V_HEAD, d_model)))

    wkv3 = w_kv_b.reshape(kv_lora, MLA_HEADS, per_kv)
    wq3 = wqb.reshape(wqb.shape[0], MLA_HEADS, per_q)
    flat = lambda w3: bf(w3.reshape(w3.shape[0], -1))
    per_head = lambda v: jnp.tile(v.astype(F32), MLA_HEADS)[None, :]
    flat_w = dict(
        wk=flat(wkv3[:, :, :QK_NOPE]), wv=flat(wkv3[:, :, QK_NOPE:]), gk=per_head(k_norm),
        wn=flat(wq3[:, :, :QK_NOPE]), wr=flat(wq3[:, :, QK_NOPE:]), wrs=flat(wq3[:, :, QK_NOPE:][:, :, swap]),
        bdn=_block_diag_mean([(o, QK_NOPE) for o in range(0, LANES, QK_NOPE)]),
        bdr=_block_diag_mean([(o, QK_ROPE) for o in range(0, LANES, QK_ROPE)]),
        gn=per_head(b_qn_nope[0]) * EXP2_SCALE, gr=per_head(gq_rope) * EXP2_SCALE,
        grs=per_head(gq_rope[swap]) * EXP2_SCALE, wo=bf(b_w_o[0]))

    def rope_tabs(pos):
        cos, sin = _rope_tables(pos)
        seq = pos.shape[0]
        cos32 = jnp.concatenate([cos, cos], axis=1)
        sin32 = jnp.concatenate([-sin, sin], axis=1)
        tab128 = jnp.concatenate([jnp.ones((seq, QK_NOPE), F32), cos32, sin32], axis=1)
        return cos32, sin32, tab128

    def run_group(x, p, pos0, conv_in, ssm_in, past_ckv, past_kr):
        batch, seq, _ = x.shape
        m = batch * seq
        pos = pos0 + jnp.arange(seq, dtype=jnp.int32)
        cos32, sin32, tab128 = rope_tabs(pos)
        h = x.reshape(m, d_model)
        p2 = p.reshape(depth, m, p.shape[-1])

        h = _ffn(h, 0, *ffn1)
        mw = mamba_w
        z, xbc, dt = _inproj(h, mw["g"], mw["w_in"], (d_inner, conv_dim, heads))
        conv8 = jnp.pad(conv_in, ((0, 0), (SUBLANES - (CONV_W - 1), 0), (0, 0)))
        ssm4 = ssm_in.reshape(batch, SSM_GROUPS, d_inner // SSM_GROUPS, D_STATE)
        h, conv_out, ssm_out = _mamba(z, xbc, dt, h, conv8, ssm4, mw["convw"], mw["convb"], mw["dtb"],
                                      mw["alog"], mw["dexp"], mw["ng"], mw["wout"], batch, seq)
        conv_out = conv_out[None, :, SUBLANES - (CONV_W - 1):, :]
        ssm_out = ssm_out.reshape(1, batch, heads, SSM_HEAD_DIM, D_STATE)
        h = _ffn_ple(h, p2, 0, *ffn2)

        kw = kv_w
        ckv_new, kr_new = _kvside(h, kw["g"], kw["wc"], kw["wr"], kw["wrs"], kw["gc"], kw["gr"], kw["grs"],
                                  cos32, sin32, seq)

        h = _ffn(h, 1, *ffn1)
        qw = q_w
        if past_ckv is None:
            ew = ex_w
            k_all, vt_all = _expand(ckv_new, kr_new, ew["wk"], ew["wvt"], ew["bd"], ew["gk"], ew["place"],
                                    ew["one"], seq, min(ATTN_TILE, seq))
            q = _queries(h, qw["g"], qw["wdq"], qw["gq"], qw["wq"], qw["bd"], qw["gh"], tab128, seq)
            width = MLA_HEADS * LANES
            h = _self_attention(q.reshape(batch, seq, width), k_all.reshape(batch, seq, width), vt_all,
                                h.reshape(batch, seq, d_model), wo, batch)
        else:
            fw = flat_w
            cos_f, sin_f = jnp.tile(cos32, (1, MLA_HEADS)), jnp.tile(sin32, (1, MLA_HEADS))
            qn, qr = _queries_flat(h, qw["g"], qw["wdq"], qw["gq"], fw["wn"], fw["wr"], fw["wrs"], fw["bdn"],
                                   fw["bdr"], fw["gn"], fw["gr"], fw["grs"], cos_f, sin_f, seq)
            h = _cache_attention(qn.reshape(batch, seq, -1), qr.reshape(batch, seq, -1), past_ckv, past_kr,
                                 ckv_new.reshape(batch, seq, kv_lora), kr_new.reshape(batch, seq, QK_ROPE),
                                 h.reshape(batch, seq, d_model), fw["wk"], fw["wv"], fw["bdn"], fw["gk"], fw["wo"])
        h = _ffn_ple(h.reshape(m, d_model), p2, 1, *ffn2)
        return (h.reshape(batch, seq, d_model), conv_out, ssm_out,
                ckv_new.reshape(batch, seq, kv_lora), kr_new.reshape(batch, seq, QK_ROPE))

    b_p = x_prompt.shape[0]
    conv0 = jnp.zeros((b_p, CONV_W - 1, conv_dim), F32)
    ssm0 = jnp.zeros((b_p, heads, SSM_HEAD_DIM, D_STATE), F32)
    y_p, conv_p, ssm_p, kv_p, kr_p = run_group(x_prompt, p_prompt, 0, conv0, ssm0, None, None)
    y_s, conv_s, ssm_s, kv_s, kr_s = run_group(x_sample, p_sample, cache_kv_latent.shape[1],
                                               state_conv[0], state_ssm[0], cache_kv_latent, cache_k_rope)
    return (y_p, y_s, conv_p, ssm_p, kv_p, kr_p, conv_s, ssm_s, kv_s, kr_s)
```

```python
import functools

import numpy as np
import jax
import jax.numpy as jnp
from jax import lax
from jax.experimental import pallas as pl
from jax.experimental.pallas import tpu as pltpu

F32 = jnp.float32
BF16 = jnp.bfloat16

EPS = 1e-6
CHUNK = 64
SSM_HEAD_DIM = 64
SSM_GROUPS = 4
D_STATE = 128
CONV_W = 4
MLA_HEADS = 16
QK_NOPE = 64
QK_ROPE = 32
V_HEAD = 64
ROPE_BASE = 10000.0
ATTN_SCALE = (QK_NOPE + QK_ROPE) ** -0.5
EXP2_SCALE = ATTN_SCALE * float(np.log2(np.e))

LANES = 128
VT_ROWS = V_HEAD + 16
SUBLANES = 8
MXU_DIM = 256
VMEM_LIMIT_BYTES = 56 * 1024 * 1024
ROW_TILE = 512
ATTN_TILE = 256
MAMBA_TILE = 256
CONV_ROWS = 32


def _cparams(n_grid):
    return pltpu.CompilerParams(dimension_semantics=("arbitrary",) * n_grid,
                                vmem_limit_bytes=VMEM_LIMIT_BYTES)


def _const_spec(shape):
    nd = len(shape)
    return pl.BlockSpec(shape, lambda *_: (0,) * nd, pipeline_mode=pl.Buffered(1))


def _row_spec(tm, width):
    return pl.BlockSpec((tm, width), lambda i: (i, 0))


def _row_tile(m):
    tm = min(ROW_TILE, m)
    assert m % tm == 0 and tm % SUBLANES == 0
    return tm


def _dot(a, b):
    return jnp.dot(a, b, preferred_element_type=F32)


def _dot_nt(a, b):
    return lax.dot_general(a, b, (((1,), (1,)), ((), ())), preferred_element_type=F32)


def _dot_tn(a, b):
    return lax.dot_general(a, b, (((0,), (0,)), ((), ())), preferred_element_type=F32)


def _rms(x, g):
    ms = jnp.mean(x * x, axis=-1, keepdims=True)
    return x * lax.rsqrt(ms + EPS) * g


def _silu(x):
    return x * jax.nn.sigmoid(x)


def _silu_tanh(x):
    half = 0.5 * x
    return half + half * jnp.tanh(half)


def _softplus(x):
    return jnp.maximum(x, 0.0) + jnp.log1p(jnp.exp(-jnp.abs(x)))


def _split_bf16(v, n):
    pieces = []
    r = v
    for _ in range(n):
        p = r.astype(BF16)
        pieces.append(p)
        r = r - p.astype(F32)
    return pieces


def _ffn_body(h, g, wg_ref, wu_ref, wd_ref):
    u = _rms(h, g).astype(BF16)
    a = _dot(u, wg_ref[...])
    b = _dot(u, wu_ref[...])
    act = (_silu(a) * b).astype(BF16)
    return h + 0.5 * _dot(act, wd_ref[...])


def _ffn_kernel(h_ref, g_ref, wg_ref, wu_ref, wd_ref, o_ref):
    o_ref[...] = _ffn_body(h_ref[...], g_ref[...], wg_ref, wu_ref, wd_ref)


def _ffn_ple_kernel(h_ref, p_ref, g_ref, wg_ref, wu_ref, wd_ref, gp_ref, wpi_ref, wpg_ref, o_ref):
    h2 = _ffn_body(h_ref[...], g_ref[...], wg_ref, wu_ref, wd_ref)
    gate = jax.nn.sigmoid(_dot(_rms(h2, gp_ref[...]).astype(BF16), wpg_ref[...]))
    o_ref[...] = h2 + _dot(p_ref[...].astype(BF16), wpi_ref[...]) * gate


def _layer_spec(w, layer):
    return pl.BlockSpec((None,) + w.shape[1:], lambda *_: (layer, 0, 0), pipeline_mode=pl.Buffered(1))


def _ffn(h, layer, g, wg, wu, wd):
    m, d = h.shape
    tm = _row_tile(m)
    return pl.pallas_call(
        _ffn_kernel,
        grid=(m // tm,),
        in_specs=[_row_spec(tm, d)] + [_layer_spec(w, layer) for w in (g, wg, wu, wd)],
        out_specs=_row_spec(tm, d),
        out_shape=jax.ShapeDtypeStruct((m, d), F32),
        compiler_params=_cparams(1),
    )(h, g, wg, wu, wd)


def _ffn_ple(h, p, layer, g, wg, wu, wd, gp, wpi, wpg):
    m, d = h.shape
    tm = _row_tile(m)
    params = (g, wg, wu, wd, gp, wpi, wpg)
    return pl.pallas_call(
        _ffn_ple_kernel,
        grid=(m // tm,),
        in_specs=[_row_spec(tm, d), pl.BlockSpec((None, tm, p.shape[2]), lambda i: (layer, i, 0))]
                 + [_layer_spec(w, layer) for w in params],
        out_specs=_row_spec(tm, d),
        out_shape=jax.ShapeDtypeStruct((m, d), F32),
        compiler_params=_cparams(1),
    )(h, p, *params)


def _inproj_kernel(h_ref, g_ref, w_ref, z_ref, xbc_ref, dt_ref):
    u = _rms(h_ref[...], g_ref[...]).astype(BF16)
    lo = 0
    for out in (z_ref, xbc_ref, dt_ref):
        hi = lo + out.shape[1]
        out[...] = _dot(u, w_ref[:, lo:hi])
        lo = hi


def _inproj(h, g, w, widths):
    m, d = h.shape
    tm = _row_tile(m)
    assert sum(widths) == w.shape[1] and all(x % LANES == 0 for x in widths[:-1])
    return pl.pallas_call(
        _inproj_kernel,
        grid=(m // tm,),
        in_specs=[_row_spec(tm, d), _const_spec(g.shape), _const_spec(w.shape)],
        out_specs=[_row_spec(tm, x) for x in widths],
        out_shape=[jax.ShapeDtypeStruct((m, x), F32) for x in widths],
        compiler_params=_cparams(1),
    )(h, g, w)


def _mamba_kernel(chunk, z_ref, xbc_ref, dt_ref, h_ref, cst_ref, sst_ref,
                  convw_ref, convb_ref, dtb_ref, alog_ref, dexp_ref, ng_ref, wout_ref,
                  blt_ref, e2_ref, i2_ref,
                  o_ref, cout_ref, sout_ref,
                  xpad, st, x_s, bc_s, ae_s, de_s, y_s, at_s, yn_s):
    t = pl.program_id(1)
    n_t = pl.num_programs(1)
    tb = xbc_ref.shape[0]
    d_inner = x_s.shape[1]
    gn = SSM_GROUPS * D_STATE
    n_chunks = tb // chunk
    gw = d_inner // SSM_GROUPS
    half = LANES // 2
    pad_rows = half - chunk

    @pl.when(t == 0)
    def _():
        xpad[0:SUBLANES, :] = cst_ref[0]
        for g in range(SSM_GROUPS):
            st[g] = sst_ref[0, g].T

    xpad[SUBLANES:SUBLANES + tb, :] = xbc_ref[...]
    w = convw_ref[...]
    base = SUBLANES - (CONV_W - 1)
    rb = min(CONV_ROWS, chunk)
    row8 = lax.broadcasted_iota(jnp.int32, (SUBLANES, xpad.shape[1]), 0)

    def conv_rows(r0):
        ext = xpad[r0:r0 + rb + SUBLANES, :]
        slabs = [ext[s:s + SUBLANES, :] for s in range(0, rb + SUBLANES, SUBLANES)]
        ys = [convb_ref[...] + sl * w[CONV_W - 1:CONV_W, :] for sl in slabs[1:]]
        for k in range(CONV_W - 1):
            back = CONV_W - 1 - k
            rolled = [pltpu.roll(sl, back, axis=0) for sl in slabs]
            for n in range(len(ys)):
                ys[n] = ys[n] + jnp.where(row8 < back, rolled[n], rolled[n + 1]) * w[k:k + 1, :]
        y = jnp.concatenate(ys, axis=0)
        act = _silu_tanh(y)
        x_s[r0:r0 + rb, :] = act[:, :d_inner]
        bc_s[r0:r0 + rb, :] = act[:, d_inner:]

    dt =_softplus(dt_ref[...] + dtb_ref[...])
    a = dt * (-jnp.exp(alog_ref[...]))
    blt = blt_ref[...]
    acs = sum(_dot(blt, p) for p in _split_bf16(a, 3))
    e2 = e2_ref[...]
    ae_s[...] = _dot(jnp.concatenate(_split_bf16(acs, 2), axis=1), e2)
    de_s[...] = _dot(jnp.concatenate(_split_bf16(dt, 2), axis=1), e2)
    i2 = i2_ref[...]
    at_all = sum(_dot_tn(p, i2) for p in _split_bf16(acs, 3))
    for c in range(n_chunks):
        at_s[c] = at_all[:, c * LANES:(c + 1) * LANES]

    lane = lax.broadcasted_iota(jnp.int32, (chunk, LANES), 1)
    row = lax.broadcasted_iota(jnp.int32, (chunk, LANES), 0)
    low_half = lane < half
    src = lane & (half - 1)
    causal = (src <= row) & (src < chunk)
    dexp = dexp_ref[...]

    def stack_pair(top, bot):
        if pad_rows == 0:
            return jnp.concatenate([top, bot], axis=0)
        zeros = jnp.zeros((pad_rows, LANES), top.dtype)
        return jnp.concatenate([top, zeros, bot, zeros], axis=0)

    def chunk_body(c, carry):
        rows = pl.ds(c * chunk, chunk)
        last_row = pl.ds(c * chunk + chunk - 1, 1)
        at2 = at_s[c]
        for g in range(SSM_GROUPS):
            gl = slice(g * gw, (g + 1) * gw)
            ae = ae_s[rows, gl]
            xc = x_s[rows, gl]
            last = ae_s[last_row, gl]
            xdt = xc * de_s[rows, gl]
            xw = (xdt * jnp.exp(last - ae)).astype(BF16)
            xdt_b = xdt.astype(BF16)
            c_g = bc_s[rows, gn + g * D_STATE:gn + (g + 1) * D_STATE].astype(BF16)
            b_g = bc_s[rows, g * D_STATE:(g + 1) * D_STATE].astype(BF16)
            cb2 = _dot_nt(c_g, stack_pair(b_g, b_g))
            s_g = st[g]
            y_off = _dot(c_g, s_g.astype(BF16)) * jnp.exp(ae)
            pairs = []
            for q in range(gw // LANES):
                pq = g * (gw // LANES) + q
                pl_ = slice(q * LANES, (q + 1) * LANES)
                rowvec = jnp.where(low_half[0:1, :], at2[2 * pq:2 * pq + 1, :], at2[2 * pq + 1:2 * pq + 2, :])
                seg = jnp.exp(jnp.where(causal, ae[:, pl_] - rowvec, -jnp.inf))
                wmat = (cb2 * seg).astype(BF16)
                xp = xdt_b[:, pl_]
                zero = jnp.zeros_like(xp)
                x2 = stack_pair(jnp.where(low_half, xp, zero), jnp.where(low_half, zero, xp))
                pairs.append(_dot(wmat, x2))
            y_diag = jnp.concatenate(pairs, axis=1)
            y_s[rows, gl] = y_diag + y_off + dexp[:, gl] * xc
            st[g] = s_g * jnp.exp(last) + _dot_tn(b_g, xw)
        return carry

    for c in range(n_chunks):
        for r0 in range(c * chunk, (c + 1) * chunk, rb):
            conv_rows(r0)
        chunk_body(c, 0)
        rows = pl.ds(c * chunk, chunk)
        yz = y_s[rows, :] * _silu_tanh(z_ref[rows, :])
        yn_s[rows, :] = _rms(yz, ng_ref[...]).astype(BF16)
    tail = xpad[tb:tb + SUBLANES, :]
    cout_ref[0] = tail
    xpad[0:SUBLANES, :] = tail
    o_ref[...] = h_ref[...] + _dot(yn_s[...], wout_ref[...])

    @pl.when(t == n_t - 1)
    def _():
        for g in range(SSM_GROUPS):
            sout_ref[0, g] = st[g].T


def _mamba(z, xbc, dt, h, conv_state8, ssm_state, convw, convb, dtb, alog, dexp, ng, wout, batch, seq):
    m, d_inner = z.shape
    conv_dim = xbc.shape[1]
    heads = dt.shape[1]
    d = h.shape[1]
    tb = min(MAMBA_TILE, seq)
    chunk = min(CHUNK, seq)
    assert seq % tb == 0 and tb % chunk == 0 and chunk % SUBLANES == 0 and LANES // 2 % chunk == 0
    n_t = seq // tb
    n_chunks = tb // chunk
    gw = d_inner // SSM_GROUPS

    r = np.arange(tb)
    blt = ((r[:, None] // chunk == r[None, :] // chunk) & (r[None, :] <= r[:, None])).astype(np.float32)
    e = (np.arange(d_inner)[None, :] // SSM_HEAD_DIM == np.arange(heads)[:, None]).astype(np.float32)
    e = np.concatenate([e, e], axis=0)
    col = np.arange(n_chunks * LANES)
    i2 = ((col[None, :] // LANES == r[:, None] // chunk)
          & (col[None, :] % (LANES // 2) == r[:, None] % chunk)).astype(np.float32)
    blt, e, i2 = (jnp.asarray(v, BF16) for v in (blt, e, i2))

    row_map = lambda b, t: (b * n_t + t, 0)
    const = lambda v: _const_spec(v.shape)
    return pl.pallas_call(
        functools.partial(_mamba_kernel, chunk),
        grid=(batch, n_t),
        in_specs=[pl.BlockSpec((tb, d_inner), row_map), pl.BlockSpec((tb, conv_dim), row_map),
                  pl.BlockSpec((tb, heads), row_map), pl.BlockSpec((tb, d), row_map),
                  pl.BlockSpec((1, SUBLANES, conv_dim), lambda b, t: (b, 0, 0)),
                  pl.BlockSpec((1, SSM_GROUPS, gw, D_STATE), lambda b, t: (b, 0, 0, 0)),
                  const(convw), const(convb), const(dtb), const(alog), const(dexp), const(ng), const(wout),
                  const(blt), const(e), const(i2)],
        out_specs=[pl.BlockSpec((tb, d), row_map),
                   pl.BlockSpec((1, SUBLANES, conv_dim), lambda b, t: (b, 0, 0)),
                   pl.BlockSpec((1, SSM_GROUPS, gw, D_STATE), lambda b, t: (b, 0, 0, 0))],
        out_shape=[jax.ShapeDtypeStruct((m, d), F32),
                   jax.ShapeDtypeStruct((batch, SUBLANES, conv_dim), F32),
                   jax.ShapeDtypeStruct((batch, SSM_GROUPS, gw, D_STATE), F32)],
        scratch_shapes=[pltpu.VMEM((tb + SUBLANES, conv_dim), F32),
                        pltpu.VMEM((SSM_GROUPS, D_STATE, gw), F32),
                        pltpu.VMEM((tb, d_inner), F32),
                        pltpu.VMEM((tb, conv_dim - d_inner), F32),
                        pltpu.VMEM((tb, d_inner), F32),
                        pltpu.VMEM((tb, d_inner), F32),
                        pltpu.VMEM((tb, d_inner), F32),
                        pltpu.VMEM((n_chunks, heads, LANES), F32),
                        pltpu.VMEM((tb, d_inner), BF16)],
        compiler_params=_cparams(2),
    )(z, xbc, dt, h, conv_state8, ssm_state, convw, convb, dtb, alog, dexp, ng, wout, blt, e, i2)


def _table_spec(tm, seq, width):
    if seq <= tm:
        return pl.BlockSpec((tm, width), lambda i: (0, 0))
    per = seq // tm
    return pl.BlockSpec((tm, width), lambda i: (i % per, 0))


def _tile_table(tab, tm, seq):
    return jnp.tile(tab, (tm // seq, 1)) if seq < tm else tab


def _kvside_body(h_ref, g_ref, wc_ref, wr_ref, wrs_ref, gc_ref, gr_ref, grs_ref, cos_ref, sin_ref):
    u = _rms(h_ref[...], g_ref[...]).astype(BF16)
    ckv = _rms(_dot(u, wc_ref[...]), gc_ref[...])
    r = _dot(u, wr_ref[...])
    rs = _dot(u, wrs_ref[...])
    inv = lax.rsqrt(jnp.mean(r * r, axis=-1, keepdims=True) + EPS)
    return ckv, r * inv * gr_ref[...] * cos_ref[...] + rs * inv * grs_ref[...] * sin_ref[...]


def _kvside_kernel(*refs):
    ckv_ref, kr_ref = refs[-2:]
    ckv_ref[...], kr_ref[...] = _kvside_body(*refs[:-2])


def _kvside(h, g, wc, wr, wrs, gc, gr, grs, cos32, sin32, seq):
    m, d = h.shape
    tm = _row_tile(m)
    assert tm % seq == 0 or seq % tm == 0
    cos32, sin32 = _tile_table(cos32, tm, seq), _tile_table(sin32, tm, seq)
    consts = (g, wc, wr, wrs, gc, gr, grs)
    return pl.pallas_call(
        _kvside_kernel,
        grid=(m // tm,),
        in_specs=[_row_spec(tm, d)] + [_const_spec(v.shape) for v in consts]
                 + [_table_spec(tm, seq, QK_ROPE)] * 2,
        out_specs=[_row_spec(tm, wc.shape[1]), _row_spec(tm, QK_ROPE)],
        out_shape=[jax.ShapeDtypeStruct((m, wc.shape[1]), F32), jax.ShapeDtypeStruct((m, QK_ROPE), F32)],
        compiler_params=_cparams(1),
    )(h, *consts, cos32, sin32)


def _group_mean_sq(x, bd_ref):
    bd = bd_ref[...]
    sq = (x * x).astype(BF16)
    return jnp.concatenate([_dot(sq[:, s:s + MXU_DIM], bd) for s in range(0, x.shape[1], MXU_DIM)], axis=1)


def _kv_expand_kernel(*refs):
    n_side = 10
    ckv_ref, kr_ref, k_ref, vt_ref = refs[-4:]
    ckv, kr = _kvside_body(*refs[:n_side])
    ckv_ref[...] = ckv
    kr_ref[...] = kr
    _expand_body(ckv, kr, *refs[n_side:-4], k_ref, vt_ref)


def _expand_body(ckv, kr, wk_ref, wvt_ref, bd_ref, gk_ref, place_ref, one_ref, k_ref, vt_ref):
    c = ckv.astype(BF16)
    k = _dot(c, wk_ref[...])
    kn = k * lax.rsqrt(_group_mean_sq(k, bd_ref) + EPS) * gk_ref[...]
    kr = _dot(kr.astype(BF16), place_ref[...])
    for hh in range(MLA_HEADS):
        hl = slice(hh * LANES, (hh + 1) * LANES)
        k_ref[:, hl] = (kn[:, hl] + kr).astype(BF16)
    vt = _dot_nt(wvt_ref[...], c)
    one = one_ref[...]
    tk = vt_ref.shape[3]
    for j in range(vt_ref.shape[1]):
        for t in range(0, tk, LANES):
            vt_ref[0, j, :, t:t + LANES] = (vt[:, j * tk + t:j * tk + t + LANES] + one).astype(BF16)


def _kv_expand(h, side, cos32, sin32, expand, seq, tk):
    m, d = h.shape
    tm = min(_row_tile(m), seq)
    assert seq % tm == 0 and tm % tk == 0
    per = seq // tm
    kv_lora = side[1].shape[1]
    vrows = expand[1].shape[0]
    width = MLA_HEADS * LANES
    return pl.pallas_call(
        _kv_expand_kernel,
        grid=(m // tm,),
        in_specs=[_row_spec(tm, d)] + [_const_spec(v.shape) for v in side]
                 + [_table_spec(tm, seq, QK_ROPE)] * 2 + [_const_spec(v.shape) for v in expand],
        out_specs=[_row_spec(tm, kv_lora), _row_spec(tm, QK_ROPE), _row_spec(tm, width),
                   pl.BlockSpec((1, tm // tk, vrows, tk), lambda i: (i // per, i % per, 0, 0))],
        out_shape=[jax.ShapeDtypeStruct((m, kv_lora), F32), jax.ShapeDtypeStruct((m, QK_ROPE), F32),
                   jax.ShapeDtypeStruct((m, width), BF16),
                   jax.ShapeDtypeStruct((m // seq, seq // tk, vrows, tk), BF16)],
        compiler_params=_cparams(1),
    )(h, *side, cos32, sin32, *expand)


def _q_kernel(h_ref, g_ref, wdq_ref, gq_ref, wq_ref, bd_ref, gh_ref, tab_ref, q_ref):
    u = _rms(h_ref[...], g_ref[...]).astype(BF16)
    cq = _rms(_dot(u, wdq_ref[...]), gq_ref[...]).astype(BF16)
    q = _dot(cq, wq_ref[...])
    qn = q * lax.rsqrt(_group_mean_sq(q, bd_ref) + EPS) * gh_ref[...]
    tab = tab_ref[...]
    for hh in range(MLA_HEADS):
        hl = slice(hh * LANES, (hh + 1) * LANES)
        q_ref[:, hl] = (qn[:, hl] * tab).astype(BF16)


def _queries(h, g, wdq, gq, wq, bd, gh, tab, seq):
    m, d = h.shape
    tm = _row_tile(m)
    assert tm % seq == 0 or seq % tm == 0
    tab = _tile_table(tab, tm, seq)
    consts = (g, wdq, gq, wq, bd, gh)
    width = MLA_HEADS * LANES
    return pl.pallas_call(
        _q_kernel,
        grid=(m // tm,),
        in_specs=[_row_spec(tm, d)] + [_const_spec(v.shape) for v in consts]
                 + [_table_spec(tm, seq, LANES)],
        out_specs=_row_spec(tm, width),
        out_shape=jax.ShapeDtypeStruct((m, width), BF16),
        compiler_params=_cparams(1),
    )(h, *consts, tab)


def _q_flat_kernel(h_ref, g_ref, wdq_ref, gq_ref, wn_ref, wr_ref, wrs_ref, bdn_ref, bdr_ref,
                   gn_ref, gr_ref, grs_ref, cos_ref, sin_ref, qn_ref, qr_ref):
    u = _rms(h_ref[...], g_ref[...]).astype(BF16)
    cq = _rms(_dot(u, wdq_ref[...]), gq_ref[...]).astype(BF16)
    qn = _dot(cq, wn_ref[...])
    qn_ref[...] = (qn * lax.rsqrt(_group_mean_sq(qn, bdn_ref) + EPS) * gn_ref[...]).astype(BF16)
    r = _dot(cq, wr_ref[...])
    rs = _dot(cq, wrs_ref[...])
    inv = lax.rsqrt(_group_mean_sq(r, bdr_ref) + EPS)
    qr_ref[...] = (r * inv * gr_ref[...] * cos_ref[...] + rs * inv * grs_ref[...] * sin_ref[...]).astype(BF16)


def _queries_flat(h, g, wdq, gq, wn, wr, wrs, bdn, bdr, gn, gr, grs, cos, sin, seq):
    m, d = h.shape
    tm = _row_tile(m)
    assert tm % seq == 0 or seq % tm == 0
    cos, sin = _tile_table(cos, tm, seq), _tile_table(sin, tm, seq)
    consts = (g, wdq, gq, wn, wr, wrs, bdn, bdr, gn, gr, grs)
    widths = (wn.shape[1], wr.shape[1])
    return pl.pallas_call(
        _q_flat_kernel,
        grid=(m // tm,),
        in_specs=[_row_spec(tm, d)] + [_const_spec(v.shape) for v in consts]
                 + [_table_spec(tm, seq, widths[1])] * 2,
        out_specs=[_row_spec(tm, w) for w in widths],
        out_shape=[jax.ShapeDtypeStruct((m, w), BF16) for w in widths],
        compiler_params=_cparams(1),
    )(h, *consts, cos, sin)


def _cache_attn_kernel(kb, qn_ref, qr_ref, ckv_ref, kr_ref, ckvn_ref, krn_ref, h_ref,
                       wk_ref, wv_ref, bd_ref, gk_ref, wo_ref, o_ref, s_s, v_s, acc_s):
    tq = qn_ref.shape[1]
    past = ckv_ref.shape[1]
    cols = MLA_HEADS * tq
    chunk_shift = CHUNK.bit_length() - 1
    assert CHUNK == 1 << chunk_shift and past % kb == 0

    qn = jnp.concatenate([qn_ref[0]] * MLA_HEADS, axis=0)
    tq_shift, nope_shift = tq.bit_length() - 1, QK_NOPE.bit_length() - 1
    assert tq == 1 << tq_shift and QK_NOPE == 1 << nope_shift
    r_head = lax.broadcasted_iota(jnp.int32, qn.shape, 0) >> tq_shift
    c_head = lax.broadcasted_iota(jnp.int32, qn.shape, 1) >> nope_shift
    q_bd = jnp.where(r_head == c_head, qn, jnp.zeros_like(qn))
    qr_all = qr_ref[0]
    q_rope = jnp.concatenate([qr_all[:, hh * QK_ROPE:(hh + 1) * QK_ROPE] for hh in range(MLA_HEADS)], axis=0)

    def expand(c, kr, rows):
        c = c.astype(BF16)
        k = _dot(c, wk_ref[...])
        kn = (k * lax.rsqrt(_group_mean_sq(k, bd_ref) + EPS) * gk_ref[...]).astype(BF16)
        s = _dot_nt(kn, q_bd) + _dot_nt(kr.astype(BF16), q_rope)
        v = _dot(c, wv_ref[...]).astype(BF16)
        v_s[rows, :] = jnp.concatenate([v, jnp.ones((v.shape[0], LANES), BF16)], axis=1)
        return s

    def expand_block(j, m_run):
        rows = pl.ds(pl.multiple_of(j * kb, kb), kb)
        s = expand(ckv_ref[0, rows, :], kr_ref[0, rows, :], rows)
        s_s[rows, :] = s
        return jnp.maximum(m_run, jnp.max(s, axis=0, keepdims=True))

    m_run = lax.fori_loop(0, past // kb, expand_block, jnp.full((1, cols), -jnp.inf, F32))
    new_rows = pl.ds(past, tq)
    s_new = expand(ckvn_ref[0], krn_ref[0], new_rows)
    k_chunk = (past + lax.broadcasted_iota(jnp.int32, (tq, cols), 0)) >> chunk_shift
    q_chunk = (past + (lax.broadcasted_iota(jnp.int32, (tq, cols), 1) & (tq - 1))) >> chunk_shift
    s_new = jnp.where(k_chunk <= q_chunk, s_new, -jnp.inf)
    m_all = jnp.maximum(m_run, jnp.max(s_new, axis=0, keepdims=True))

    acc_s[...] = _dot_tn(jnp.exp2(s_new - m_all).astype(BF16), v_s[new_rows, :])

    def pv_block(j, carry):
        rows = pl.ds(pl.multiple_of(j * kb, kb), kb)
        p = jnp.exp2(s_s[rows, :] - m_all).astype(BF16)
        acc_s[...] += _dot_tn(p, v_s[rows, :])
        return carry

    lax.fori_loop(0, past // kb, pv_block, 0)

    width = MLA_HEADS * V_HEAD
    low_half = lax.broadcasted_iota(jnp.int32, (tq, LANES), 1) < V_HEAD
    tiles = []
    for c in range(width // LANES):
        ev = acc_s[2 * c * tq:(2 * c + 1) * tq, :]
        od = acc_s[(2 * c + 1) * tq:(2 * c + 2) * tq, :]
        cl = slice(c * LANES, (c + 1) * LANES)
        tiles.append(jnp.where(low_half, ev[:, cl] / ev[:, width:], od[:, cl] / od[:, width:]))
    o = jnp.concatenate(tiles, axis=1).astype(BF16)
    o_ref[0] = h_ref[0] + _dot(o, wo_ref[...])


def _cache_attention(qn, qr, ckv, kr, ckv_new, kr_new, h, wk, wv, bd, gk, wo):
    batch, tq, _ = qn.shape
    past = ckv.shape[1]
    d = h.shape[2]
    kb = min(ROW_TILE, past)
    assert past % kb == 0 and tq % 16 == 0 and (MLA_HEADS * tq) % LANES == 0
    per = lambda a: pl.BlockSpec((1,) + a.shape[1:], lambda b: (b, 0, 0))
    consts = (wk, wv, bd, gk, wo)
    return pl.pallas_call(
        functools.partial(_cache_attn_kernel, kb),
        grid=(batch,),
        in_specs=[per(qn), per(qr), per(ckv), per(kr), per(ckv_new), per(kr_new), per(h)]
                 + [_const_spec(v.shape) for v in consts],
        out_specs=per(h),
        out_shape=jax.ShapeDtypeStruct(h.shape, F32),
        scratch_shapes=[pltpu.VMEM((past + tq, MLA_HEADS * tq), F32),
                        pltpu.VMEM((past + tq, MLA_HEADS * V_HEAD + LANES), BF16),
                        pltpu.VMEM((MLA_HEADS * tq, MLA_HEADS * V_HEAD + LANES), F32)],
        compiler_params=_cparams(1),
    )(qn, qr, ckv, kr, ckv_new, kr_new, h, *consts)


def _self_attn_kernel(q_ref, k_ref, vt_ref, h_ref, wo_ref, o_ref, m_s, acc_s, ot_s, st_s):
    i = pl.program_id(1)
    tq = q_ref.shape[1]
    chunk_shift = CHUNK.bit_length() - 1
    assert CHUNK == 1 << chunk_shift
    visible = ((lax.broadcasted_iota(jnp.int32, (tq, tq), 0) >> chunk_shift)
               <= (lax.broadcasted_iota(jnp.int32, (tq, tq), 1) >> chunk_shift))

    m_s[...] = jnp.full(m_s.shape, -jnp.inf, F32)
    acc_s[...] = jnp.zeros(acc_s.shape, F32)

    def step(j, masked):
        k0 = pl.multiple_of(j * tq, tq)
        for hh in range(MLA_HEADS):
            hl = slice(hh * LANES, (hh + 1) * LANES)
            st_s[hh] = _dot_nt(k_ref[0, pl.ds(k0, tq), hl], q_ref[0, :, hl])
        for hh in range(MLA_HEADS):
            st = st_s[hh]
            if masked:
                st = jnp.where(visible, st, -jnp.inf)
            m_old = m_s[hh]
            m_new = jnp.maximum(m_old, jnp.max(st, axis=0, keepdims=True))
            alpha = jnp.exp2(m_old - m_new)
            pt = jnp.exp2(st - m_new).astype(BF16)
            vt = vt_ref[0, j, hh * VT_ROWS:(hh + 1) * VT_ROWS, :]
            acc_s[hh] = alpha * acc_s[hh] + _dot(vt, pt)
            m_s[hh] = m_new

    def full_step(j, carry):
        step(j, False)
        return carry

    lax.fori_loop(0, i, full_step, 0)
    step(i, True)

    for hh in range(MLA_HEADS):
        acc = acc_s[hh]
        ot_s[hh * VT_ROWS:(hh + 1) * VT_ROWS, :] = (acc / acc[V_HEAD:V_HEAD + 1, :]).astype(BF16)
    o_ref[0] = h_ref[0] + _dot_tn(ot_s[...], wo_ref[...])


def _self_attention(q, k, vt, h, wo, batch):
    width = q.shape[2]
    seq = q.shape[1]
    d = h.shape[2]
    tq = vt.shape[3]
    vrows = vt.shape[2]
    assert seq % tq == 0 and tq % CHUNK == 0 and vt.shape[1] * tq == seq and vrows == MLA_HEADS * VT_ROWS
    return pl.pallas_call(
        _self_attn_kernel,
        grid=(batch, seq // tq),
        in_specs=[pl.BlockSpec((1, tq, width), lambda b, i: (b, i, 0)),
                  pl.BlockSpec((1, seq, width), lambda b, i: (b, 0, 0)),
                  pl.BlockSpec((1, seq // tq, vrows, tq), lambda b, i: (b, 0, 0, 0)),
                  pl.BlockSpec((1, tq, d), lambda b, i: (b, i, 0)),
                  _const_spec(wo.shape)],
        out_specs=pl.BlockSpec((1, tq, d), lambda b, i: (b, i, 0)),
        out_shape=jax.ShapeDtypeStruct((batch, seq, d), F32),
        scratch_shapes=[pltpu.VMEM((MLA_HEADS, 1, tq), F32), pltpu.VMEM((MLA_HEADS, VT_ROWS, tq), F32),
                        pltpu.VMEM((vrows, tq), BF16), pltpu.VMEM((MLA_HEADS, tq, tq), F32)],
        compiler_params=_cparams(2),
    )(q, k, vt, h, wo)


def _head_pad_cols(w, per_head, start, count, offset=0):
    k = w.shape[0]
    w3 = w.reshape(k, MLA_HEADS, per_head)[:, :, start:start + count]
    out = jnp.zeros((k, MLA_HEADS, LANES), w.dtype)
    out = out.at[:, :, offset:offset + count].set(w3)
    return out.reshape(k, MLA_HEADS * LANES)


def _head_rows(w3):
    h, r, n = w3.shape
    return jnp.zeros((h, VT_ROWS, n), w3.dtype).at[:, :r].set(w3).reshape(h * VT_ROWS, n)


def _head_pad_vec(pieces):
    blk = jnp.zeros((LANES,), F32)
    for off, vec in pieces:
        blk = blk.at[off:off + vec.shape[0]].set(vec.astype(F32))
    return jnp.tile(blk, MLA_HEADS)[None, :]


def _block_diag_mean(groups):
    bd = np.zeros((MXU_DIM, MXU_DIM), np.float32)
    for tile in range(MXU_DIM // LANES):
        for off, size, *spread in groups:
            lo = tile * LANES + off
            bd[lo:lo + size, lo:lo + (spread[0] if spread else size)] = 1.0 / size
    return jnp.asarray(bd, BF16)


def _rope_tables(pos):
    half = QK_ROPE // 2
    inv = ROPE_BASE ** (-jnp.arange(half, dtype=F32) / half)
    ang = pos.astype(F32)[:, None] * inv[None, :]
    return jnp.cos(ang), jnp.sin(ang)


def kernel(x_prompt, x_sample, p_prompt, p_sample, state_conv, state_ssm, cache_kv_latent, cache_k_rope, ln_ffn1, w1_gate, w1_up, w1_down, ln_mix, ln_ffn2, w2_gate, w2_up, w2_down, ln_ple, w_ple_in, w_ple_gate, a_in_proj, a_conv_w, a_conv_b, a_dt_bias, a_A_log, a_D, a_norm, a_out_proj, ln_kv, w_kv_a, kv_norm, kr_norm, w_kv_b, k_norm, b_w_dq, b_q_norm, b_w_q_b, b_qn_nope, b_qn_rope, b_w_o):
    depth = ln_ffn1.shape[0]
    n_a = a_in_proj.shape[0]
    assert depth == 2 and n_a == 1 and b_w_dq.shape[0] == 1
    d_model = x_prompt.shape[-1]
    heads = a_dt_bias.shape[-1]
    d_inner = heads * SSM_HEAD_DIM
    conv_dim = a_conv_w.shape[-1]
    kv_lora = kv_norm.shape[0]
    half = QK_ROPE // 2
    bf = lambda w: w.astype(BF16)
    row = lambda v: v.astype(F32)[None, :]

    rows = lambda v: v.astype(F32)[:, None, :]
    ffn1 = (rows(ln_ffn1), bf(w1_gate), bf(w1_up), bf(w1_down))
    ffn2 = (rows(ln_ffn2), bf(w2_gate), bf(w2_up), bf(w2_down), rows(ln_ple), bf(w_ple_in), bf(w_ple_gate))
    mamba_w = dict(
        g=row(ln_mix[0]), w_in=bf(a_in_proj[0]), convw=a_conv_w[0], convb=row(a_conv_b[0]),
        dtb=row(a_dt_bias[0]), alog=row(a_A_log[0]), dexp=row(jnp.repeat(a_D[0], SSM_HEAD_DIM)),
        ng=row(a_norm[0]), wout=bf(a_out_proj[0]))

    swap = np.concatenate([np.arange(half, QK_ROPE), np.arange(half)])
    w_r = w_kv_a[:, kv_lora:]
    kv_w = dict(g=row(ln_kv), wc=bf(w_kv_a[:, :kv_lora]), wr=bf(w_r), wrs=bf(w_r[:, swap]),
                gc=row(kv_norm), gr=row(kr_norm), grs=row(kr_norm[swap]))
    per_kv = QK_NOPE + V_HEAD
    ex_w = dict(wk=bf(_head_pad_cols(w_kv_b, per_kv, 0, QK_NOPE)),
                wvt=bf(_head_rows(w_kv_b.reshape(kv_lora, MLA_HEADS, per_kv)[:, :, QK_NOPE:].transpose(1, 2, 0))),
                bd=_block_diag_mean([(0, QK_NOPE)]),
                gk=_head_pad_vec([(0, k_norm)]),
                place=jnp.asarray(np.eye(QK_ROPE, LANES, k=QK_NOPE) + np.eye(QK_ROPE, LANES, k=QK_NOPE + QK_ROPE), BF16),
                one=jnp.asarray(np.tile((np.arange(VT_ROWS) == V_HEAD).astype(np.float32)[:, None],
                                        (MLA_HEADS, LANES))))
    per_q = QK_NOPE + QK_ROPE
    wqb = b_w_q_b[0]
    wq = (_head_pad_cols(wqb, per_q, 0, per_q)
          + _head_pad_cols(wqb, per_q, QK_NOPE + half, half, offset=per_q)
          + _head_pad_cols(wqb, per_q, QK_NOPE, half, offset=per_q + half))
    gq_rope = b_qn_rope[0]
    q_w = dict(g=row(ln_mix[1]), wdq=bf(b_w_dq[0]), gq=row(b_q_norm[0]), wq=bf(wq),
               bd=_block_diag_mean([(0, QK_NOPE), (QK_NOPE, QK_ROPE, 2 * QK_ROPE)]),
               gh=_head_pad_vec([(0, b_qn_nope[0]), (QK_NOPE, gq_rope), (per_q, gq_rope[swap])]) * EXP2_SCALE)
    wo = bf(_head_rows(b_w_o[0].reshape(MLA_HEADS, V_HEAD, d_model)))

    wkv3 = w_kv_b.reshape(kv_lora, MLA_HEADS, per_kv)
    wq3 = wqb.reshape(wqb.shape[0], MLA_HEADS, per_q)
    flat = lambda w3: bf(w3.reshape(w3.shape[0], -1))
    per_head = lambda v: jnp.tile(v.astype(F32), MLA_HEADS)[None, :]
    flat_w = dict(
        wk=flat(wkv3[:, :, :QK_NOPE]), wv=flat(wkv3[:, :, QK_NOPE:]), gk=per_head(k_norm),
        wn=flat(wq3[:, :, :QK_NOPE]), wr=flat(wq3[:, :, QK_NOPE:]), wrs=flat(wq3[:, :, QK_NOPE:][:, :, swap]),
        bdn=_block_diag_mean([(o, QK_NOPE) for o in range(0, LANES, QK_NOPE)]),
        bdr=_block_diag_mean([(o, QK_ROPE) for o in range(0, LANES, QK_ROPE)]),
        gn=per_head(b_qn_nope[0]) * EXP2_SCALE, gr=per_head(gq_rope) * EXP2_SCALE,
        grs=per_head(gq_rope[swap]) * EXP2_SCALE, wo=bf(b_w_o[0]))

    def rope_tabs(pos):
        cos, sin = _rope_tables(pos)
        seq = pos.shape[0]
        cos32 = jnp.concatenate([cos, cos], axis=1)
        sin32 = jnp.concatenate([-sin, sin], axis=1)
        tab128 = jnp.concatenate([jnp.ones((seq, QK_NOPE), F32), cos32, sin32], axis=1)
        return cos32, sin32, tab128

    def run_group(x, p, pos0, conv_in, ssm_in, past_ckv, past_kr):
        batch, seq, _ = x.shape
        m = batch * seq
        pos = pos0 + jnp.arange(seq, dtype=jnp.int32)
        cos32, sin32, tab128 = rope_tabs(pos)
        h = x.reshape(m, d_model)
        p2 = p.reshape(depth, m, p.shape[-1])

        h = _ffn(h, 0, *ffn1)
        mw = mamba_w
        z, xbc, dt = _inproj(h, mw["g"], mw["w_in"], (d_inner, conv_dim, heads))
        conv8 = jnp.pad(conv_in, ((0, 0), (SUBLANES - (CONV_W - 1), 0), (0, 0)))
        ssm4 = ssm_in.reshape(batch, SSM_GROUPS, d_inner // SSM_GROUPS, D_STATE)
        h, conv_out, ssm_out = _mamba(z, xbc, dt, h, conv8, ssm4, mw["convw"], mw["convb"], mw["dtb"],
                                      mw["alog"], mw["dexp"], mw["ng"], mw["wout"], batch, seq)
        conv_out = conv_out[None, :, SUBLANES - (CONV_W - 1):, :]
        ssm_out = ssm_out.reshape(1, batch, heads, SSM_HEAD_DIM, D_STATE)
        h = _ffn_ple(h, p2, 0, *ffn2)

        kw = kv_w
        side = (kw["g"], kw["wc"], kw["wr"], kw["wrs"], kw["gc"], kw["gr"], kw["grs"])
        if past_ckv is None:
            ew = ex_w
            ckv_new, kr_new, k_all, vt_all = _kv_expand(
                h, side, cos32, sin32, (ew["wk"], ew["wvt"], ew["bd"], ew["gk"], ew["place"], ew["one"]),
                seq, min(ATTN_TILE, seq))
        else:
            ckv_new, kr_new = _kvside(h, *side, cos32, sin32, seq)

        h = _ffn(h, 1, *ffn1)
        qw = q_w
        if past_ckv is None:
            q = _queries(h, qw["g"], qw["wdq"], qw["gq"], qw["wq"], qw["bd"], qw["gh"], tab128, seq)
            width = MLA_HEADS * LANES
            h = _self_attention(q.reshape(batch, seq, width), k_all.reshape(batch, seq, width), vt_all,
                                h.reshape(batch, seq, d_model), wo, batch)
        else:
            fw = flat_w
            cos_f, sin_f = jnp.tile(cos32, (1, MLA_HEADS)), jnp.tile(sin32, (1, MLA_HEADS))
            qn, qr = _queries_flat(h, qw["g"], qw["wdq"], qw["gq"], fw["wn"], fw["wr"], fw["wrs"], fw["bdn"],
                                   fw["bdr"], fw["gn"], fw["gr"], fw["grs"], cos_f, sin_f, seq)
            h = _cache_attention(qn.reshape(batch, seq, -1), qr.reshape(batch, seq, -1), past_ckv, past_kr,
                                 ckv_new.reshape(batch, seq, kv_lora), kr_new.reshape(batch, seq, QK_ROPE),
                                 h.reshape(batch, seq, d_model), fw["wk"], fw["wv"], fw["bdn"], fw["gk"], fw["wo"])
        h = _ffn_ple(h.reshape(m, d_model), p2, 1, *ffn2)
        return (h.reshape(batch, seq, d_model), conv_out, ssm_out,
                ckv_new.reshape(batch, seq, kv_lora), kr_new.reshape(batch, seq, QK_ROPE))

    b_p = x_prompt.shape[0]
    conv0 = jnp.zeros((b_p, CONV_W - 1, conv_dim), F32)
    ssm0 = jnp.zeros((b_p, heads, SSM_HEAD_DIM, D_STATE), F32)
    y_p, conv_p, ssm_p, kv_p, kr_p = run_group(x_prompt, p_prompt, 0, conv0, ssm0, None, None)
    y_s, conv_s, ssm_s, kv_s, kr_s = run_group(x_sample, p_sample, cache_kv_latent.shape[1],
                                               state_conv[0], state_ssm[0], cache_kv_latent, cache_k_rope)
    return (y_p, y_s, conv_p, ssm_p, kv_p, kr_p, conv_s, ssm_s, kv_s, kr_s)
```

```python
import functools

import numpy as np
import jax
import jax.numpy as jnp
from jax import lax
from jax.experimental import pallas as pl
from jax.experimental.pallas import tpu as pltpu

F32 = jnp.float32
BF16 = jnp.bfloat16

EPS = 1e-6
CHUNK = 64
SSM_HEAD_DIM = 64
SSM_GROUPS = 4
D_STATE = 128
CONV_W = 4
MLA_HEADS = 16
QK_NOPE = 64
QK_ROPE = 32
V_HEAD = 64
ROPE_BASE = 10000.0
ATTN_SCALE = (QK_NOPE + QK_ROPE) ** -0.5
EXP2_SCALE = ATTN_SCALE * float(np.log2(np.e))

LANES = 128
VT_ROWS = V_HEAD + 16
SUBLANES = 8
MXU_DIM = 256
VMEM_LIMIT_BYTES = 56 * 1024 * 1024
ROW_TILE = 512
ATTN_TILE = 256
MAMBA_TILE = 256
CONV_ROWS = 32


def _cparams(n_grid):
    return pltpu.CompilerParams(dimension_semantics=("arbitrary",) * n_grid,
                                vmem_limit_bytes=VMEM_LIMIT_BYTES)


def _const_spec(shape):
    nd = len(shape)
    return pl.BlockSpec(shape, lambda *_: (0,) * nd, pipeline_mode=pl.Buffered(1))


def _row_spec(tm, width):
    return pl.BlockSpec((tm, width), lambda i: (i, 0))


def _row_tile(m):
    tm = min(ROW_TILE, m)
    assert m % tm == 0 and tm % SUBLANES == 0
    return tm


def _dot(a, b):
    return jnp.dot(a, b, preferred_element_type=F32)


def _dot_nt(a, b):
    return lax.dot_general(a, b, (((1,), (1,)), ((), ())), preferred_element_type=F32)


def _dot_tn(a, b):
    return lax.dot_general(a, b, (((0,), (0,)), ((), ())), preferred_element_type=F32)


def _rms(x, g):
    ms = jnp.mean(x * x, axis=-1, keepdims=True)
    return x * lax.rsqrt(ms + EPS) * g


def _silu(x):
    return x * jax.nn.sigmoid(x)


def _silu_tanh(x):
    half = 0.5 * x
    return half + half * jnp.tanh(half)


def _softplus(x):
    return jnp.maximum(x, 0.0) + jnp.log1p(jnp.exp(-jnp.abs(x)))


def _split_bf16(v, n):
    pieces = []
    r = v
    for _ in range(n):
        p = r.astype(BF16)
        pieces.append(p)
        r = r - p.astype(F32)
    return pieces


def _ffn_body(h, g, wg_ref, wu_ref, wd_ref):
    u = _rms(h, g).astype(BF16)
    a = _dot(u, wg_ref[...])
    b = _dot(u, wu_ref[...])
    act = (_silu(a) * b).astype(BF16)
    return h + 0.5 * _dot(act, wd_ref[...])


def _ffn_kernel(h_ref, g_ref, wg_ref, wu_ref, wd_ref, o_ref):
    o_ref[...] = _ffn_body(h_ref[...], g_ref[...], wg_ref, wu_ref, wd_ref)


def _ffn_ple_kernel(h_ref, p_ref, g_ref, wg_ref, wu_ref, wd_ref, gp_ref, wpi_ref, wpg_ref, o_ref):
    h2 = _ffn_body(h_ref[...], g_ref[...], wg_ref, wu_ref, wd_ref)
    gate = jax.nn.sigmoid(_dot(_rms(h2, gp_ref[...]).astype(BF16), wpg_ref[...]))
    o_ref[...] = h2 + _dot(p_ref[...].astype(BF16), wpi_ref[...]) * gate


def _layer_spec(w, layer):
    return pl.BlockSpec((None,) + w.shape[1:], lambda *_: (layer, 0, 0), pipeline_mode=pl.Buffered(1))


def _ffn(h, layer, g, wg, wu, wd):
    m, d = h.shape
    tm = _row_tile(m)
    return pl.pallas_call(
        _ffn_kernel,
        grid=(m // tm,),
        in_specs=[_row_spec(tm, d)] + [_layer_spec(w, layer) for w in (g, wg, wu, wd)],
        out_specs=_row_spec(tm, d),
        out_shape=jax.ShapeDtypeStruct((m, d), F32),
        compiler_params=_cparams(1),
    )(h, g, wg, wu, wd)


def _ffn_ple(h, p, layer, g, wg, wu, wd, gp, wpi, wpg):
    m, d = h.shape
    tm = _row_tile(m)
    params = (g, wg, wu, wd, gp, wpi, wpg)
    return pl.pallas_call(
        _ffn_ple_kernel,
        grid=(m // tm,),
        in_specs=[_row_spec(tm, d), pl.BlockSpec((None, tm, p.shape[2]), lambda i: (layer, i, 0))]
                 + [_layer_spec(w, layer) for w in params],
        out_specs=_row_spec(tm, d),
        out_shape=jax.ShapeDtypeStruct((m, d), F32),
        compiler_params=_cparams(1),
    )(h, p, *params)


def _inproj_kernel(h_ref, g_ref, w_ref, z_ref, xbc_ref, dt_ref):
    u = _rms(h_ref[...], g_ref[...]).astype(BF16)
    lo = 0
    for out in (z_ref, xbc_ref, dt_ref):
        hi = lo + out.shape[1]
        out[...] = _dot(u, w_ref[:, lo:hi])
        lo = hi


def _inproj(h, g, w, widths):
    m, d = h.shape
    tm = _row_tile(m)
    assert sum(widths) == w.shape[1] and all(x % LANES == 0 for x in widths[:-1])
    return pl.pallas_call(
        _inproj_kernel,
        grid=(m // tm,),
        in_specs=[_row_spec(tm, d), _const_spec(g.shape), _const_spec(w.shape)],
        out_specs=[_row_spec(tm, x) for x in widths],
        out_shape=[jax.ShapeDtypeStruct((m, x), F32) for x in widths],
        compiler_params=_cparams(1),
    )(h, g, w)


def _mamba_kernel(chunk, z_ref, xbc_ref, dt_ref, h_ref, cst_ref, sst_ref,
                  convw_ref, convb_ref, dtb_ref, alog_ref, dexp_ref, ng_ref, wout_ref,
                  blt_ref, e3_ref, i2_ref,
                  o_ref, cout_ref, sout_ref,
                  xpad, st, x_s, bc_s, ae_s, de_s, y_s, at_s, yn_s):
    t = pl.program_id(1)
    n_t = pl.num_programs(1)
    tb = xbc_ref.shape[0]
    d_inner = x_s.shape[1]
    gn = SSM_GROUPS * D_STATE
    n_chunks = tb // chunk
    gw = d_inner // SSM_GROUPS
    half = LANES // 2
    pad_rows = half - chunk

    @pl.when(t == 0)
    def _():
        xpad[0:SUBLANES, :] = cst_ref[0]
        for g in range(SSM_GROUPS):
            st[g] = sst_ref[0, g].T

    xpad[SUBLANES:SUBLANES + tb, :] = xbc_ref[...]
    w = convw_ref[...]
    base = SUBLANES - (CONV_W - 1)
    rb = min(CONV_ROWS, chunk)
    row8 = lax.broadcasted_iota(jnp.int32, (SUBLANES, xpad.shape[1]), 0)

    def conv_rows(r0):
        ext = xpad[r0:r0 + rb + SUBLANES, :]
        slabs = [ext[s:s + SUBLANES, :] for s in range(0, rb + SUBLANES, SUBLANES)]
        ys = [convb_ref[...] + sl * w[CONV_W - 1:CONV_W, :] for sl in slabs[1:]]
        for k in range(CONV_W - 1):
            back = CONV_W - 1 - k
            rolled = [pltpu.roll(sl, back, axis=0) for sl in slabs]
            for n in range(len(ys)):
                ys[n] = ys[n] + jnp.where(row8 < back, rolled[n], rolled[n + 1]) * w[k:k + 1, :]
        y = jnp.concatenate(ys, axis=0)
        act = _silu_tanh(y)
        x_s[r0:r0 + rb, :] = act[:, :d_inner]
        bc_s[r0:r0 + rb, :] = act[:, d_inner:]

    dt = _softplus(dt_ref[...] + dtb_ref[...])
    a = dt * (-jnp.exp(alog_ref[...]))
    blt = blt_ref[...]
    acs = sum(_dot(blt, p) for p in _split_bf16(a, 3))
    e3 = e3_ref[...]
    ae_s[...] = _dot(jnp.concatenate(_split_bf16(acs, 3), axis=1), e3)
    de_s[...] = _dot(jnp.concatenate(_split_bf16(dt, 3), axis=1), e3)
    i2 = i2_ref[...]
    at_all = sum(_dot_tn(p, i2) for p in _split_bf16(acs, 3))
    for c in range(n_chunks):
        at_s[c] = at_all[:, c * LANES:(c + 1) * LANES]

    lane = lax.broadcasted_iota(jnp.int32, (chunk, LANES), 1)
    row = lax.broadcasted_iota(jnp.int32, (chunk, LANES), 0)
    low_half = lane < half
    src = lane & (half - 1)
    causal = (src <= row) & (src < chunk)
    dexp = dexp_ref[...]

    def stack_pair(top, bot):
        if pad_rows == 0:
            return jnp.concatenate([top, bot], axis=0)
        zeros = jnp.zeros((pad_rows, LANES), top.dtype)
        return jnp.concatenate([top, zeros, bot, zeros], axis=0)

    def chunk_body(c, carry):
        rows = pl.ds(c * chunk, chunk)
        last_row = pl.ds(c * chunk + chunk - 1, 1)
        at2 = at_s[c]
        for g in range(SSM_GROUPS):
            gl = slice(g * gw, (g + 1) * gw)
            ae = ae_s[rows, gl]
            xc = x_s[rows, gl]
            last = ae_s[last_row, gl]
            xdt = xc * de_s[rows, gl]
            xw = (xdt * jnp.exp(last - ae)).astype(BF16)
            xdt_b = xdt.astype(BF16)
            c_g = bc_s[rows, gn + g * D_STATE:gn + (g + 1) * D_STATE].astype(BF16)
            b_g = bc_s[rows, g * D_STATE:(g + 1) * D_STATE].astype(BF16)
            cb2 = _dot_nt(c_g, stack_pair(b_g, b_g))
            s_g = st[g]
            y_off = _dot(c_g, s_g.astype(BF16)) * jnp.exp(ae)
            pairs = []
            for q in range(gw // LANES):
                pq = g * (gw // LANES) + q
                pl_ = slice(q * LANES, (q + 1) * LANES)
                rowvec = jnp.where(low_half[0:1, :], at2[2 * pq:2 * pq + 1, :], at2[2 * pq + 1:2 * pq + 2, :])
                seg = jnp.exp(jnp.where(causal, ae[:, pl_] - rowvec, -jnp.inf))
                wmat = (cb2 * seg).astype(BF16)
                xp = xdt_b[:, pl_]
                zero = jnp.zeros_like(xp)
                x2 = stack_pair(jnp.where(low_half, xp, zero), jnp.where(low_half, zero, xp))
                pairs.append(_dot(wmat, x2))
            y_diag = jnp.concatenate(pairs, axis=1)
            y_s[rows, gl] = y_diag + y_off + dexp[:, gl] * xc
            st[g] = s_g * jnp.exp(last) + _dot_tn(b_g, xw)
        return carry

    for c in range(n_chunks):
        for r0 in range(c * chunk, (c + 1) * chunk, rb):
            conv_rows(r0)
        chunk_body(c, 0)
        rows = pl.ds(c * chunk, chunk)
        yz = y_s[rows, :] * _silu_tanh(z_ref[rows, :])
        yn_s[rows, :] = _rms(yz, ng_ref[...]).astype(BF16)
    tail = xpad[tb:tb + SUBLANES, :]
    cout_ref[0] = tail
    xpad[0:SUBLANES, :] = tail
    o_ref[...] = h_ref[...] + _dot(yn_s[...], wout_ref[...])

    @pl.when(t == n_t - 1)
    def _():
        for g in range(SSM_GROUPS):
            sout_ref[0, g] = st[g].T


def _mamba(z, xbc, dt, h, conv_state8, ssm_state, convw, convb, dtb, alog, dexp, ng, wout, batch, seq):
    m, d_inner = z.shape
    conv_dim = xbc.shape[1]
    heads = dt.shape[1]
    d = h.shape[1]
    tb = min(MAMBA_TILE, seq)
    chunk = min(CHUNK, seq)
    assert seq % tb == 0 and tb % chunk == 0 and chunk % SUBLANES == 0 and LANES // 2 % chunk == 0
    n_t = seq // tb
    n_chunks = tb // chunk
    gw = d_inner // SSM_GROUPS

    r = np.arange(tb)
    blt = ((r[:, None] // chunk == r[None, :] // chunk) & (r[None, :] <= r[:, None])).astype(np.float32)
    e = (np.arange(d_inner)[None, :] // SSM_HEAD_DIM == np.arange(heads)[:, None]).astype(np.float32)
    e = np.concatenate([e, e, e], axis=0)
    col = np.arange(n_chunks * LANES)
    i2 = ((col[None, :] // LANES == r[:, None] // chunk)
          & (col[None, :] % (LANES // 2) == r[:, None] % chunk)).astype(np.float32)
    blt, e, i2 = (jnp.asarray(v, BF16) for v in (blt, e, i2))

    row_map = lambda b, t: (b * n_t + t, 0)
    const = lambda v: _const_spec(v.shape)
    return pl.pallas_call(
        functools.partial(_mamba_kernel, chunk),
        grid=(batch, n_t),
        in_specs=[pl.BlockSpec((tb, d_inner), row_map), pl.BlockSpec((tb, conv_dim), row_map),
                  pl.BlockSpec((tb, heads), row_map), pl.BlockSpec((tb, d), row_map),
                  pl.BlockSpec((1, SUBLANES, conv_dim), lambda b, t: (b, 0, 0)),
                  pl.BlockSpec((1, SSM_GROUPS, gw, D_STATE), lambda b, t: (b, 0, 0, 0)),
                  const(convw), const(convb), const(dtb), const(alog), const(dexp), const(ng), const(wout),
                  const(blt), const(e), const(i2)],
        out_specs=[pl.BlockSpec((tb, d), row_map),
                   pl.BlockSpec((1, SUBLANES, conv_dim), lambda b, t: (b, 0, 0)),
                   pl.BlockSpec((1, SSM_GROUPS, gw, D_STATE), lambda b, t: (b, 0, 0, 0))],
        out_shape=[jax.ShapeDtypeStruct((m, d), F32),
                   jax.ShapeDtypeStruct((batch, SUBLANES, conv_dim), F32),
                   jax.ShapeDtypeStruct((batch, SSM_GROUPS, gw, D_STATE), F32)],
        scratch_shapes=[pltpu.VMEM((tb + SUBLANES, conv_dim), F32),
                        pltpu.VMEM((SSM_GROUPS, D_STATE, gw), F32),
                        pltpu.VMEM((tb, d_inner), F32),
                        pltpu.VMEM((tb, conv_dim - d_inner), F32),
                        pltpu.VMEM((tb, d_inner), F32),
                        pltpu.VMEM((tb, d_inner), F32),
                        pltpu.VMEM((tb, d_inner), F32),
                        pltpu.VMEM((n_chunks, heads, LANES), F32),
                        pltpu.VMEM((tb, d_inner), BF16)],
        compiler_params=_cparams(2),
    )(z, xbc, dt, h, conv_state8, ssm_state, convw, convb, dtb, alog, dexp, ng, wout, blt, e, i2)


def _table_spec(tm, seq, width):
    if seq <= tm:
        return pl.BlockSpec((tm, width), lambda i: (0, 0))
    per = seq // tm
    return pl.BlockSpec((tm, width), lambda i: (i % per, 0))


def _tile_table(tab, tm, seq):
    return jnp.tile(tab, (tm // seq, 1)) if seq < tm else tab


def _kvside_body(h_ref, g_ref, wc_ref, wr_ref, wrs_ref, gc_ref, gr_ref, grs_ref, cos_ref, sin_ref):
    u = _rms(h_ref[...], g_ref[...]).astype(BF16)
    ckv = _rms(_dot(u, wc_ref[...]), gc_ref[...])
    r = _dot(u, wr_ref[...])
    rs = _dot(u, wrs_ref[...])
    inv = lax.rsqrt(jnp.mean(r * r, axis=-1, keepdims=True) + EPS)
    return ckv, r * inv * gr_ref[...] * cos_ref[...] + rs * inv * grs_ref[...] * sin_ref[...]


def _kvside_kernel(*refs):
    ckv_ref, kr_ref = refs[-2:]
    ckv_ref[...], kr_ref[...] = _kvside_body(*refs[:-2])


def _kvside(h, g, wc, wr, wrs, gc, gr, grs, cos32, sin32, seq):
    m, d = h.shape
    tm = _row_tile(m)
    assert tm % seq == 0 or seq % tm == 0
    cos32, sin32 = _tile_table(cos32, tm, seq), _tile_table(sin32, tm, seq)
    consts = (g, wc, wr, wrs, gc, gr, grs)
    return pl.pallas_call(
        _kvside_kernel,
        grid=(m // tm,),
        in_specs=[_row_spec(tm, d)] + [_const_spec(v.shape) for v in consts]
                 + [_table_spec(tm, seq, QK_ROPE)] * 2,
        out_specs=[_row_spec(tm, wc.shape[1]), _row_spec(tm, QK_ROPE)],
        out_shape=[jax.ShapeDtypeStruct((m, wc.shape[1]), F32), jax.ShapeDtypeStruct((m, QK_ROPE), F32)],
        compiler_params=_cparams(1),
    )(h, *consts, cos32, sin32)


def _group_mean_sq(x, bd_ref):
    bd = bd_ref[...]
    sq = (x * x).astype(BF16)
    return jnp.concatenate([_dot(sq[:, s:s + MXU_DIM], bd) for s in range(0, x.shape[1], MXU_DIM)], axis=1)


def _kv_expand_kernel(*refs):
    n_side = 10
    ckv_ref, kr_ref, k_ref, vt_ref = refs[-4:]
    ckv, kr = _kvside_body(*refs[:n_side])
    ckv_ref[...] = ckv
    kr_ref[...] = kr
    _expand_body(ckv, kr, *refs[n_side:-4], k_ref, vt_ref)


def _expand_body(ckv, kr, wk_ref, wvt_ref, bd_ref, gk_ref, place_ref, one_ref, k_ref, vt_ref):
    c = ckv.astype(BF16)
    k = _dot(c, wk_ref[...])
    kn = k * lax.rsqrt(_group_mean_sq(k, bd_ref) + EPS) * gk_ref[...]
    kr = _dot(kr.astype(BF16), place_ref[...])
    for hh in range(MLA_HEADS):
        hl = slice(hh * LANES, (hh + 1) * LANES)
        k_ref[:, hl] = (kn[:, hl] + kr).astype(BF16)
    vt = _dot_nt(wvt_ref[...], c)
    one = one_ref[...]
    tk = vt_ref.shape[3]
    for j in range(vt_ref.shape[1]):
        for t in range(0, tk, LANES):
            vt_ref[0, j, :, t:t + LANES] = (vt[:, j * tk + t:j * tk + t + LANES] + one).astype(BF16)


def _kv_expand(h, side, cos32, sin32, expand, seq, tk):
    m, d = h.shape
    tm = min(_row_tile(m), seq)
    assert seq % tm == 0 and tm % tk == 0
    per = seq // tm
    kv_lora = side[1].shape[1]
    vrows = expand[1].shape[0]
    width = MLA_HEADS * LANES
    return pl.pallas_call(
        _kv_expand_kernel,
        grid=(m // tm,),
        in_specs=[_row_spec(tm, d)] + [_const_spec(v.shape) for v in side]
                 + [_table_spec(tm, seq, QK_ROPE)] * 2 + [_const_spec(v.shape) for v in expand],
        out_specs=[_row_spec(tm, kv_lora), _row_spec(tm, QK_ROPE), _row_spec(tm, width),
                   pl.BlockSpec((1, tm // tk, vrows, tk), lambda i: (i // per, i % per, 0, 0))],
        out_shape=[jax.ShapeDtypeStruct((m, kv_lora), F32), jax.ShapeDtypeStruct((m, QK_ROPE), F32),
                   jax.ShapeDtypeStruct((m, width), BF16),
                   jax.ShapeDtypeStruct((m // seq, seq // tk, vrows, tk), BF16)],
        compiler_params=_cparams(1),
    )(h, *side, cos32, sin32, *expand)


def _q_kernel(h_ref, g_ref, wdq_ref, gq_ref, wq_ref, bd_ref, gh_ref, tab_ref, q_ref):
    u = _rms(h_ref[...], g_ref[...]).astype(BF16)
    cq = _rms(_dot(u, wdq_ref[...]), gq_ref[...]).astype(BF16)
    q = _dot(cq, wq_ref[...])
    qn = q * lax.rsqrt(_group_mean_sq(q, bd_ref) + EPS) * gh_ref[...]
    tab = tab_ref[...]
    for hh in range(MLA_HEADS):
        hl = slice(hh * LANES, (hh + 1) * LANES)
        q_ref[:, hl] = (qn[:, hl] * tab).astype(BF16)


def _queries(h, g, wdq, gq, wq, bd, gh, tab, seq):
    m, d = h.shape
    tm = _row_tile(m)
    assert tm % seq == 0 or seq % tm == 0
    tab = _tile_table(tab, tm, seq)
    consts = (g, wdq, gq, wq, bd, gh)
    width = MLA_HEADS * LANES
    return pl.pallas_call(
        _q_kernel,
        grid=(m // tm,),
        in_specs=[_row_spec(tm, d)] + [_const_spec(v.shape) for v in consts]
                 + [_table_spec(tm, seq, LANES)],
        out_specs=_row_spec(tm, width),
        out_shape=jax.ShapeDtypeStruct((m, width), BF16),
        compiler_params=_cparams(1),
    )(h, *consts, tab)


def _q_flat_kernel(h_ref, g_ref, wdq_ref, gq_ref, wn_ref, wr_ref, wrs_ref, bdn_ref, bdr_ref,
                   gn_ref, gr_ref, grs_ref, cos_ref, sin_ref, qn_ref, qr_ref):
    u = _rms(h_ref[...], g_ref[...]).astype(BF16)
    cq = _rms(_dot(u, wdq_ref[...]), gq_ref[...]).astype(BF16)
    qn = _dot(cq, wn_ref[...])
    qn_ref[...] = (qn * lax.rsqrt(_group_mean_sq(qn, bdn_ref) + EPS) * gn_ref[...]).astype(BF16)
    r = _dot(cq, wr_ref[...])
    rs = _dot(cq, wrs_ref[...])
    inv = lax.rsqrt(_group_mean_sq(r, bdr_ref) + EPS)
    qr_ref[...] = (r * inv * gr_ref[...] * cos_ref[...] + rs * inv * grs_ref[...] * sin_ref[...]).astype(BF16)


def _queries_flat(h, g, wdq, gq, wn, wr, wrs, bdn, bdr, gn, gr, grs, cos, sin, seq):
    m, d = h.shape
    tm = _row_tile(m)
    assert tm % seq == 0 or seq % tm == 0
    cos, sin = _tile_table(cos, tm, seq), _tile_table(sin, tm, seq)
    consts = (g, wdq, gq, wn, wr, wrs, bdn, bdr, gn, gr, grs)
    widths = (wn.shape[1], wr.shape[1])
    return pl.pallas_call(
        _q_flat_kernel,
        grid=(m // tm,),
        in_specs=[_row_spec(tm, d)] + [_const_spec(v.shape) for v in consts]
                 + [_table_spec(tm, seq, widths[1])] * 2,
        out_specs=[_row_spec(tm, w) for w in widths],
        out_shape=[jax.ShapeDtypeStruct((m, w), BF16) for w in widths],
        compiler_params=_cparams(1),
    )(h, *consts, cos, sin)


def _cache_attn_kernel(kb, qn_ref, qr_ref, ckv_ref, kr_ref, ckvn_ref, krn_ref, h_ref,
                       wk_ref, wv_ref, bd_ref, gk_ref, wo_ref, o_ref, s_s, v_s, acc_s):
    tq = qn_ref.shape[1]
    past = ckv_ref.shape[1]
    cols = MLA_HEADS * tq
    chunk_shift = CHUNK.bit_length() - 1
    assert CHUNK == 1 << chunk_shift and past % kb == 0

    qn = jnp.concatenate([qn_ref[0]] * MLA_HEADS, axis=0)
    tq_shift, nope_shift = tq.bit_length() - 1, QK_NOPE.bit_length() - 1
    assert tq == 1 << tq_shift and QK_NOPE == 1 << nope_shift
    r_head = lax.broadcasted_iota(jnp.int32, qn.shape, 0) >> tq_shift
    c_head = lax.broadcasted_iota(jnp.int32, qn.shape, 1) >> nope_shift
    q_bd = jnp.where(r_head == c_head, qn, jnp.zeros_like(qn))
    qr_all = qr_ref[0]
    q_rope = jnp.concatenate([qr_all[:, hh * QK_ROPE:(hh + 1) * QK_ROPE] for hh in range(MLA_HEADS)], axis=0)

    def expand(c, kr, rows):
        c = c.astype(BF16)
        k = _dot(c, wk_ref[...])
        kn = (k * lax.rsqrt(_group_mean_sq(k, bd_ref) + EPS) * gk_ref[...]).astype(BF16)
        s = _dot_nt(kn, q_bd) + _dot_nt(kr.astype(BF16), q_rope)
        v = _dot(c, wv_ref[...]).astype(BF16)
        v_s[rows, :] = jnp.concatenate([v, jnp.ones((v.shape[0], LANES), BF16)], axis=1)
        return s

    def expand_block(j, m_run):
        rows = pl.ds(pl.multiple_of(j * kb, kb), kb)
        s = expand(ckv_ref[0, rows, :], kr_ref[0, rows, :], rows)
        s_s[rows, :] = s
        return jnp.maximum(m_run, jnp.max(s, axis=0, keepdims=True))

    m_run = lax.fori_loop(0, past // kb, expand_block, jnp.full((1, cols), -jnp.inf, F32))
    new_rows = pl.ds(past, tq)
    s_new = expand(ckvn_ref[0], krn_ref[0], new_rows)
    k_chunk = (past + lax.broadcasted_iota(jnp.int32, (tq, cols), 0)) >> chunk_shift
    q_chunk = (past + (lax.broadcasted_iota(jnp.int32, (tq, cols), 1) & (tq - 1))) >> chunk_shift
    s_new = jnp.where(k_chunk <= q_chunk, s_new, -jnp.inf)
    m_all = jnp.maximum(m_run, jnp.max(s_new, axis=0, keepdims=True))

    acc_s[...] = _dot_tn(jnp.exp2(s_new - m_all).astype(BF16), v_s[new_rows, :])

    def pv_block(j, carry):
        rows = pl.ds(pl.multiple_of(j * kb, kb), kb)
        p = jnp.exp2(s_s[rows, :] - m_all).astype(BF16)
        acc_s[...] += _dot_tn(p, v_s[rows, :])
        return carry

    lax.fori_loop(0, past // kb, pv_block, 0)

    width = MLA_HEADS * V_HEAD
    low_half = lax.broadcasted_iota(jnp.int32, (tq, LANES), 1) < V_HEAD
    tiles = []
    for c in range(width // LANES):
        ev = acc_s[2 * c * tq:(2 * c + 1) * tq, :]
        od = acc_s[(2 * c + 1) * tq:(2 * c + 2) * tq, :]
        cl = slice(c * LANES, (c + 1) * LANES)
        tiles.append(jnp.where(low_half, ev[:, cl] / ev[:, width:], od[:, cl] / od[:, width:]))
    o = jnp.concatenate(tiles, axis=1).astype(BF16)
    o_ref[0] = h_ref[0] + _dot(o, wo_ref[...])


def _cache_attention(qn, qr, ckv, kr, ckv_new, kr_new, h, wk, wv, bd, gk, wo):
    batch, tq, _ = qn.shape
    past = ckv.shape[1]
    d = h.shape[2]
    kb = min(ROW_TILE, past)
    assert past % kb == 0 and tq % 16 == 0 and (MLA_HEADS * tq) % LANES == 0
    per = lambda a: pl.BlockSpec((1,) + a.shape[1:], lambda b: (b, 0, 0))
    consts = (wk, wv, bd, gk, wo)
    return pl.pallas_call(
        functools.partial(_cache_attn_kernel, kb),
        grid=(batch,),
        in_specs=[per(qn), per(qr), per(ckv), per(kr), per(ckv_new), per(kr_new), per(h)]
                 + [_const_spec(v.shape) for v in consts],
        out_specs=per(h),
        out_shape=jax.ShapeDtypeStruct(h.shape, F32),
        scratch_shapes=[pltpu.VMEM((past + tq, MLA_HEADS * tq), F32),
                        pltpu.VMEM((past + tq, MLA_HEADS * V_HEAD + LANES), BF16),
                        pltpu.VMEM((MLA_HEADS * tq, MLA_HEADS * V_HEAD + LANES), F32)],
        compiler_params=_cparams(1),
    )(qn, qr, ckv, kr, ckv_new, kr_new, h, *consts)


def _self_attn_kernel(q_ref, k_ref, vt_ref, h_ref, wo_ref, o_ref, m_s, acc_s, ot_s, st_s):
    i = pl.program_id(1)
    tq = q_ref.shape[1]
    chunk_shift = CHUNK.bit_length() - 1
    assert CHUNK == 1 << chunk_shift
    visible = ((lax.broadcasted_iota(jnp.int32, (tq, tq), 0) >> chunk_shift)
               <= (lax.broadcasted_iota(jnp.int32, (tq, tq), 1) >> chunk_shift))

    m_s[...] = jnp.full(m_s.shape, -jnp.inf, F32)
    acc_s[...] = jnp.zeros(acc_s.shape, F32)

    def step(j, masked):
        k0 = pl.multiple_of(j * tq, tq)
        for hh in range(MLA_HEADS):
            hl = slice(hh * LANES, (hh + 1) * LANES)
            st_s[hh] = _dot_nt(k_ref[0, pl.ds(k0, tq), hl], q_ref[0, :, hl])
        for hh in range(MLA_HEADS):
            st = st_s[hh]
            if masked:
                st = jnp.where(visible, st, -jnp.inf)
            m_old = m_s[hh]
            m_new = jnp.maximum(m_old, jnp.max(st, axis=0, keepdims=True))
            alpha = jnp.exp2(m_old - m_new)
            pt = jnp.exp2(st - m_new).astype(BF16)
            vt = vt_ref[0, j, hh * VT_ROWS:(hh + 1) * VT_ROWS, :]
            acc_s[hh] = alpha * acc_s[hh] + _dot(vt, pt)
            m_s[hh] = m_new

    def full_step(j, carry):
        step(j, False)
        return carry

    lax.fori_loop(0, i, full_step, 0)
    step(i, True)

    for hh in range(MLA_HEADS):
        acc = acc_s[hh]
        ot_s[hh * VT_ROWS:(hh + 1) * VT_ROWS, :] = (acc / acc[V_HEAD:V_HEAD + 1, :]).astype(BF16)
    o_ref[0] = h_ref[0] + _dot_tn(ot_s[...], wo_ref[...])


def _self_attention(q, k, vt, h, wo, batch):
    width = q.shape[2]
    seq = q.shape[1]
    d = h.shape[2]
    tq = vt.shape[3]
    vrows = vt.shape[2]
    assert seq % tq == 0 and tq % CHUNK == 0 and vt.shape[1] * tq == seq and vrows == MLA_HEADS * VT_ROWS
    return pl.pallas_call(
        _self_attn_kernel,
        grid=(batch, seq // tq),
        in_specs=[pl.BlockSpec((1, tq, width), lambda b, i: (b, i, 0)),
                  pl.BlockSpec((1, seq, width), lambda b, i: (b, 0, 0)),
                  pl.BlockSpec((1, seq // tq, vrows, tq), lambda b, i: (b, 0, 0, 0)),
                  pl.BlockSpec((1, tq, d), lambda b, i: (b, i, 0)),
                  _const_spec(wo.shape)],
        out_specs=pl.BlockSpec((1, tq, d), lambda b, i: (b, i, 0)),
        out_shape=jax.ShapeDtypeStruct((batch, seq, d), F32),
        scratch_shapes=[pltpu.VMEM((MLA_HEADS, 1, tq), F32), pltpu.VMEM((MLA_HEADS, VT_ROWS, tq), F32),
                        pltpu.VMEM((vrows, tq), BF16), pltpu.VMEM((MLA_HEADS, tq, tq), F32)],
        compiler_params=_cparams(2),
    )(q, k, vt, h, wo)


def _head_pad_cols(w, per_head, start, count, offset=0):
    k = w.shape[0]
    w3 = w.reshape(k, MLA_HEADS, per_head)[:, :, start:start + count]
    out = jnp.zeros((k, MLA_HEADS, LANES), w.dtype)
    out = out.at[:, :, offset:offset + count].set(w3)
    return out.reshape(k, MLA_HEADS * LANES)


def _head_rows(w3):
    h, r, n = w3.shape
    return jnp.zeros((h, VT_ROWS, n), w3.dtype).at[:, :r].set(w3).reshape(h * VT_ROWS, n)


def _head_pad_vec(pieces):
    blk = jnp.zeros((LANES,), F32)
    for off, vec in pieces:
        blk = blk.at[off:off + vec.shape[0]].set(vec.astype(F32))
    return jnp.tile(blk, MLA_HEADS)[None, :]


def _block_diag_mean(groups):
    bd = np.zeros((MXU_DIM, MXU_DIM), np.float32)
    for tile in range(MXU_DIM // LANES):
        for off, size, *spread in groups:
            lo = tile * LANES + off
            bd[lo:lo + size, lo:lo + (spread[0] if spread else size)] = 1.0 / size
    return jnp.asarray(bd, BF16)


def _rope_tables(pos):
    half = QK_ROPE // 2
    inv = ROPE_BASE ** (-jnp.arange(half, dtype=F32) / half)
    ang = pos.astype(F32)[:, None] * inv[None, :]
    return jnp.cos(ang), jnp.sin(ang)


def kernel(x_prompt, x_sample, p_prompt, p_sample, state_conv, state_ssm, cache_kv_latent, cache_k_rope, ln_ffn1, w1_gate, w1_up, w1_down, ln_mix, ln_ffn2, w2_gate, w2_up, w2_down, ln_ple, w_ple_in, w_ple_gate, a_in_proj, a_conv_w, a_conv_b, a_dt_bias, a_A_log, a_D, a_norm, a_out_proj, ln_kv, w_kv_a, kv_norm, kr_norm, w_kv_b, k_norm, b_w_dq, b_q_norm, b_w_q_b, b_qn_nope, b_qn_rope, b_w_o):
    depth = ln_ffn1.shape[0]
    n_a = a_in_proj.shape[0]
    assert depth == 2 and n_a == 1 and b_w_dq.shape[0] == 1
    d_model = x_prompt.shape[-1]
    heads = a_dt_bias.shape[-1]
    d_inner = heads * SSM_HEAD_DIM
    conv_dim = a_conv_w.shape[-1]
    kv_lora = kv_norm.shape[0]
    half = QK_ROPE // 2
    bf = lambda w: w.astype(BF16)
    row = lambda v: v.astype(F32)[None, :]

    rows = lambda v: v.astype(F32)[:, None, :]
    ffn1 = (rows(ln_ffn1), bf(w1_gate), bf(w1_up), bf(w1_down))
    ffn2 = (rows(ln_ffn2), bf(w2_gate), bf(w2_up), bf(w2_down), rows(ln_ple), bf(w_ple_in), bf(w_ple_gate))
    mamba_w = dict(
        g=row(ln_mix[0]), w_in=bf(a_in_proj[0]), convw=a_conv_w[0], convb=row(a_conv_b[0]),
        dtb=row(a_dt_bias[0]), alog=row(a_A_log[0]), dexp=row(jnp.repeat(a_D[0], SSM_HEAD_DIM)),
        ng=row(a_norm[0]), wout=bf(a_out_proj[0]))

    swap = np.concatenate([np.arange(half, QK_ROPE), np.arange(half)])
    w_r = w_kv_a[:, kv_lora:]
    kv_w = dict(g=row(ln_kv), wc=bf(w_kv_a[:, :kv_lora]), wr=bf(w_r), wrs=bf(w_r[:, swap]),
                gc=row(kv_norm), gr=row(kr_norm), grs=row(kr_norm[swap]))
    per_kv = QK_NOPE + V_HEAD
    ex_w = dict(wk=bf(_head_pad_cols(w_kv_b, per_kv, 0, QK_NOPE)),
                wvt=bf(_head_rows(w_kv_b.reshape(kv_lora, MLA_HEADS, per_kv)[:, :, QK_NOPE:].transpose(1, 2, 0))),
                bd=_block_diag_mean([(0, QK_NOPE)]),
                gk=_head_pad_vec([(0, k_norm)]),
                place=jnp.asarray(np.eye(QK_ROPE, LANES, k=QK_NOPE) + np.eye(QK_ROPE, LANES, k=QK_NOPE + QK_ROPE), BF16),
                one=jnp.asarray(np.tile((np.arange(VT_ROWS) == V_HEAD).astype(np.float32)[:, None],
                                        (MLA_HEADS, LANES))))
    per_q = QK_NOPE + QK_ROPE
    wqb = b_w_q_b[0]
    wq = (_head_pad_cols(wqb, per_q, 0, per_q)
          + _head_pad_cols(wqb, per_q, QK_NOPE + half, half, offset=per_q)
          + _head_pad_cols(wqb, per_q, QK_NOPE, half, offset=per_q + half))
    gq_rope = b_qn_rope[0]
    q_w = dict(g=row(ln_mix[1]), wdq=bf(b_w_dq[0]), gq=row(b_q_norm[0]), wq=bf(wq),
               bd=_block_diag_mean([(0, QK_NOPE), (QK_NOPE, QK_ROPE, 2 * QK_ROPE)]),
               gh=_head_pad_vec([(0, b_qn_nope[0]), (QK_NOPE, gq_rope), (per_q, gq_rope[swap])]) * EXP2_SCALE)
    wo = bf(_head_rows(b_w_o[0].reshape(MLA_HEADS, V_HEAD, d_model)))

    wkv3 = w_kv_b.reshape(kv_lora, MLA_HEADS, per_kv)
    wq3 = wqb.reshape(wqb.shape[0], MLA_HEADS, per_q)
    flat = lambda w3: bf(w3.reshape(w3.shape[0], -1))
    per_head = lambda v: jnp.tile(v.astype(F32), MLA_HEADS)[None, :]
    flat_w = dict(
        wk=flat(wkv3[:, :, :QK_NOPE]), wv=flat(wkv3[:, :, QK_NOPE:]), gk=per_head(k_norm),
        wn=flat(wq3[:, :, :QK_NOPE]), wr=flat(wq3[:, :, QK_NOPE:]), wrs=flat(wq3[:, :, QK_NOPE:][:, :, swap]),
        bdn=_block_diag_mean([(o, QK_NOPE) for o in range(0, LANES, QK_NOPE)]),
        bdr=_block_diag_mean([(o, QK_ROPE) for o in range(0, LANES, QK_ROPE)]),
        gn=per_head(b_qn_nope[0]) * EXP2_SCALE, gr=per_head(gq_rope) * EXP2_SCALE,
        grs=per_head(gq_rope[swap]) * EXP2_SCALE, wo=bf(b_w_o[0]))

    def rope_tabs(pos):
        cos, sin = _rope_tables(pos)
        seq = pos.shape[0]
        cos32 = jnp.concatenate([cos, cos], axis=1)
        sin32 = jnp.concatenate([-sin, sin], axis=1)
        tab128 = jnp.concatenate([jnp.ones((seq, QK_NOPE), F32), cos32, sin32], axis=1)
        return cos32, sin32, tab128

    def run_group(x, p, pos0, conv_in, ssm_in, past_ckv, past_kr):
        batch, seq, _ = x.shape
        m = batch * seq
        pos = pos0 + jnp.arange(seq, dtype=jnp.int32)
        cos32, sin32, tab128 = rope_tabs(pos)
        h = x.reshape(m, d_model)
        p2 = p.reshape(depth, m, p.shape[-1])

        h = _ffn(h, 0, *ffn1)
        mw = mamba_w
        z, xbc, dt = _inproj(h, mw["g"], mw["w_in"], (d_inner, conv_dim, heads))
        conv8 = jnp.pad(conv_in, ((0, 0), (SUBLANES - (CONV_W - 1), 0), (0, 0)))
        ssm4 = ssm_in.reshape(batch, SSM_GROUPS, d_inner // SSM_GROUPS, D_STATE)
        h, conv_out, ssm_out = _mamba(z, xbc, dt, h, conv8, ssm4, mw["convw"], mw["convb"], mw["dtb"],
                                      mw["alog"], mw["dexp"], mw["ng"], mw["wout"], batch, seq)
        conv_out = conv_out[None, :, SUBLANES - (CONV_W - 1):, :]
        ssm_out = ssm_out.reshape(1, batch, heads, SSM_HEAD_DIM, D_STATE)
        h = _ffn_ple(h, p2, 0, *ffn2)

        kw = kv_w
        side = (kw["g"], kw["wc"], kw["wr"], kw["wrs"], kw["gc"], kw["gr"], kw["grs"])
        if past_ckv is None:
            ew = ex_w
            ckv_new, kr_new, k_all, vt_all = _kv_expand(
                h, side, cos32, sin32, (ew["wk"], ew["wvt"], ew["bd"], ew["gk"], ew["place"], ew["one"]),
                seq, min(ATTN_TILE, seq))
        else:
            ckv_new, kr_new = _kvside(h, *side, cos32, sin32, seq)

        h = _ffn(h, 1, *ffn1)
        qw = q_w
        if past_ckv is None:
            q = _queries(h, qw["g"], qw["wdq"], qw["gq"], qw["wq"], qw["bd"], qw["gh"], tab128, seq)
            width = MLA_HEADS * LANES
            h = _self_attention(q.reshape(batch, seq, width), k_all.reshape(batch, seq, width), vt_all,
                                h.reshape(batch, seq, d_model), wo, batch)
        else:
            fw = flat_w
            cos_f, sin_f = jnp.tile(cos32, (1, MLA_HEADS)), jnp.tile(sin32, (1, MLA_HEADS))
            qn, qr = _queries_flat(h, qw["g"], qw["wdq"], qw["gq"], fw["wn"], fw["wr"], fw["wrs"], fw["bdn"],
                                   fw["bdr"], fw["gn"], fw["gr"], fw["grs"], cos_f, sin_f, seq)
            h = _cache_attention(qn.reshape(batch, seq, -1), qr.reshape(batch, seq, -1), past_ckv, past_kr,
                                 ckv_new.reshape(batch, seq, kv_lora), kr_new.reshape(batch, seq, QK_ROPE),
                                 h.reshape(batch, seq, d_model), fw["wk"], fw["wv"], fw["bdn"], fw["gk"], fw["wo"])
        h = _ffn_ple(h.reshape(m, d_model), p2, 1, *ffn2)
        return (h.reshape(batch, seq, d_model), conv_out, ssm_out,
                ckv_new.reshape(batch, seq, kv_lora), kr_new.reshape(batch, seq, QK_ROPE))

    b_p = x_prompt.shape[0]
    conv0 = jnp.zeros((b_p, CONV_W - 1, conv_dim), F32)
    ssm0 = jnp.zeros((b_p, heads, SSM_HEAD_DIM, D_STATE), F32)
    y_p, conv_p, ssm_p, kv_p, kr_p = run_group(x_prompt, p_prompt, 0, conv0, ssm0, None, None)
    y_s, conv_s, ssm_s, kv_s, kr_s = run_group(x_sample, p_sample, cache_kv_latent.shape[1],
                                               state_conv[0], state_ssm[0], cache_kv_latent, cache_k_rope)
    return (y_p, y_s, conv_p, ssm_p, kv_p, kr_p, conv_s, ssm_s, kv_s, kr_s)
```

```python
import functools

import numpy as np
import jax
import jax.numpy as jnp
from jax import lax
from jax.experimental import pallas as pl
from jax.experimental.pallas import tpu as pltpu

F32 = jnp.float32
BF16 = jnp.bfloat16

EPS = 1e-6
CHUNK = 64
SSM_HEAD_DIM = 64
SSM_GROUPS = 4
D_STATE = 128
CONV_W = 4
MLA_HEADS = 16
QK_NOPE = 64
QK_ROPE = 32
V_HEAD = 64
ROPE_BASE = 10000.0
ATTN_SCALE = (QK_NOPE + QK_ROPE) ** -0.5
EXP2_SCALE = ATTN_SCALE * float(np.log2(np.e))

LANES = 128
VT_ROWS = V_HEAD + 16
SUBLANES = 8
MXU_DIM = 256
VMEM_LIMIT_BYTES = 56 * 1024 * 1024
ROW_TILE = 512
ATTN_TILE = 256
MAMBA_TILE = 256
CONV_ROWS = 32


def _cparams(n_grid):
    return pltpu.CompilerParams(dimension_semantics=("arbitrary",) * n_grid,
                                vmem_limit_bytes=VMEM_LIMIT_BYTES)


def _const_spec(shape):
    nd = len(shape)
    return pl.BlockSpec(shape, lambda *_: (0,) * nd, pipeline_mode=pl.Buffered(1))


def _row_spec(tm, width):
    return pl.BlockSpec((tm, width), lambda i: (i, 0))


def _row_tile(m):
    tm = min(ROW_TILE, m)
    assert m % tm == 0 and tm % SUBLANES == 0
    return tm


def _dot(a, b):
    return jnp.dot(a, b, preferred_element_type=F32)


def _dot_nt(a, b):
    return lax.dot_general(a, b, (((1,), (1,)), ((), ())), preferred_element_type=F32)


def _dot_tn(a, b):
    return lax.dot_general(a, b, (((0,), (0,)), ((), ())), preferred_element_type=F32)


def _rms(x, g):
    ms = jnp.mean(x * x, axis=-1, keepdims=True)
    return x * lax.rsqrt(ms + EPS) * g


def _silu(x):
    return x * jax.nn.sigmoid(x)


def _silu_tanh(x):
    half = 0.5 * x
    return half + half * jnp.tanh(half)


def _softplus(x):
    return jnp.maximum(x, 0.0) + jnp.log1p(jnp.exp(-jnp.abs(x)))


def _split_bf16(v, n):
    pieces = []
    r = v
    for _ in range(n):
        p = r.astype(BF16)
        pieces.append(p)
        r = r - p.astype(F32)
    return pieces


def _ffn_body(h, g, wg_ref, wu_ref, wd_ref):
    u = _rms(h, g).astype(BF16)
    a = _dot(u, wg_ref[...])
    b = _dot(u, wu_ref[...])
    act = (_silu(a) * b).astype(BF16)
    return h + 0.5 * _dot(act, wd_ref[...])


def _ffn_kernel(h_ref, g_ref, wg_ref, wu_ref, wd_ref, o_ref):
    o_ref[...] = _ffn_body(h_ref[...], g_ref[...], wg_ref, wu_ref, wd_ref)


def _ffn_ple_kernel(h_ref, p_ref, g_ref, wg_ref, wu_ref, wd_ref, gp_ref, wpi_ref, wpg_ref, o_ref):
    h2 = _ffn_body(h_ref[...], g_ref[...], wg_ref, wu_ref, wd_ref)
    gate = jax.nn.sigmoid(_dot(_rms(h2, gp_ref[...]).astype(BF16), wpg_ref[...]))
    o_ref[...] = h2 + _dot(p_ref[...].astype(BF16), wpi_ref[...]) * gate


def _layer_spec(w, layer):
    return pl.BlockSpec((None,) + w.shape[1:], lambda *_: (layer, 0, 0), pipeline_mode=pl.Buffered(1))


def _ffn(h, layer, g, wg, wu, wd):
    m, d = h.shape
    tm = _row_tile(m)
    return pl.pallas_call(
        _ffn_kernel,
        grid=(m // tm,),
        in_specs=[_row_spec(tm, d)] + [_layer_spec(w, layer) for w in (g, wg, wu, wd)],
        out_specs=_row_spec(tm, d),
        out_shape=jax.ShapeDtypeStruct((m, d), F32),
        compiler_params=_cparams(1),
    )(h, g, wg, wu, wd)


def _ffn_ple(h, p, layer, g, wg, wu, wd, gp, wpi, wpg):
    m, d = h.shape
    tm = _row_tile(m)
    params = (g, wg, wu, wd, gp, wpi, wpg)
    return pl.pallas_call(
        _ffn_ple_kernel,
        grid=(m // tm,),
        in_specs=[_row_spec(tm, d), pl.BlockSpec((None, tm, p.shape[2]), lambda i: (layer, i, 0))]
                 + [_layer_spec(w, layer) for w in params],
        out_specs=_row_spec(tm, d),
        out_shape=jax.ShapeDtypeStruct((m, d), F32),
        compiler_params=_cparams(1),
    )(h, p, *params)


def _inproj_kernel(h_ref, g_ref, w_ref, z_ref, xbc_ref, dt_ref):
    u = _rms(h_ref[...], g_ref[...]).astype(BF16)
    lo = 0
    for out in (z_ref, xbc_ref, dt_ref):
        hi = lo + out.shape[1]
        out[...] = _dot(u, w_ref[:, lo:hi])
        lo = hi


def _inproj(h, g, w, widths):
    m, d = h.shape
    tm = _row_tile(m)
    assert sum(widths) == w.shape[1] and all(x % LANES == 0 for x in widths[:-1])
    return pl.pallas_call(
        _inproj_kernel,
        grid=(m // tm,),
        in_specs=[_row_spec(tm, d), _const_spec(g.shape), _const_spec(w.shape)],
        out_specs=[_row_spec(tm, x) for x in widths],
        out_shape=[jax.ShapeDtypeStruct((m, x), F32) for x in widths],
        compiler_params=_cparams(1),
    )(h, g, w)


def _mamba_kernel(chunk, z_ref, xbc_ref, dt_ref, h_ref, cst_ref, sst_ref,
                  convw_ref, convb_ref, dtb_ref, alog_ref, dexp_ref, ng_ref, wout_ref,
                  blt_ref, e3_ref, i2_ref,
                  o_ref, cout_ref, sout_ref,
                  xpad, st, x_s, bc_s, ae_s, de_s, y_s, at_s, yn_s):
    t = pl.program_id(1)
    n_t = pl.num_programs(1)
    tb = xbc_ref.shape[0]
    d_inner = x_s.shape[1]
    gn = SSM_GROUPS * D_STATE
    n_chunks = tb // chunk
    gw = d_inner // SSM_GROUPS
    half = LANES // 2
    pad_rows = half - chunk

    @pl.when(t == 0)
    def _():
        xpad[0:SUBLANES, :] = cst_ref[0]
        for g in range(SSM_GROUPS):
            st[g] = sst_ref[0, g].T

    xpad[SUBLANES:SUBLANES + tb, :] = xbc_ref[...]
    w = convw_ref[...]
    base = SUBLANES - (CONV_W - 1)
    rb = min(CONV_ROWS, chunk)
    row8 = lax.broadcasted_iota(jnp.int32, (SUBLANES, xpad.shape[1]), 0)

    def conv_rows(r0):
        ext = xpad[r0:r0 + rb + SUBLANES, :]
        slabs = [ext[s:s + SUBLANES, :] for s in range(0, rb + SUBLANES, SUBLANES)]
        ys = [convb_ref[...] + sl * w[CONV_W - 1:CONV_W, :] for sl in slabs[1:]]
        for k in range(CONV_W - 1):
            back = CONV_W - 1 - k
            rolled = [pltpu.roll(sl, back, axis=0) for sl in slabs]
            for n in range(len(ys)):
                ys[n] = ys[n] + jnp.where(row8 < back, rolled[n], rolled[n + 1]) * w[k:k + 1, :]
        y = jnp.concatenate(ys, axis=0)
        act = _silu_tanh(y)
        x_s[r0:r0 + rb, :] = act[:, :d_inner]
        bc_s[r0:r0 + rb, :] = act[:, d_inner:]

    dt = _softplus(dt_ref[...] + dtb_ref[...])
    a = dt * (-jnp.exp(alog_ref[...]))
    blt = blt_ref[...]
    acs = sum(_dot(blt, p) for p in _split_bf16(a, 3))
    e3 = e3_ref[...]
    ae_s[...] = _dot(jnp.concatenate(_split_bf16(acs, 3), axis=1), e3)
    de_s[...] = _dot(jnp.concatenate(_split_bf16(dt, 3), axis=1), e3)
    i2 = i2_ref[...]
    at_all = sum(_dot_tn(p, i2) for p in _split_bf16(acs, 3))
    for c in range(n_chunks):
        at_s[c] = at_all[:, c * LANES:(c + 1) * LANES]

    lane = lax.broadcasted_iota(jnp.int32, (chunk, LANES), 1)
    row = lax.broadcasted_iota(jnp.int32, (chunk, LANES), 0)
    low_half = lane < half
    src = lane & (half - 1)
    causal = (src <= row) & (src < chunk)
    dexp = dexp_ref[...]

    def stack_pair(top, bot):
        if pad_rows == 0:
            return jnp.concatenate([top, bot], axis=0)
        zeros = jnp.zeros((pad_rows, LANES), top.dtype)
        return jnp.concatenate([top, zeros, bot, zeros], axis=0)

    def chunk_body(c, carry):
        rows = pl.ds(c * chunk, chunk)
        last_row = pl.ds(c * chunk + chunk - 1, 1)
        at2 = at_s[c]
        for g in range(SSM_GROUPS):
            gl = slice(g * gw, (g + 1) * gw)
            ae = ae_s[rows, gl]
            xc = x_s[rows, gl]
            last = ae_s[last_row, gl]
            xdt = xc * de_s[rows, gl]
            xw = (xdt * jnp.exp(last - ae)).astype(BF16)
            xdt_b = xdt.astype(BF16)
            c_g = bc_s[rows, gn + g * D_STATE:gn + (g + 1) * D_STATE].astype(BF16)
            b_g = bc_s[rows, g * D_STATE:(g + 1) * D_STATE].astype(BF16)
            cb2 = _dot_nt(c_g, stack_pair(b_g, b_g))
            s_g = st[g]
            y_off = _dot(c_g, s_g.astype(BF16)) * jnp.exp(ae)
            pairs = []
            for q in range(gw // LANES):
                pq = g * (gw // LANES) + q
                pl_ = slice(q * LANES, (q + 1) * LANES)
                rowvec = jnp.where(low_half[0:1, :], at2[2 * pq:2 * pq + 1, :], at2[2 * pq + 1:2 * pq + 2, :])
                seg = jnp.exp(jnp.where(causal, ae[:, pl_] - rowvec, -jnp.inf))
                wmat = (cb2 * seg).astype(BF16)
                xp = xdt_b[:, pl_]
                zero = jnp.zeros_like(xp)
                x2 = stack_pair(jnp.where(low_half, xp, zero), jnp.where(low_half, zero, xp))
                pairs.append(_dot(wmat, x2))
            y_diag = jnp.concatenate(pairs, axis=1)
            y_s[rows, gl] = y_diag + y_off + dexp[:, gl] * xc
            st[g] = s_g * jnp.exp(last) + _dot_tn(b_g, xw)
        return carry

    for c in range(n_chunks):
        for r0 in range(c * chunk, (c + 1) * chunk, rb):
            conv_rows(r0)
        chunk_body(c, 0)
        rows = pl.ds(c * chunk, chunk)
        yz = y_s[rows, :] * _silu_tanh(z_ref[rows, :])
        yn_s[rows, :] = _rms(yz, ng_ref[...]).astype(BF16)
    tail = xpad[tb:tb + SUBLANES, :]
    cout_ref[0] = tail
    xpad[0:SUBLANES, :] = tail
    o_ref[...] = h_ref[...] + _dot(yn_s[...], wout_ref[...])

    @pl.when(t == n_t - 1)
    def _():
        for g in range(SSM_GROUPS):
            sout_ref[0, g] = st[g].T


def _mamba(z, xbc, dt, h, conv_state8, ssm_state, convw, convb, dtb, alog, dexp, ng, wout, batch, seq):
    m, d_inner = z.shape
    conv_dim = xbc.shape[1]
    heads = dt.shape[1]
    d = h.shape[1]
    tb = min(MAMBA_TILE, seq)
    chunk = min(CHUNK, seq)
    assert seq % tb == 0 and tb % chunk == 0 and chunk % SUBLANES == 0 and LANES // 2 % chunk == 0
    n_t = seq // tb
    n_chunks = tb // chunk
    gw = d_inner // SSM_GROUPS

    r = np.arange(tb)
    blt = ((r[:, None] // chunk == r[None, :] // chunk) & (r[None, :] <= r[:, None])).astype(np.float32)
    e = (np.arange(d_inner)[None, :] // SSM_HEAD_DIM == np.arange(heads)[:, None]).astype(np.float32)
    e = np.concatenate([e, e, e], axis=0)
    col = np.arange(n_chunks * LANES)
    i2 = ((col[None, :] // LANES == r[:, None] // chunk)
          & (col[None, :] % (LANES // 2) == r[:, None] % chunk)).astype(np.float32)
    blt, e, i2 = (jnp.asarray(v, BF16) for v in (blt, e, i2))

    row_map = lambda b, t: (b * n_t + t, 0)
    const = lambda v: _const_spec(v.shape)
    return pl.pallas_call(
        functools.partial(_mamba_kernel, chunk),
        grid=(batch, n_t),
        in_specs=[pl.BlockSpec((tb, d_inner), row_map), pl.BlockSpec((tb, conv_dim), row_map),
                  pl.BlockSpec((tb, heads), row_map), pl.BlockSpec((tb, d), row_map),
                  pl.BlockSpec((1, SUBLANES, conv_dim), lambda b, t: (b, 0, 0)),
                  pl.BlockSpec((1, SSM_GROUPS, gw, D_STATE), lambda b, t: (b, 0, 0, 0)),
                  const(convw), const(convb), const(dtb), const(alog), const(dexp), const(ng), const(wout),
                  const(blt), const(e), const(i2)],
        out_specs=[pl.BlockSpec((tb, d), row_map),
                   pl.BlockSpec((1, SUBLANES, conv_dim), lambda b, t: (b, 0, 0)),
                   pl.BlockSpec((1, SSM_GROUPS, gw, D_STATE), lambda b, t: (b, 0, 0, 0))],
        out_shape=[jax.ShapeDtypeStruct((m, d), F32),
                   jax.ShapeDtypeStruct((batch, SUBLANES, conv_dim), F32),
                   jax.ShapeDtypeStruct((batch, SSM_GROUPS, gw, D_STATE), F32)],
        scratch_shapes=[pltpu.VMEM((tb + SUBLANES, conv_dim), F32),
                        pltpu.VMEM((SSM_GROUPS, D_STATE, gw), F32),
                        pltpu.VMEM((tb, d_inner), F32),
                        pltpu.VMEM((tb, conv_dim - d_inner), F32),
                        pltpu.VMEM((tb, d_inner), F32),
                        pltpu.VMEM((tb, d_inner), F32),
                        pltpu.VMEM((tb, d_inner), F32),
                        pltpu.VMEM((n_chunks, heads, LANES), F32),
                        pltpu.VMEM((tb, d_inner), BF16)],
        compiler_params=_cparams(2),
    )(z, xbc, dt, h, conv_state8, ssm_state, convw, convb, dtb, alog, dexp, ng, wout, blt, e, i2)


def _table_spec(tm, seq, width):
    if seq <= tm:
        return pl.BlockSpec((tm, width), lambda i: (0, 0))
    per = seq // tm
    return pl.BlockSpec((tm, width), lambda i: (i % per, 0))


def _tile_table(tab, tm, seq):
    return jnp.tile(tab, (tm // seq, 1)) if seq < tm else tab


def _kvside_body(h_ref, g_ref, wc_ref, wr_ref, wrs_ref, gc_ref, gr_ref, grs_ref, cos_ref, sin_ref):
    u = _rms(h_ref[...], g_ref[...]).astype(BF16)
    ckv = _rms(_dot(u, wc_ref[...]), gc_ref[...])
    r = _dot(u, wr_ref[...])
    rs = _dot(u, wrs_ref[...])
    inv = lax.rsqrt(jnp.mean(r * r, axis=-1, keepdims=True) + EPS)
    return ckv, r * inv * gr_ref[...] * cos_ref[...] + rs * inv * grs_ref[...] * sin_ref[...]


def _kvside_kernel(*refs):
    ckv_ref, kr_ref = refs[-2:]
    ckv_ref[...], kr_ref[...] = _kvside_body(*refs[:-2])


def _kvside(h, g, wc, wr, wrs, gc, gr, grs, cos32, sin32, seq):
    m, d = h.shape
    tm = _row_tile(m)
    assert tm % seq == 0 or seq % tm == 0
    cos32, sin32 = _tile_table(cos32, tm, seq), _tile_table(sin32, tm, seq)
    consts = (g, wc, wr, wrs, gc, gr, grs)
    return pl.pallas_call(
        _kvside_kernel,
        grid=(m // tm,),
        in_specs=[_row_spec(tm, d)] + [_const_spec(v.shape) for v in consts]
                 + [_table_spec(tm, seq, QK_ROPE)] * 2,
        out_specs=[_row_spec(tm, wc.shape[1]), _row_spec(tm, QK_ROPE)],
        out_shape=[jax.ShapeDtypeStruct((m, wc.shape[1]), F32), jax.ShapeDtypeStruct((m, QK_ROPE), F32)],
        compiler_params=_cparams(1),
    )(h, *consts, cos32, sin32)


def _group_mean_sq(x, bd_ref):
    bd = bd_ref[...]
    sq = (x * x).astype(BF16)
    return jnp.concatenate([_dot(sq[:, s:s + MXU_DIM], bd) for s in range(0, x.shape[1], MXU_DIM)], axis=1)


def _kv_expand_kernel(*refs):
    n_side = 10
    ckv_ref, kr_ref, k_ref, vt_ref = refs[-4:]
    ckv, kr = _kvside_body(*refs[:n_side])
    ckv_ref[...] = ckv
    kr_ref[...] = kr
    _expand_body(ckv, kr, *refs[n_side:-4], k_ref, vt_ref)


def _expand_body(ckv, kr, wk_ref, wvt_ref, bd_ref, gk_ref, place_ref, one_ref, k_ref, vt_ref):
    c = ckv.astype(BF16)
    k = _dot(c, wk_ref[...])
    kn = k * lax.rsqrt(_group_mean_sq(k, bd_ref) + EPS) * gk_ref[...]
    kr = _dot(kr.astype(BF16), place_ref[...])
    for hh in range(MLA_HEADS):
        hl = slice(hh * LANES, (hh + 1) * LANES)
        k_ref[:, hl] = (kn[:, hl] + kr).astype(BF16)
    vt = _dot_nt(wvt_ref[...], c)
    one = one_ref[...]
    tk = vt_ref.shape[3]
    for j in range(vt_ref.shape[1]):
        for t in range(0, tk, LANES):
            vt_ref[0, j, :, t:t + LANES] = (vt[:, j * tk + t:j * tk + t + LANES] + one).astype(BF16)


def _kv_expand(h, side, cos32, sin32, expand, seq, tk):
    m, d = h.shape
    tm = min(_row_tile(m), seq)
    assert seq % tm == 0 and tm % tk == 0
    per = seq // tm
    kv_lora = side[1].shape[1]
    vrows = expand[1].shape[0]
    width = MLA_HEADS * LANES
    return pl.pallas_call(
        _kv_expand_kernel,
        grid=(m // tm,),
        in_specs=[_row_spec(tm, d)] + [_const_spec(v.shape) for v in side]
                 + [_table_spec(tm, seq, QK_ROPE)] * 2 + [_const_spec(v.shape) for v in expand],
        out_specs=[_row_spec(tm, kv_lora), _row_spec(tm, QK_ROPE), _row_spec(tm, width),
                   pl.BlockSpec((1, tm // tk, vrows, tk), lambda i: (i // per, i % per, 0, 0))],
        out_shape=[jax.ShapeDtypeStruct((m, kv_lora), F32), jax.ShapeDtypeStruct((m, QK_ROPE), F32),
                   jax.ShapeDtypeStruct((m, width), BF16),
                   jax.ShapeDtypeStruct((m // seq, seq // tk, vrows, tk), BF16)],
        compiler_params=_cparams(1),
    )(h, *side, cos32, sin32, *expand)


def _q_kernel(h_ref, g_ref, wdq_ref, gq_ref, wq_ref, bd_ref, gh_ref, tab_ref, q_ref):
    u = _rms(h_ref[...], g_ref[...]).astype(BF16)
    cq = _rms(_dot(u, wdq_ref[...]), gq_ref[...]).astype(BF16)
    q = _dot(cq, wq_ref[...])
    qn = q * lax.rsqrt(_group_mean_sq(q, bd_ref) + EPS) * gh_ref[...]
    tab = tab_ref[...]
    for hh in range(MLA_HEADS):
        hl = slice(hh * LANES, (hh + 1) * LANES)
        q_ref[:, hl] = (qn[:, hl] * tab).astype(BF16)


def _queries(h, g, wdq, gq, wq, bd, gh, tab, seq):
    m, d = h.shape
    tm = _row_tile(m)
    assert tm % seq == 0 or seq % tm == 0
    tab = _tile_table(tab, tm, seq)
    consts = (g, wdq, gq, wq, bd, gh)
    width = MLA_HEADS * LANES
    return pl.pallas_call(
        _q_kernel,
        grid=(m // tm,),
        in_specs=[_row_spec(tm, d)] + [_const_spec(v.shape) for v in consts]
                 + [_table_spec(tm, seq, LANES)],
        out_specs=_row_spec(tm, width),
        out_shape=jax.ShapeDtypeStruct((m, width), BF16),
        compiler_params=_cparams(1),
    )(h, *consts, tab)


def _q_flat_kernel(h_ref, g_ref, wdq_ref, gq_ref, wn_ref, wr_ref, wrs_ref, bdn_ref, bdr_ref,
                   gn_ref, gr_ref, grs_ref, cos_ref, sin_ref, qn_ref, qr_ref):
    u = _rms(h_ref[...], g_ref[...]).astype(BF16)
    cq = _rms(_dot(u, wdq_ref[...]), gq_ref[...]).astype(BF16)
    qn = _dot(cq, wn_ref[...])
    qn_ref[...] = (qn * lax.rsqrt(_group_mean_sq(qn, bdn_ref) + EPS) * gn_ref[...]).astype(BF16)
    r = _dot(cq, wr_ref[...])
    rs = _dot(cq, wrs_ref[...])
    inv = lax.rsqrt(_group_mean_sq(r, bdr_ref) + EPS)
    qr_ref[...] = (r * inv * gr_ref[...] * cos_ref[...] + rs * inv * grs_ref[...] * sin_ref[...]).astype(BF16)


def _queries_flat(h, g, wdq, gq, wn, wr, wrs, bdn, bdr, gn, gr, grs, cos, sin, seq):
    m, d = h.shape
    tm = _row_tile(m)
    assert tm % seq == 0 or seq % tm == 0
    cos, sin = _tile_table(cos, tm, seq), _tile_table(sin, tm, seq)
    consts = (g, wdq, gq, wn, wr, wrs, bdn, bdr, gn, gr, grs)
    widths = (wn.shape[1], wr.shape[1])
    return pl.pallas_call(
        _q_flat_kernel,
        grid=(m // tm,),
        in_specs=[_row_spec(tm, d)] + [_const_spec(v.shape) for v in consts]
                 + [_table_spec(tm, seq, widths[1])] * 2,
        out_specs=[_row_spec(tm, w) for w in widths],
        out_shape=[jax.ShapeDtypeStruct((m, w), BF16) for w in widths],
        compiler_params=_cparams(1),
    )(h, *consts, cos, sin)


def _cache_attn_kernel(kb, qn_ref, qr_ref, ckv_ref, kr_ref, ckvn_ref, krn_ref, h_ref,
                       wk_ref, wv_ref, bd_ref, gk_ref, wo_ref, o_ref, s_s, v_s, acc_s):
    tq = qn_ref.shape[1]
    past = ckv_ref.shape[1]
    cols = MLA_HEADS * tq
    chunk_shift = CHUNK.bit_length() - 1
    assert CHUNK == 1 << chunk_shift and past % kb == 0

    qn = jnp.concatenate([qn_ref[0]] * MLA_HEADS, axis=0)
    tq_shift, nope_shift = tq.bit_length() - 1, QK_NOPE.bit_length() - 1
    assert tq == 1 << tq_shift and QK_NOPE == 1 << nope_shift
    r_head = lax.broadcasted_iota(jnp.int32, qn.shape, 0) >> tq_shift
    c_head = lax.broadcasted_iota(jnp.int32, qn.shape, 1) >> nope_shift
    q_bd = jnp.where(r_head == c_head, qn, jnp.zeros_like(qn))
    qr_all = qr_ref[0]
    q_rope = jnp.concatenate([qr_all[:, hh * QK_ROPE:(hh + 1) * QK_ROPE] for hh in range(MLA_HEADS)], axis=0)

    def expand(c, kr, rows):
        c = c.astype(BF16)
        k = _dot(c, wk_ref[...])
        kn = (k * lax.rsqrt(_group_mean_sq(k, bd_ref) + EPS) * gk_ref[...]).astype(BF16)
        s = _dot_nt(kn, q_bd) + _dot_nt(kr.astype(BF16), q_rope)
        v = _dot(c, wv_ref[...]).astype(BF16)
        v_s[rows, :] = jnp.concatenate([v, jnp.ones((v.shape[0], LANES), BF16)], axis=1)
        return s

    def expand_block(j, m_run):
        rows = pl.ds(pl.multiple_of(j * kb, kb), kb)
        s = expand(ckv_ref[0, rows, :], kr_ref[0, rows, :], rows)
        s_s[rows, :] = s
        return jnp.maximum(m_run, jnp.max(s, axis=0, keepdims=True))

    m_run = lax.fori_loop(0, past // kb, expand_block, jnp.full((1, cols), -jnp.inf, F32))
    new_rows = pl.ds(past, tq)
    s_new = expand(ckvn_ref[0], krn_ref[0], new_rows)
    k_chunk = (past + lax.broadcasted_iota(jnp.int32, (tq, cols), 0)) >> chunk_shift
    q_chunk = (past + (lax.broadcasted_iota(jnp.int32, (tq, cols), 1) & (tq - 1))) >> chunk_shift
    s_new = jnp.where(k_chunk <= q_chunk, s_new, -jnp.inf)
    m_all = jnp.maximum(m_run, jnp.max(s_new, axis=0, keepdims=True))

    acc_s[...] = _dot_tn(jnp.exp2(s_new - m_all).astype(BF16), v_s[new_rows, :])

    def pv_block(j, carry):
        rows = pl.ds(pl.multiple_of(j * kb, kb), kb)
        p = jnp.exp2(s_s[rows, :] - m_all).astype(BF16)
        acc_s[...] += _dot_tn(p, v_s[rows, :])
        return carry

    lax.fori_loop(0, past // kb, pv_block, 0)

    width = MLA_HEADS * V_HEAD
    low_half = lax.broadcasted_iota(jnp.int32, (tq, LANES), 1) < V_HEAD
    tiles = []
    for c in range(width // LANES):
        ev = acc_s[2 * c * tq:(2 * c + 1) * tq, :]
        od = acc_s[(2 * c + 1) * tq:(2 * c + 2) * tq, :]
        cl = slice(c * LANES, (c + 1) * LANES)
        tiles.append(jnp.where(low_half, ev[:, cl] / ev[:, width:], od[:, cl] / od[:, width:]))
    o = jnp.concatenate(tiles, axis=1).astype(BF16)
    o_ref[0] = h_ref[0] + _dot(o, wo_ref[...])


def _cache_attention(qn, qr, ckv, kr, ckv_new, kr_new, h, wk, wv, bd, gk, wo):
    batch, tq, _ = qn.shape
    past = ckv.shape[1]
    d = h.shape[2]
    kb = min(ROW_TILE, past)
    assert past % kb == 0 and tq % 16 == 0 and (MLA_HEADS * tq) % LANES == 0
    per = lambda a: pl.BlockSpec((1,) + a.shape[1:], lambda b: (b, 0, 0))
    consts = (wk, wv, bd, gk, wo)
    return pl.pallas_call(
        functools.partial(_cache_attn_kernel, kb),
        grid=(batch,),
        in_specs=[per(qn), per(qr), per(ckv), per(kr), per(ckv_new), per(kr_new), per(h)]
                 + [_const_spec(v.shape) for v in consts],
        out_specs=per(h),
        out_shape=jax.ShapeDtypeStruct(h.shape, F32),
        scratch_shapes=[pltpu.VMEM((past + tq, MLA_HEADS * tq), F32),
                        pltpu.VMEM((past + tq, MLA_HEADS * V_HEAD + LANES), BF16),
                        pltpu.VMEM((MLA_HEADS * tq, MLA_HEADS * V_HEAD + LANES), F32)],
        compiler_params=_cparams(1),
    )(qn, qr, ckv, kr, ckv_new, kr_new, h, *consts)


def _self_attn_kernel(q_ref, k_ref, vt_ref, h_ref, wo_ref, o_ref, m_s, acc_s, ot_s, st_s):
    i = pl.program_id(1)
    tq = q_ref.shape[1]
    chunk_shift = CHUNK.bit_length() - 1
    assert CHUNK == 1 << chunk_shift
    visible = ((lax.broadcasted_iota(jnp.int32, (tq, tq), 0) >> chunk_shift)
               <= (lax.broadcasted_iota(jnp.int32, (tq, tq), 1) >> chunk_shift))

    m_s[...] = jnp.full(m_s.shape, -jnp.inf, F32)
    acc_s[...] = jnp.zeros(acc_s.shape, F32)

    def step(j, n_tiles, masked):
        rows = n_tiles * tq
        k0 = pl.multiple_of(j * tq, tq)
        for hh in range(MLA_HEADS):
            hl = slice(hh * LANES, (hh + 1) * LANES)
            st_s[hh, 0:rows, :] = _dot_nt(k_ref[0, pl.ds(k0, rows), hl], q_ref[0, :, hl])
        for hh in range(MLA_HEADS):
            st = st_s[hh, 0:rows, :]
            if masked:
                st = jnp.where(visible, st, -jnp.inf)
            m_old = m_s[hh]
            m_new = jnp.maximum(m_old, jnp.max(st, axis=0, keepdims=True))
            alpha = jnp.exp2(m_old - m_new)
            pt = jnp.exp2(st - m_new).astype(BF16)
            vrows = slice(hh * VT_ROWS, (hh + 1) * VT_ROWS)
            vt = jnp.concatenate([vt_ref[0, j + n, vrows, :] for n in range(n_tiles)], axis=1)
            acc_s[hh] = alpha * acc_s[hh] + _dot(vt, pt)
            m_s[hh] = m_new

    def pair_step(jj, carry):
        step(2 * jj, 2, False)
        return carry

    lax.fori_loop(0, i // 2, pair_step, 0)

    @pl.when(i % 2 == 1)
    def _():
        step(i - 1, 1, False)

    step(i, 1, True)

    for hh in range(MLA_HEADS):
        acc = acc_s[hh]
        ot_s[hh * VT_ROWS:(hh + 1) * VT_ROWS, :] = (acc / acc[V_HEAD:V_HEAD + 1, :]).astype(BF16)
    o_ref[0] = h_ref[0] + _dot_tn(ot_s[...], wo_ref[...])


def _self_attention(q, k, vt, h, wo, batch):
    width = q.shape[2]
    seq = q.shape[1]
    d = h.shape[2]
    tq = vt.shape[3]
    vrows = vt.shape[2]
    assert seq % tq == 0 and tq % CHUNK == 0 and vt.shape[1] * tq == seq and vrows == MLA_HEADS * VT_ROWS
    return pl.pallas_call(
        _self_attn_kernel,
        grid=(batch, seq // tq),
        in_specs=[pl.BlockSpec((1, tq, width), lambda b, i: (b, i, 0)),
                  pl.BlockSpec((1, seq, width), lambda b, i: (b, 0, 0)),
                  pl.BlockSpec((1, seq // tq, vrows, tq), lambda b, i: (b, 0, 0, 0)),
                  pl.BlockSpec((1, tq, d), lambda b, i: (b, i, 0)),
                  _const_spec(wo.shape)],
        out_specs=pl.BlockSpec((1, tq, d), lambda b, i: (b, i, 0)),
        out_shape=jax.ShapeDtypeStruct((batch, seq, d), F32),
        scratch_shapes=[pltpu.VMEM((MLA_HEADS, 1, tq), F32), pltpu.VMEM((MLA_HEADS, VT_ROWS, tq), F32),
                        pltpu.VMEM((vrows, tq), BF16), pltpu.VMEM((MLA_HEADS, 2 * tq, tq), F32)],
        compiler_params=_cparams(2),
    )(q, k, vt, h, wo)


def _head_pad_cols(w, per_head, start, count, offset=0):
    k = w.shape[0]
    w3 = w.reshape(k, MLA_HEADS, per_head)[:, :, start:start + count]
    out = jnp.zeros((k, MLA_HEADS, LANES), w.dtype)
    out = out.at[:, :, offset:offset + count].set(w3)
    return out.reshape(k, MLA_HEADS * LANES)


def _head_rows(w3):
    h, r, n = w3.shape
    return jnp.zeros((h, VT_ROWS, n), w3.dtype).at[:, :r].set(w3).reshape(h * VT_ROWS, n)


def _head_pad_vec(pieces):
    blk = jnp.zeros((LANES,), F32)
    for off, vec in pieces:
        blk = blk.at[off:off + vec.shape[0]].set(vec.astype(F32))
    return jnp.tile(blk, MLA_HEADS)[None, :]


def _block_diag_mean(groups):
    bd = np.zeros((MXU_DIM, MXU_DIM), np.float32)
    for tile in range(MXU_DIM // LANES):
        for off, size, *spread in groups:
            lo = tile * LANES + off
            bd[lo:lo + size, lo:lo + (spread[0] if spread else size)] = 1.0 / size
    return jnp.asarray(bd, BF16)


def _rope_tables(pos):
    half = QK_ROPE // 2
    inv = ROPE_BASE ** (-jnp.arange(half, dtype=F32) / half)
    ang = pos.astype(F32)[:, None] * inv[None, :]
    return jnp.cos(ang), jnp.sin(ang)


def kernel(x_prompt, x_sample, p_prompt, p_sample, state_conv, state_ssm, cache_kv_latent, cache_k_rope, ln_ffn1, w1_gate, w1_up, w1_down, ln_mix, ln_ffn2, w2_gate, w2_up, w2_down, ln_ple, w_ple_in, w_ple_gate, a_in_proj, a_conv_w, a_conv_b, a_dt_bias, a_A_log, a_D, a_norm, a_out_proj, ln_kv, w_kv_a, kv_norm, kr_norm, w_kv_b, k_norm, b_w_dq, b_q_norm, b_w_q_b, b_qn_nope, b_qn_rope, b_w_o):
    depth = ln_ffn1.shape[0]
    n_a = a_in_proj.shape[0]
    assert depth == 2 and n_a == 1 and b_w_dq.shape[0] == 1
    d_model = x_prompt.shape[-1]
    heads = a_dt_bias.shape[-1]
    d_inner = heads * SSM_HEAD_DIM
    conv_dim = a_conv_w.shape[-1]
    kv_lora = kv_norm.shape[0]
    half = QK_ROPE // 2
    bf = lambda w: w.astype(BF16)
    row = lambda v: v.astype(F32)[None, :]

    rows = lambda v: v.astype(F32)[:, None, :]
    ffn1 = (rows(ln_ffn1), bf(w1_gate), bf(w1_up), bf(w1_down))
    ffn2 = (rows(ln_ffn2), bf(w2_gate), bf(w2_up), bf(w2_down), rows(ln_ple), bf(w_ple_in), bf(w_ple_gate))
    mamba_w = dict(
        g=row(ln_mix[0]), w_in=bf(a_in_proj[0]), convw=a_conv_w[0], convb=row(a_conv_b[0]),
        dtb=row(a_dt_bias[0]), alog=row(a_A_log[0]), dexp=row(jnp.repeat(a_D[0], SSM_HEAD_DIM)),
        ng=row(a_norm[0]), wout=bf(a_out_proj[0]))

    swap = np.concatenate([np.arange(half, QK_ROPE), np.arange(half)])
    w_r = w_kv_a[:, kv_lora:]
    kv_w = dict(g=row(ln_kv), wc=bf(w_kv_a[:, :kv_lora]), wr=bf(w_r), wrs=bf(w_r[:, swap]),
                gc=row(kv_norm), gr=row(kr_norm), grs=row(kr_norm[swap]))
    per_kv = QK_NOPE + V_HEAD
    ex_w = dict(wk=bf(_head_pad_cols(w_kv_b, per_kv, 0, QK_NOPE)),
                wvt=bf(_head_rows(w_kv_b.reshape(kv_lora, MLA_HEADS, per_kv)[:, :, QK_NOPE:].transpose(1, 2, 0))),
                bd=_block_diag_mean([(0, QK_NOPE)]),
                gk=_head_pad_vec([(0, k_norm)]),
                place=jnp.asarray(np.eye(QK_ROPE, LANES, k=QK_NOPE) + np.eye(QK_ROPE, LANES, k=QK_NOPE + QK_ROPE), BF16),
                one=jnp.asarray(np.tile((np.arange(VT_ROWS) == V_HEAD).astype(np.float32)[:, None],
                                        (MLA_HEADS, LANES))))
    per_q = QK_NOPE + QK_ROPE
    wqb = b_w_q_b[0]
    wq = (_head_pad_cols(wqb, per_q, 0, per_q)
          + _head_pad_cols(wqb, per_q, QK_NOPE + half, half, offset=per_q)
          + _head_pad_cols(wqb, per_q, QK_NOPE, half, offset=per_q + half))
    gq_rope = b_qn_rope[0]
    q_w = dict(g=row(ln_mix[1]), wdq=bf(b_w_dq[0]), gq=row(b_q_norm[0]), wq=bf(wq),
               bd=_block_diag_mean([(0, QK_NOPE), (QK_NOPE, QK_ROPE, 2 * QK_ROPE)]),
               gh=_head_pad_vec([(0, b_qn_nope[0]), (QK_NOPE, gq_rope), (per_q, gq_rope[swap])]) * EXP2_SCALE)
    wo = bf(_head_rows(b_w_o[0].reshape(MLA_HEADS, V_HEAD, d_model)))

    wkv3 = w_kv_b.reshape(kv_lora, MLA_HEADS, per_kv)
    wq3 = wqb.reshape(wqb.shape[0], MLA_HEADS, per_q)
    flat = lambda w3: bf(w3.reshape(w3.shape[0], -1))
    per_head = lambda v: jnp.tile(v.astype(F32), MLA_HEADS)[None, :]
    flat_w = dict(
        wk=flat(wkv3[:, :, :QK_NOPE]), wv=flat(wkv3[:, :, QK_NOPE:]), gk=per_head(k_norm),
        wn=flat(wq3[:, :, :QK_NOPE]), wr=flat(wq3[:, :, QK_NOPE:]), wrs=flat(wq3[:, :, QK_NOPE:][:, :, swap]),
        bdn=_block_diag_mean([(o, QK_NOPE) for o in range(0, LANES, QK_NOPE)]),
        bdr=_block_diag_mean([(o, QK_ROPE) for o in range(0, LANES, QK_ROPE)]),
        gn=per_head(b_qn_nope[0]) * EXP2_SCALE, gr=per_head(gq_rope) * EXP2_SCALE,
        grs=per_head(gq_rope[swap]) * EXP2_SCALE, wo=bf(b_w_o[0]))

    def rope_tabs(pos):
        cos, sin = _rope_tables(pos)
        seq = pos.shape[0]
        cos32 = jnp.concatenate([cos, cos], axis=1)
        sin32 = jnp.concatenate([-sin, sin], axis=1)
        tab128 = jnp.concatenate([jnp.ones((seq, QK_NOPE), F32), cos32, sin32], axis=1)
        return cos32, sin32, tab128

    def run_group(x, p, pos0, conv_in, ssm_in, past_ckv, past_kr):
        batch, seq, _ = x.shape
        m = batch * seq
        pos = pos0 + jnp.arange(seq, dtype=jnp.int32)
        cos32, sin32, tab128 = rope_tabs(pos)
        h = x.reshape(m, d_model)
        p2 = p.reshape(depth, m, p.shape[-1])

        h = _ffn(h, 0, *ffn1)
        mw = mamba_w
        z, xbc, dt = _inproj(h, mw["g"], mw["w_in"], (d_inner, conv_dim, heads))
        conv8 = jnp.pad(conv_in, ((0, 0), (SUBLANES - (CONV_W - 1), 0), (0, 0)))
        ssm4 = ssm_in.reshape(batch, SSM_GROUPS, d_inner // SSM_GROUPS, D_STATE)
        h, conv_out, ssm_out = _mamba(z, xbc, dt, h, conv8, ssm4, mw["convw"], mw["convb"], mw["dtb"],
                                      mw["alog"], mw["dexp"], mw["ng"], mw["wout"], batch, seq)
        conv_out = conv_out[None, :, SUBLANES - (CONV_W - 1):, :]
        ssm_out = ssm_out.reshape(1, batch, heads, SSM_HEAD_DIM, D_STATE)
        h = _ffn_ple(h, p2, 0, *ffn2)

        kw = kv_w
        side = (kw["g"], kw["wc"], kw["wr"], kw["wrs"], kw["gc"], kw["gr"], kw["grs"])
        if past_ckv is None:
            ew = ex_w
            ckv_new, kr_new, k_all, vt_all = _kv_expand(
                h, side, cos32, sin32, (ew["wk"], ew["wvt"], ew["bd"], ew["gk"], ew["place"], ew["one"]),
                seq, min(ATTN_TILE, seq))
        else:
            ckv_new, kr_new = _kvside(h, *side, cos32, sin32, seq)

        h = _ffn(h, 1, *ffn1)
        qw = q_w
        if past_ckv is None:
            q = _queries(h, qw["g"], qw["wdq"], qw["gq"], qw["wq"], qw["bd"], qw["gh"], tab128, seq)
            width = MLA_HEADS * LANES
            h = _self_attention(q.reshape(batch, seq, width), k_all.reshape(batch, seq, width), vt_all,
                                h.reshape(batch, seq, d_model), wo, batch)
        else:
            fw = flat_w
            cos_f, sin_f = jnp.tile(cos32, (1, MLA_HEADS)), jnp.tile(sin32, (1, MLA_HEADS))
            qn, qr = _queries_flat(h, qw["g"], qw["wdq"], qw["gq"], fw["wn"], fw["wr"], fw["wrs"], fw["bdn"],
                                   fw["bdr"], fw["gn"], fw["gr"], fw["grs"], cos_f, sin_f, seq)
            h = _cache_attention(qn.reshape(batch, seq, -1), qr.reshape(batch, seq, -1), past_ckv, past_kr,
                                 ckv_new.reshape(batch, seq, kv_lora), kr_new.reshape(batch, seq, QK_ROPE),
                                 h.reshape(batch, seq, d_model), fw["wk"], fw["wv"], fw["bdn"], fw["gk"], fw["wo"])
        h = _ffn_ple(h.reshape(m, d_model), p2, 1, *ffn2)
        return (h.reshape(batch, seq, d_model), conv_out, ssm_out,
                ckv_new.reshape(batch, seq, kv_lora), kr_new.reshape(batch, seq, QK_ROPE))

    b_p = x_prompt.shape[0]
    conv0 = jnp.zeros((b_p, CONV_W - 1, conv_dim), F32)
    ssm0 = jnp.zeros((b_p, heads, SSM_HEAD_DIM, D_STATE), F32)
    y_p, conv_p, ssm_p, kv_p, kr_p = run_group(x_prompt, p_prompt, 0, conv0, ssm0, None, None)
    y_s, conv_s, ssm_s, kv_s, kr_s = run_group(x_sample, p_sample, cache_kv_latent.shape[1],
                                               state_conv[0], state_ssm[0], cache_kv_latent, cache_k_rope)
    return (y_p, y_s, conv_p, ssm_p, kv_p, kr_p, conv_s, ssm_s, kv_s, kr_s)
```

```python
import functools

import numpy as np
import jax
import jax.numpy as jnp
from jax import lax
from jax.experimental import pallas as pl
from jax.experimental.pallas import tpu as pltpu

F32 = jnp.float32
BF16 = jnp.bfloat16

EPS = 1e-6
CHUNK = 64
SSM_HEAD_DIM = 64
SSM_GROUPS = 4
D_STATE = 128
CONV_W = 4
MLA_HEADS = 16
QK_NOPE = 64
QK_ROPE = 32
V_HEAD = 64
ROPE_BASE = 10000.0
ATTN_SCALE = (QK_NOPE + QK_ROPE) ** -0.5
EXP2_SCALE = ATTN_SCALE * float(np.log2(np.e))

LANES = 128
VT_ROWS = V_HEAD + 16
SUBLANES = 8
MXU_DIM = 256
VMEM_LIMIT_BYTES = 56 * 1024 * 1024
ROW_TILE = 512
ATTN_TILE = 256
MAMBA_TILE = 256
CONV_ROWS = 32


def _cparams(n_grid):
    return pltpu.CompilerParams(dimension_semantics=("arbitrary",) * n_grid,
                                vmem_limit_bytes=VMEM_LIMIT_BYTES)


def _const_spec(shape):
    nd = len(shape)
    return pl.BlockSpec(shape, lambda *_: (0,) * nd, pipeline_mode=pl.Buffered(1))


def _row_spec(tm, width):
    return pl.BlockSpec((tm, width), lambda i: (i, 0))


def _row_tile(m):
    tm = min(ROW_TILE, m)
    assert m % tm == 0 and tm % SUBLANES == 0
    return tm


def _dot(a, b):
    return jnp.dot(a, b, preferred_element_type=F32)


def _dot_nt(a, b):
    return lax.dot_general(a, b, (((1,), (1,)), ((), ())), preferred_element_type=F32)


def _dot_tn(a, b):
    return lax.dot_general(a, b, (((0,), (0,)), ((), ())), preferred_element_type=F32)


def _rms(x, g):
    ms = jnp.mean(x * x, axis=-1, keepdims=True)
    return x * lax.rsqrt(ms + EPS) * g


def _silu(x):
    return x * jax.nn.sigmoid(x)


def _silu_tanh(x):
    half = 0.5 * x
    return half + half * jnp.tanh(half)


def _softplus(x):
    return jnp.maximum(x, 0.0) + jnp.log1p(jnp.exp(-jnp.abs(x)))


def _split_bf16(v, n):
    pieces = []
    r = v
    for _ in range(n):
        p = r.astype(BF16)
        pieces.append(p)
        r = r - p.astype(F32)
    return pieces


def _ffn_body(h, g, wg_ref, wu_ref, wd_ref):
    u = _rms(h, g).astype(BF16)
    a = _dot(u, wg_ref[...])
    b = _dot(u, wu_ref[...])
    act = (_silu(a) * b).astype(BF16)
    return h + 0.5 * _dot(act, wd_ref[...])


def _ffn_kernel(h_ref, g_ref, wg_ref, wu_ref, wd_ref, o_ref):
    o_ref[...] = _ffn_body(h_ref[...], g_ref[...], wg_ref, wu_ref, wd_ref)


def _ffn_ple_kernel(h_ref, p_ref, g_ref, wg_ref, wu_ref, wd_ref, gp_ref, wpi_ref, wpg_ref, o_ref):
    h2 = _ffn_body(h_ref[...], g_ref[...], wg_ref, wu_ref, wd_ref)
    gate = jax.nn.sigmoid(_dot(_rms(h2, gp_ref[...]).astype(BF16), wpg_ref[...]))
    o_ref[...] = h2 + _dot(p_ref[...].astype(BF16), wpi_ref[...]) * gate


def _layer_spec(w, layer):
    return pl.BlockSpec((None,) + w.shape[1:], lambda *_: (layer, 0, 0), pipeline_mode=pl.Buffered(1))


def _ffn(h, layer, g, wg, wu, wd):
    m, d = h.shape
    tm = _row_tile(m)
    return pl.pallas_call(
        _ffn_kernel,
        grid=(m // tm,),
        in_specs=[_row_spec(tm, d)] + [_layer_spec(w, layer) for w in (g, wg, wu, wd)],
        out_specs=_row_spec(tm, d),
        out_shape=jax.ShapeDtypeStruct((m, d), F32),
        compiler_params=_cparams(1),
    )(h, g, wg, wu, wd)


def _ffn_ple(h, p, layer, g, wg, wu, wd, gp, wpi, wpg):
    m, d = h.shape
    tm = _row_tile(m)
    params = (g, wg, wu, wd, gp, wpi, wpg)
    return pl.pallas_call(
        _ffn_ple_kernel,
        grid=(m // tm,),
        in_specs=[_row_spec(tm, d), pl.BlockSpec((None, tm, p.shape[2]), lambda i: (layer, i, 0))]
                 + [_layer_spec(w, layer) for w in params],
        out_specs=_row_spec(tm, d),
        out_shape=jax.ShapeDtypeStruct((m, d), F32),
        compiler_params=_cparams(1),
    )(h, p, *params)


def _inproj_kernel(h_ref, g_ref, w_ref, z_ref, xbc_ref, dt_ref):
    u = _rms(h_ref[...], g_ref[...]).astype(BF16)
    lo = 0
    for out in (z_ref, xbc_ref, dt_ref):
        hi = lo + out.shape[1]
        out[...] = _dot(u, w_ref[:, lo:hi])
        lo = hi


def _inproj(h, g, w, widths):
    m, d = h.shape
    tm = _row_tile(m)
    assert sum(widths) == w.shape[1] and all(x % LANES == 0 for x in widths[:-1])
    return pl.pallas_call(
        _inproj_kernel,
        grid=(m // tm,),
        in_specs=[_row_spec(tm, d), _const_spec(g.shape), _const_spec(w.shape)],
        out_specs=[_row_spec(tm, x) for x in widths],
        out_shape=[jax.ShapeDtypeStruct((m, x), F32) for x in widths],
        compiler_params=_cparams(1),
    )(h, g, w)


def _mamba_kernel(chunk, z_ref, xbc_ref, dt_ref, h_ref, cst_ref, sst_ref,
                  convw_ref, convb_ref, dtb_ref, alog_ref, dexp_ref, ng_ref, wout_ref,
                  blt_ref, e3_ref, i2_ref,
                  o_ref, cout_ref, sout_ref,
                  xpad, st, x_s, bc_s, ae_s, de_s, y_s, at_s, yn_s):
    t = pl.program_id(1)
    n_t = pl.num_programs(1)
    tb = xbc_ref.shape[0]
    d_inner = x_s.shape[1]
    gn = SSM_GROUPS * D_STATE
    n_chunks = tb // chunk
    gw = d_inner // SSM_GROUPS
    half = LANES // 2
    pad_rows = half - chunk

    @pl.when(t == 0)
    def _():
        xpad[0:SUBLANES, :] = cst_ref[0]
        for g in range(SSM_GROUPS):
            st[g] = sst_ref[0, g].T

    xpad[SUBLANES:SUBLANES + tb, :] = xbc_ref[...]
    w = convw_ref[...]
    base = SUBLANES - (CONV_W - 1)
    rb = min(CONV_ROWS, chunk)
    row8 = lax.broadcasted_iota(jnp.int32, (SUBLANES, xpad.shape[1]), 0)

    def conv_rows(r0):
        ext = xpad[r0:r0 + rb + SUBLANES, :]
        slabs = [ext[s:s + SUBLANES, :] for s in range(0, rb + SUBLANES, SUBLANES)]
        ys = [convb_ref[...] + sl * w[CONV_W - 1:CONV_W, :] for sl in slabs[1:]]
        for k in range(CONV_W - 1):
            back = CONV_W - 1 - k
            rolled = [pltpu.roll(sl, back, axis=0) for sl in slabs]
            for n in range(len(ys)):
                ys[n] = ys[n] + jnp.where(row8 < back, rolled[n], rolled[n + 1]) * w[k:k + 1, :]
        y = jnp.concatenate(ys, axis=0)
        act = _silu_tanh(y)
        x_s[r0:r0 + rb, :] = act[:, :d_inner]
        bc_s[r0:r0 + rb, :] = act[:, d_inner:]

    dt = _softplus(dt_ref[...] + dtb_ref[...])
    a = dt * (-jnp.exp(alog_ref[...]))
    blt = blt_ref[...]
    acs = sum(_dot(blt, p) for p in _split_bf16(a, 3))
    e3 = e3_ref[...]
    ae_s[...] = _dot(jnp.concatenate(_split_bf16(acs, 3), axis=1), e3)
    de_s[...] = _dot(jnp.concatenate(_split_bf16(dt, 3), axis=1), e3)
    i2 = i2_ref[...]
    at_all = sum(_dot_tn(p, i2) for p in _split_bf16(acs, 3))
    for c in range(n_chunks):
        at_s[c] = at_all[:, c * LANES:(c + 1) * LANES]

    lane = lax.broadcasted_iota(jnp.int32, (chunk, LANES), 1)
    row = lax.broadcasted_iota(jnp.int32, (chunk, LANES), 0)
    low_half = lane < half
    src = lane & (half - 1)
    causal = (src <= row) & (src < chunk)
    dexp = dexp_ref[...]

    def stack_pair(top, bot):
        if pad_rows == 0:
            return jnp.concatenate([top, bot], axis=0)
        zeros = jnp.zeros((pad_rows, LANES), top.dtype)
        return jnp.concatenate([top, zeros, bot, zeros], axis=0)

    def chunk_body(c, carry):
        rows = pl.ds(c * chunk, chunk)
        last_row = pl.ds(c * chunk + chunk - 1, 1)
        at2 = at_s[c]
        for g in range(SSM_GROUPS):
            gl = slice(g * gw, (g + 1) * gw)
            ae = ae_s[rows, gl]
            xc = x_s[rows, gl]
            last = ae_s[last_row, gl]
            xdt = xc * de_s[rows, gl]
            xw = (xdt * jnp.exp(last - ae)).astype(BF16)
            xdt_b = xdt.astype(BF16)
            c_g = bc_s[rows, gn + g * D_STATE:gn + (g + 1) * D_STATE].astype(BF16)
            b_g = bc_s[rows, g * D_STATE:(g + 1) * D_STATE].astype(BF16)
            cb2 = _dot_nt(c_g, stack_pair(b_g, b_g))
            s_g = st[g]
            y_off = _dot(c_g, s_g.astype(BF16)) * jnp.exp(ae)
            pairs = []
            for q in range(gw // LANES):
                pq = g * (gw // LANES) + q
                pl_ = slice(q * LANES, (q + 1) * LANES)
                rowvec = jnp.where(low_half[0:1, :], at2[2 * pq:2 * pq + 1, :], at2[2 * pq + 1:2 * pq + 2, :])
                seg = jnp.exp(jnp.where(causal, ae[:, pl_] - rowvec, -jnp.inf))
                wmat = (cb2 * seg).astype(BF16)
                xp = xdt_b[:, pl_]
                zero = jnp.zeros_like(xp)
                x2 = stack_pair(jnp.where(low_half, xp, zero), jnp.where(low_half, zero, xp))
                pairs.append(_dot(wmat, x2))
            y_diag = jnp.concatenate(pairs, axis=1)
            y_s[rows, gl] = y_diag + y_off + dexp[:, gl] * xc
            st[g] = s_g * jnp.exp(last) + _dot_tn(b_g, xw)
        return carry

    for c in range(n_chunks):
        for r0 in range(c * chunk, (c + 1) * chunk, rb):
            conv_rows(r0)
        chunk_body(c, 0)
        rows = pl.ds(c * chunk, chunk)
        yz = y_s[rows, :] * _silu_tanh(z_ref[rows, :])
        yn_s[rows, :] = _rms(yz, ng_ref[...]).astype(BF16)
    tail = xpad[tb:tb + SUBLANES, :]
    cout_ref[0] = tail
    xpad[0:SUBLANES, :] = tail
    o_ref[...] = h_ref[...] + _dot(yn_s[...], wout_ref[...])

    @pl.when(t == n_t - 1)
    def _():
        for g in range(SSM_GROUPS):
            sout_ref[0, g] = st[g].T


def _mamba(z, xbc, dt, h, conv_state8, ssm_state, convw, convb, dtb, alog, dexp, ng, wout, batch, seq):
    m, d_inner = z.shape
    conv_dim = xbc.shape[1]
    heads = dt.shape[1]
    d = h.shape[1]
    tb = min(MAMBA_TILE, seq)
    chunk = min(CHUNK, seq)
    assert seq % tb == 0 and tb % chunk == 0 and chunk % SUBLANES == 0 and LANES // 2 % chunk == 0
    n_t = seq // tb
    n_chunks = tb // chunk
    gw = d_inner // SSM_GROUPS

    r = np.arange(tb)
    blt = ((r[:, None] // chunk == r[None, :] // chunk) & (r[None, :] <= r[:, None])).astype(np.float32)
    e = (np.arange(d_inner)[None, :] // SSM_HEAD_DIM == np.arange(heads)[:, None]).astype(np.float32)
    e = np.concatenate([e, e, e], axis=0)
    col = np.arange(n_chunks * LANES)
    i2 = ((col[None, :] // LANES == r[:, None] // chunk)
          & (col[None, :] % (LANES // 2) == r[:, None] % chunk)).astype(np.float32)
    blt, e, i2 = (jnp.asarray(v, BF16) for v in (blt, e, i2))

    row_map = lambda b, t: (b * n_t + t, 0)
    const = lambda v: _const_spec(v.shape)
    return pl.pallas_call(
        functools.partial(_mamba_kernel, chunk),
        grid=(batch, n_t),
        in_specs=[pl.BlockSpec((tb, d_inner), row_map), pl.BlockSpec((tb, conv_dim), row_map),
                  pl.BlockSpec((tb, heads), row_map), pl.BlockSpec((tb, d), row_map),
                  pl.BlockSpec((1, SUBLANES, conv_dim), lambda b, t: (b, 0, 0)),
                  pl.BlockSpec((1, SSM_GROUPS, gw, D_STATE), lambda b, t: (b, 0, 0, 0)),
                  const(convw), const(convb), const(dtb), const(alog), const(dexp), const(ng), const(wout),
                  const(blt), const(e), const(i2)],
        out_specs=[pl.BlockSpec((tb, d), row_map),
                   pl.BlockSpec((1, SUBLANES, conv_dim), lambda b, t: (b, 0, 0)),
                   pl.BlockSpec((1, SSM_GROUPS, gw, D_STATE), lambda b, t: (b, 0, 0, 0))],
        out_shape=[jax.ShapeDtypeStruct((m, d), F32),
                   jax.ShapeDtypeStruct((batch, SUBLANES, conv_dim), F32),
                   jax.ShapeDtypeStruct((batch, SSM_GROUPS, gw, D_STATE), F32)],
        scratch_shapes=[pltpu.VMEM((tb + SUBLANES, conv_dim), F32),
                        pltpu.VMEM((SSM_GROUPS, D_STATE, gw), F32),
                        pltpu.VMEM((tb, d_inner), F32),
                        pltpu.VMEM((tb, conv_dim - d_inner), F32),
                        pltpu.VMEM((tb, d_inner), F32),
                        pltpu.VMEM((tb, d_inner), F32),
                        pltpu.VMEM((tb, d_inner), F32),
                        pltpu.VMEM((n_chunks, heads, LANES), F32),
                        pltpu.VMEM((tb, d_inner), BF16)],
        compiler_params=_cparams(2),
    )(z, xbc, dt, h, conv_state8, ssm_state, convw, convb, dtb, alog, dexp, ng, wout, blt, e, i2)


def _table_spec(tm, seq, width):
    if seq <= tm:
        return pl.BlockSpec((tm, width), lambda i: (0, 0))
    per = seq // tm
    return pl.BlockSpec((tm, width), lambda i: (i % per, 0))


def _tile_table(tab, tm, seq):
    return jnp.tile(tab, (tm // seq, 1)) if seq < tm else tab


def _kvside_body(h_ref, g_ref, wc_ref, wr_ref, wrs_ref, gc_ref, gr_ref, grs_ref, cos_ref, sin_ref):
    u = _rms(h_ref[...], g_ref[...]).astype(BF16)
    ckv = _rms(_dot(u, wc_ref[...]), gc_ref[...])
    r = _dot(u, wr_ref[...])
    rs = _dot(u, wrs_ref[...])
    inv = lax.rsqrt(jnp.mean(r * r, axis=-1, keepdims=True) + EPS)
    return ckv, r * inv * gr_ref[...] * cos_ref[...] + rs * inv * grs_ref[...] * sin_ref[...]


def _kvside_kernel(*refs):
    ckv_ref, kr_ref = refs[-2:]
    ckv_ref[...], kr_ref[...] = _kvside_body(*refs[:-2])


def _kvside(h, g, wc, wr, wrs, gc, gr, grs, cos32, sin32, seq):
    m, d = h.shape
    tm = _row_tile(m)
    assert tm % seq == 0 or seq % tm == 0
    cos32, sin32 = _tile_table(cos32, tm, seq), _tile_table(sin32, tm, seq)
    consts = (g, wc, wr, wrs, gc, gr, grs)
    return pl.pallas_call(
        _kvside_kernel,
        grid=(m // tm,),
        in_specs=[_row_spec(tm, d)] + [_const_spec(v.shape) for v in consts]
                 + [_table_spec(tm, seq, QK_ROPE)] * 2,
        out_specs=[_row_spec(tm, wc.shape[1]), _row_spec(tm, QK_ROPE)],
        out_shape=[jax.ShapeDtypeStruct((m, wc.shape[1]), F32), jax.ShapeDtypeStruct((m, QK_ROPE), F32)],
        compiler_params=_cparams(1),
    )(h, *consts, cos32, sin32)


def _group_mean_sq(x, bd_ref):
    bd = bd_ref[...]
    sq = (x * x).astype(BF16)
    return jnp.concatenate([_dot(sq[:, s:s + MXU_DIM], bd) for s in range(0, x.shape[1], MXU_DIM)], axis=1)


def _kv_expand_kernel(*refs):
    n_side = 10
    ckv_ref, kr_ref, k_ref, vt_ref = refs[-4:]
    ckv, kr = _kvside_body(*refs[:n_side])
    ckv_ref[...] = ckv
    kr_ref[...] = kr
    _expand_body(ckv, kr, *refs[n_side:-4], k_ref, vt_ref)


def _expand_body(ckv, kr, wk_ref, wvt_ref, bd_ref, gk_ref, place_ref, one_ref, k_ref, vt_ref):
    c = ckv.astype(BF16)
    k = _dot(c, wk_ref[...])
    kn = k * lax.rsqrt(_group_mean_sq(k, bd_ref) + EPS) * gk_ref[...]
    kr = _dot(kr.astype(BF16), place_ref[...])
    for hh in range(MLA_HEADS):
        hl = slice(hh * LANES, (hh + 1) * LANES)
        k_ref[:, hl] = (kn[:, hl] + kr).astype(BF16)
    vt = _dot_nt(wvt_ref[...], c)
    one = one_ref[...]
    tk = vt_ref.shape[3]
    for j in range(vt_ref.shape[1]):
        for t in range(0, tk, LANES):
            vt_ref[0, j, :, t:t + LANES] = (vt[:, j * tk + t:j * tk + t + LANES] + one).astype(BF16)


def _kv_expand(h, side, cos32, sin32, expand, seq, tk):
    m, d = h.shape
    tm = min(_row_tile(m), seq)
    assert seq % tm == 0 and tm % tk == 0
    per = seq // tm
    kv_lora = side[1].shape[1]
    vrows = expand[1].shape[0]
    width = MLA_HEADS * LANES
    return pl.pallas_call(
        _kv_expand_kernel,
        grid=(m // tm,),
        in_specs=[_row_spec(tm, d)] + [_const_spec(v.shape) for v in side]
                 + [_table_spec(tm, seq, QK_ROPE)] * 2 + [_const_spec(v.shape) for v in expand],
        out_specs=[_row_spec(tm, kv_lora), _row_spec(tm, QK_ROPE), _row_spec(tm, width),
                   pl.BlockSpec((1, tm // tk, vrows, tk), lambda i: (i // per, i % per, 0, 0))],
        out_shape=[jax.ShapeDtypeStruct((m, kv_lora), F32), jax.ShapeDtypeStruct((m, QK_ROPE), F32),
                   jax.ShapeDtypeStruct((m, width), BF16),
                   jax.ShapeDtypeStruct((m // seq, seq // tk, vrows, tk), BF16)],
        compiler_params=_cparams(1),
    )(h, *side, cos32, sin32, *expand)


def _q_kernel(h_ref, g_ref, wdq_ref, gq_ref, wq_ref, bd_ref, gh_ref, tab_ref, q_ref):
    u = _rms(h_ref[...], g_ref[...]).astype(BF16)
    cq = _rms(_dot(u, wdq_ref[...]), gq_ref[...]).astype(BF16)
    q = _dot(cq, wq_ref[...])
    qn = q * lax.rsqrt(_group_mean_sq(q, bd_ref) + EPS) * gh_ref[...]
    tab = tab_ref[...]
    for hh in range(MLA_HEADS):
        hl = slice(hh * LANES, (hh + 1) * LANES)
        q_ref[:, hl] = (qn[:, hl] * tab).astype(BF16)


def _queries(h, g, wdq, gq, wq, bd, gh, tab, seq):
    m, d = h.shape
    tm = _row_tile(m)
    assert tm % seq == 0 or seq % tm == 0
    tab = _tile_table(tab, tm, seq)
    consts = (g, wdq, gq, wq, bd, gh)
    width = MLA_HEADS * LANES
    return pl.pallas_call(
        _q_kernel,
        grid=(m // tm,),
        in_specs=[_row_spec(tm, d)] + [_const_spec(v.shape) for v in consts]
                 + [_table_spec(tm, seq, LANES)],
        out_specs=_row_spec(tm, width),
        out_shape=jax.ShapeDtypeStruct((m, width), BF16),
        compiler_params=_cparams(1),
    )(h, *consts, tab)


def _q_flat_kernel(h_ref, g_ref, wdq_ref, gq_ref, wn_ref, wr_ref, wrs_ref, bdn_ref, bdr_ref,
                   gn_ref, gr_ref, grs_ref, cos_ref, sin_ref, qn_ref, qr_ref):
    u = _rms(h_ref[...], g_ref[...]).astype(BF16)
    cq = _rms(_dot(u, wdq_ref[...]), gq_ref[...]).astype(BF16)
    qn = _dot(cq, wn_ref[...])
    qn_ref[...] = (qn * lax.rsqrt(_group_mean_sq(qn, bdn_ref) + EPS) * gn_ref[...]).astype(BF16)
    r = _dot(cq, wr_ref[...])
    rs = _dot(cq, wrs_ref[...])
    inv = lax.rsqrt(_group_mean_sq(r, bdr_ref) + EPS)
    qr_ref[...] = (r * inv * gr_ref[...] * cos_ref[...] + rs * inv * grs_ref[...] * sin_ref[...]).astype(BF16)


def _queries_flat(h, g, wdq, gq, wn, wr, wrs, bdn, bdr, gn, gr, grs, cos, sin, seq):
    m, d = h.shape
    tm = _row_tile(m)
    assert tm % seq == 0 or seq % tm == 0
    cos, sin = _tile_table(cos, tm, seq), _tile_table(sin, tm, seq)
    consts = (g, wdq, gq, wn, wr, wrs, bdn, bdr, gn, gr, grs)
    widths = (wn.shape[1], wr.shape[1])
    return pl.pallas_call(
        _q_flat_kernel,
        grid=(m // tm,),
        in_specs=[_row_spec(tm, d)] + [_const_spec(v.shape) for v in consts]
                 + [_table_spec(tm, seq, widths[1])] * 2,
        out_specs=[_row_spec(tm, w) for w in widths],
        out_shape=[jax.ShapeDtypeStruct((m, w), BF16) for w in widths],
        compiler_params=_cparams(1),
    )(h, *consts, cos, sin)


def _cache_attn_kernel(kb, qn_ref, qr_ref, ckv_ref, kr_ref, ckvn_ref, krn_ref, h_ref,
                       wk_ref, wv_ref, bd_ref, gk_ref, wo_ref, o_ref, s_s, v_s, acc_s):
    tq = qn_ref.shape[1]
    past = ckv_ref.shape[1]
    cols = MLA_HEADS * tq
    chunk_shift = CHUNK.bit_length() - 1
    assert CHUNK == 1 << chunk_shift and past % kb == 0

    qn = jnp.concatenate([qn_ref[0]] * MLA_HEADS, axis=0)
    tq_shift, nope_shift = tq.bit_length() - 1, QK_NOPE.bit_length() - 1
    assert tq == 1 << tq_shift and QK_NOPE == 1 << nope_shift
    r_head = lax.broadcasted_iota(jnp.int32, qn.shape, 0) >> tq_shift
    c_head = lax.broadcasted_iota(jnp.int32, qn.shape, 1) >> nope_shift
    q_bd = jnp.where(r_head == c_head, qn, jnp.zeros_like(qn))
    qr_all = qr_ref[0]
    q_rope = jnp.concatenate([qr_all[:, hh * QK_ROPE:(hh + 1) * QK_ROPE] for hh in range(MLA_HEADS)], axis=0)

    def expand(c, kr, rows):
        c = c.astype(BF16)
        k = _dot(c, wk_ref[...])
        kn = (k * lax.rsqrt(_group_mean_sq(k, bd_ref) + EPS) * gk_ref[...]).astype(BF16)
        s = _dot_nt(kn, q_bd) + _dot_nt(kr.astype(BF16), q_rope)
        v = _dot(c, wv_ref[...]).astype(BF16)
        v_s[rows, :] = jnp.concatenate([v, jnp.ones((v.shape[0], LANES), BF16)], axis=1)
        return s

    def expand_block(j, m_run):
        rows = pl.ds(pl.multiple_of(j * kb, kb), kb)
        s = expand(ckv_ref[0, rows, :], kr_ref[0, rows, :], rows)
        s_s[rows, :] = s
        return jnp.maximum(m_run, jnp.max(s, axis=0, keepdims=True))

    m_run = lax.fori_loop(0, past // kb, expand_block, jnp.full((1, cols), -jnp.inf, F32))
    new_rows = pl.ds(past, tq)
    s_new = expand(ckvn_ref[0], krn_ref[0], new_rows)
    k_chunk = (past + lax.broadcasted_iota(jnp.int32, (tq, cols), 0)) >> chunk_shift
    q_chunk = (past + (lax.broadcasted_iota(jnp.int32, (tq, cols), 1) & (tq - 1))) >> chunk_shift
    s_new = jnp.where(k_chunk <= q_chunk, s_new, -jnp.inf)
    m_all = jnp.maximum(m_run, jnp.max(s_new, axis=0, keepdims=True))

    acc_s[...] = _dot_tn(jnp.exp2(s_new - m_all).astype(BF16), v_s[new_rows, :])

    def pv_block(j, carry):
        rows = pl.ds(pl.multiple_of(j * kb, kb), kb)
        p = jnp.exp2(s_s[rows, :] - m_all).astype(BF16)
        acc_s[...] += _dot_tn(p, v_s[rows, :])
        return carry

    lax.fori_loop(0, past // kb, pv_block, 0)

    width = MLA_HEADS * V_HEAD
    low_half = lax.broadcasted_iota(jnp.int32, (tq, LANES), 1) < V_HEAD
    tiles = []
    for c in range(width // LANES):
        ev = acc_s[2 * c * tq:(2 * c + 1) * tq, :]
        od = acc_s[(2 * c + 1) * tq:(2 * c + 2) * tq, :]
        cl = slice(c * LANES, (c + 1) * LANES)
        tiles.append(jnp.where(low_half, ev[:, cl] / ev[:, width:], od[:, cl] / od[:, width:]))
    o = jnp.concatenate(tiles, axis=1).astype(BF16)
    o_ref[0] = h_ref[0] + _dot(o, wo_ref[...])


def _cache_attention(qn, qr, ckv, kr, ckv_new, kr_new, h, wk, wv, bd, gk, wo):
    batch, tq, _ = qn.shape
    past = ckv.shape[1]
    d = h.shape[2]
    kb = min(ROW_TILE, past)
    assert past % kb == 0 and tq % 16 == 0 and (MLA_HEADS * tq) % LANES == 0
    per = lambda a: pl.BlockSpec((1,) + a.shape[1:], lambda b: (b, 0, 0))
    consts = (wk, wv, bd, gk, wo)
    return pl.pallas_call(
        functools.partial(_cache_attn_kernel, kb),
        grid=(batch,),
        in_specs=[per(qn), per(qr), per(ckv), per(kr), per(ckv_new), per(kr_new), per(h)]
                 + [_const_spec(v.shape) for v in consts],
        out_specs=per(h),
        out_shape=jax.ShapeDtypeStruct(h.shape, F32),
        scratch_shapes=[pltpu.VMEM((past + tq, MLA_HEADS * tq), F32),
                        pltpu.VMEM((past + tq, MLA_HEADS * V_HEAD + LANES), BF16),
                        pltpu.VMEM((MLA_HEADS * tq, MLA_HEADS * V_HEAD + LANES), F32)],
        compiler_params=_cparams(1),
    )(qn, qr, ckv, kr, ckv_new, kr_new, h, *consts)


def _self_attn_kernel(q_ref, k_ref, vt_ref, h_ref, wo_ref, o_ref, m_s, acc_s, ot_s, st_s):
    i = pl.program_id(1)
    tq = q_ref.shape[1]
    chunk_shift = CHUNK.bit_length() - 1
    assert CHUNK == 1 << chunk_shift

    def tail_mask(n_tiles):
        key = lax.broadcasted_iota(jnp.int32, (n_tiles * tq, tq), 0) - (n_tiles - 1) * tq
        qry = lax.broadcasted_iota(jnp.int32, (n_tiles * tq, tq), 1)
        return (key >> chunk_shift) <= (qry >> chunk_shift)

    m_s[...] = jnp.full(m_s.shape, -jnp.inf, F32)
    acc_s[...] = jnp.zeros(acc_s.shape, F32)

    def step(j, n_tiles, masked):
        rows = n_tiles * tq
        k0 = pl.multiple_of(j * tq, tq)
        for hh in range(MLA_HEADS):
            hl = slice(hh * LANES, (hh + 1) * LANES)
            st_s[hh, 0:rows, :] = _dot_nt(k_ref[0, pl.ds(k0, rows), hl], q_ref[0, :, hl])
        for hh in range(MLA_HEADS):
            st = st_s[hh, 0:rows, :]
            if masked:
                st = jnp.where(tail_mask(n_tiles), st, -jnp.inf)
            m_old = m_s[hh]
            m_new = jnp.maximum(m_old, jnp.max(st, axis=0, keepdims=True))
            alpha = jnp.exp2(m_old - m_new)
            pt = jnp.exp2(st - m_new).astype(BF16)
            vrows = slice(hh * VT_ROWS, (hh + 1) * VT_ROWS)
            vt = jnp.concatenate([vt_ref[0, j + n, vrows, :] for n in range(n_tiles)], axis=1)
            acc_s[hh] = alpha * acc_s[hh] + _dot(vt, pt)
            m_s[hh] = m_new

    def pair_step(jj, carry):
        step(2 * jj, 2, False)
        return carry

    lax.fori_loop(0, i // 2, pair_step, 0)

    @pl.when(i % 2 == 1)
    def _():
        step(i - 1, 2, True)

    @pl.when(i % 2 == 0)
    def _():
        step(i, 1, True)

    for hh in range(MLA_HEADS):
        acc = acc_s[hh]
        ot_s[hh * VT_ROWS:(hh + 1) * VT_ROWS, :] = (acc / acc[V_HEAD:V_HEAD + 1, :]).astype(BF16)
    o_ref[0] = h_ref[0] + _dot_tn(ot_s[...], wo_ref[...])


def _self_attention(q, k, vt, h, wo, batch):
    width = q.shape[2]
    seq = q.shape[1]
    d = h.shape[2]
    tq = vt.shape[3]
    vrows = vt.shape[2]
    assert seq % tq == 0 and tq % CHUNK == 0 and vt.shape[1] * tq == seq and vrows == MLA_HEADS * VT_ROWS
    return pl.pallas_call(
        _self_attn_kernel,
        grid=(batch, seq // tq),
        in_specs=[pl.BlockSpec((1, tq, width), lambda b, i: (b, i, 0)),
                  pl.BlockSpec((1, seq, width), lambda b, i: (b, 0, 0)),
                  pl.BlockSpec((1, seq // tq, vrows, tq), lambda b, i: (b, 0, 0, 0)),
                  pl.BlockSpec((1, tq, d), lambda b, i: (b, i, 0)),
                  _const_spec(wo.shape)],
        out_specs=pl.BlockSpec((1, tq, d), lambda b, i: (b, i, 0)),
        out_shape=jax.ShapeDtypeStruct((batch, seq, d), F32),
        scratch_shapes=[pltpu.VMEM((MLA_HEADS, 1, tq), F32), pltpu.VMEM((MLA_HEADS, VT_ROWS, tq), F32),
                        pltpu.VMEM((vrows, tq), BF16), pltpu.VMEM((MLA_HEADS, 2 * tq, tq), F32)],
        compiler_params=_cparams(2),
    )(q, k, vt, h, wo)


def _head_pad_cols(w, per_head, start, count, offset=0):
    k = w.shape[0]
    w3 = w.reshape(k, MLA_HEADS, per_head)[:, :, start:start + count]
    out = jnp.zeros((k, MLA_HEADS, LANES), w.dtype)
    out = out.at[:, :, offset:offset + count].set(w3)
    return out.reshape(k, MLA_HEADS * LANES)


def _head_rows(w3):
    h, r, n = w3.shape
    return jnp.zeros((h, VT_ROWS, n), w3.dtype).at[:, :r].set(w3).reshape(h * VT_ROWS, n)


def _head_pad_vec(pieces):
    blk = jnp.zeros((LANES,), F32)
    for off, vec in pieces:
        blk = blk.at[off:off + vec.shape[0]].set(vec.astype(F32))
    return jnp.tile(blk, MLA_HEADS)[None, :]


def _block_diag_mean(groups):
    bd = np.zeros((MXU_DIM, MXU_DIM), np.float32)
    for tile in range(MXU_DIM // LANES):
        for off, size, *spread in groups:
            lo = tile * LANES + off
            bd[lo:lo + size, lo:lo + (spread[0] if spread else size)] = 1.0 / size
    return jnp.asarray(bd, BF16)


def _rope_tables(pos):
    half = QK_ROPE // 2
    inv = ROPE_BASE ** (-jnp.arange(half, dtype=F32) / half)
    ang = pos.astype(F32)[:, None] * inv[None, :]
    return jnp.cos(ang), jnp.sin(ang)


def kernel(x_prompt, x_sample, p_prompt, p_sample, state_conv, state_ssm, cache_kv_latent, cache_k_rope, ln_ffn1, w1_gate, w1_up, w1_down, ln_mix, ln_ffn2, w2_gate, w2_up, w2_down, ln_ple, w_ple_in, w_ple_gate, a_in_proj, a_conv_w, a_conv_b, a_dt_bias, a_A_log, a_D, a_norm, a_out_proj, ln_kv, w_kv_a, kv_norm, kr_norm, w_kv_b, k_norm, b_w_dq, b_q_norm, b_w_q_b, b_qn_nope, b_qn_rope, b_w_o):
    depth = ln_ffn1.shape[0]
    n_a = a_in_proj.shape[0]
    assert depth == 2 and n_a == 1 and b_w_dq.shape[0] == 1
    d_model = x_prompt.shape[-1]
    heads = a_dt_bias.shape[-1]
    d_inner = heads * SSM_HEAD_DIM
    conv_dim = a_conv_w.shape[-1]
    kv_lora = kv_norm.shape[0]
    half = QK_ROPE // 2
    bf = lambda w: w.astype(BF16)
    row = lambda v: v.astype(F32)[None, :]

    rows = lambda v: v.astype(F32)[:, None, :]
    ffn1 = (rows(ln_ffn1), bf(w1_gate), bf(w1_up), bf(w1_down))
    ffn2 = (rows(ln_ffn2), bf(w2_gate), bf(w2_up), bf(w2_down), rows(ln_ple), bf(w_ple_in), bf(w_ple_gate))
    mamba_w = dict(
        g=row(ln_mix[0]), w_in=bf(a_in_proj[0]), convw=a_conv_w[0], convb=row(a_conv_b[0]),
        dtb=row(a_dt_bias[0]), alog=row(a_A_log[0]), dexp=row(jnp.repeat(a_D[0], SSM_HEAD_DIM)),
        ng=row(a_norm[0]), wout=bf(a_out_proj[0]))

    swap = np.concatenate([np.arange(half, QK_ROPE), np.arange(half)])
    w_r = w_kv_a[:, kv_lora:]
    kv_w = dict(g=row(ln_kv), wc=bf(w_kv_a[:, :kv_lora]), wr=bf(w_r), wrs=bf(w_r[:, swap]),
                gc=row(kv_norm), gr=row(kr_norm), grs=row(kr_norm[swap]))
    per_kv = QK_NOPE + V_HEAD
    ex_w = dict(wk=bf(_head_pad_cols(w_kv_b, per_kv, 0, QK_NOPE)),
                wvt=bf(_head_rows(w_kv_b.reshape(kv_lora, MLA_HEADS, per_kv)[:, :, QK_NOPE:].transpose(1, 2, 0))),
                bd=_block_diag_mean([(0, QK_NOPE)]),
                gk=_head_pad_vec([(0, k_norm)]),
                place=jnp.asarray(np.eye(QK_ROPE, LANES, k=QK_NOPE) + np.eye(QK_ROPE, LANES, k=QK_NOPE + QK_ROPE), BF16),
                one=jnp.asarray(np.tile((np.arange(VT_ROWS) == V_HEAD).astype(np.float32)[:, None],
                                        (MLA_HEADS, LANES))))
    per_q = QK_NOPE + QK_ROPE
    wqb = b_w_q_b[0]
    wq = (_head_pad_cols(wqb, per_q, 0, per_q)
          + _head_pad_cols(wqb, per_q, QK_NOPE + half, half, offset=per_q)
          + _head_pad_cols(wqb, per_q, QK_NOPE, half, offset=per_q + half))
    gq_rope = b_qn_rope[0]
    q_w = dict(g=row(ln_mix[1]), wdq=bf(b_w_dq[0]), gq=row(b_q_norm[0]), wq=bf(wq),
               bd=_block_diag_mean([(0, QK_NOPE), (QK_NOPE, QK_ROPE, 2 * QK_ROPE)]),
               gh=_head_pad_vec([(0, b_qn_nope[0]), (QK_NOPE, gq_rope), (per_q, gq_rope[swap])]) * EXP2_SCALE)
    wo = bf(_head_rows(b_w_o[0].reshape(MLA_HEADS, V_HEAD, d_model)))

    wkv3 = w_kv_b.reshape(kv_lora, MLA_HEADS, per_kv)
    wq3 = wqb.reshape(wqb.shape[0], MLA_HEADS, per_q)
    flat = lambda w3: bf(w3.reshape(w3.shape[0], -1))
    per_head = lambda v: jnp.tile(v.astype(F32), MLA_HEADS)[None, :]
    flat_w = dict(
        wk=flat(wkv3[:, :, :QK_NOPE]), wv=flat(wkv3[:, :, QK_NOPE:]), gk=per_head(k_norm),
        wn=flat(wq3[:, :, :QK_NOPE]), wr=flat(wq3[:, :, QK_NOPE:]), wrs=flat(wq3[:, :, QK_NOPE:][:, :, swap]),
        bdn=_block_diag_mean([(o, QK_NOPE) for o in range(0, LANES, QK_NOPE)]),
        bdr=_block_diag_mean([(o, QK_ROPE) for o in range(0, LANES, QK_ROPE)]),
        gn=per_head(b_qn_nope[0]) * EXP2_SCALE, gr=per_head(gq_rope) * EXP2_SCALE,
        grs=per_head(gq_rope[swap]) * EXP2_SCALE, wo=bf(b_w_o[0]))

    def rope_tabs(pos):
        cos, sin = _rope_tables(pos)
        seq = pos.shape[0]
        cos32 = jnp.concatenate([cos, cos], axis=1)
        sin32 = jnp.concatenate([-sin, sin], axis=1)
        tab128 = jnp.concatenate([jnp.ones((seq, QK_NOPE), F32), cos32, sin32], axis=1)
        return cos32, sin32, tab128

    def run_group(x, p, pos0, conv_in, ssm_in, past_ckv, past_kr):
        batch, seq, _ = x.shape
        m = batch * seq
        pos = pos0 + jnp.arange(seq, dtype=jnp.int32)
        cos32, sin32, tab128 = rope_tabs(pos)
        h = x.reshape(m, d_model)
        p2 = p.reshape(depth, m, p.shape[-1])

        h = _ffn(h, 0, *ffn1)
        mw = mamba_w
        z, xbc, dt = _inproj(h, mw["g"], mw["w_in"], (d_inner, conv_dim, heads))
        conv8 = jnp.pad(conv_in, ((0, 0), (SUBLANES - (CONV_W - 1), 0), (0, 0)))
        ssm4 = ssm_in.reshape(batch, SSM_GROUPS, d_inner // SSM_GROUPS, D_STATE)
        h, conv_out, ssm_out = _mamba(z, xbc, dt, h, conv8, ssm4, mw["convw"], mw["convb"], mw["dtb"],
                                      mw["alog"], mw["dexp"], mw["ng"], mw["wout"], batch, seq)
        conv_out = conv_out[None, :, SUBLANES - (CONV_W - 1):, :]
        ssm_out = ssm_out.reshape(1, batch, heads, SSM_HEAD_DIM, D_STATE)
        h = _ffn_ple(h, p2, 0, *ffn2)

        kw = kv_w
        side = (kw["g"], kw["wc"], kw["wr"], kw["wrs"], kw["gc"], kw["gr"], kw["grs"])
        if past_ckv is None:
            ew = ex_w
            ckv_new, kr_new, k_all, vt_all = _kv_expand(
                h, side, cos32, sin32, (ew["wk"], ew["wvt"], ew["bd"], ew["gk"], ew["place"], ew["one"]),
                seq, min(ATTN_TILE, seq))
        else:
            ckv_new, kr_new = _kvside(h, *side, cos32, sin32, seq)

        h = _ffn(h, 1, *ffn1)
        qw = q_w
        if past_ckv is None:
            q = _queries(h, qw["g"], qw["wdq"], qw["gq"], qw["wq"], qw["bd"], qw["gh"], tab128, seq)
            width = MLA_HEADS * LANES
            h = _self_attention(q.reshape(batch, seq, width), k_all.reshape(batch, seq, width), vt_all,
                                h.reshape(batch, seq, d_model), wo, batch)
        else:
            fw = flat_w
            cos_f, sin_f = jnp.tile(cos32, (1, MLA_HEADS)), jnp.tile(sin32, (1, MLA_HEADS))
            qn, qr = _queries_flat(h, qw["g"], qw["wdq"], qw["gq"], fw["wn"], fw["wr"], fw["wrs"], fw["bdn"],
                                   fw["bdr"], fw["gn"], fw["gr"], fw["grs"], cos_f, sin_f, seq)
            h = _cache_attention(qn.reshape(batch, seq, -1), qr.reshape(batch, seq, -1), past_ckv, past_kr,
                                 ckv_new.reshape(batch, seq, kv_lora), kr_new.reshape(batch, seq, QK_ROPE),
                                 h.reshape(batch, seq, d_model), fw["wk"], fw["wv"], fw["bdn"], fw["gk"], fw["wo"])
        h = _ffn_ple(h.reshape(m, d_model), p2, 1, *ffn2)
        return (h.reshape(batch, seq, d_model), conv_out, ssm_out,
                ckv_new.reshape(batch, seq, kv_lora), kr_new.reshape(batch, seq, QK_ROPE))

    b_p = x_prompt.shape[0]
    conv0 = jnp.zeros((b_p, CONV_W - 1, conv_dim), F32)
    ssm0 = jnp.zeros((b_p, heads, SSM_HEAD_DIM, D_STATE), F32)
    y_p, conv_p, ssm_p, kv_p, kr_p = run_group(x_prompt, p_prompt, 0, conv0, ssm0, None, None)
    y_s, conv_s, ssm_s, kv_s, kr_s = run_group(x_sample, p_sample, cache_kv_latent.shape[1],
                                               state_conv[0], state_ssm[0], cache_kv_latent, cache_k_rope)
    return (y_p, y_s, conv_p, ssm_p, kv_p, kr_p, conv_s, ssm_s, kv_s, kr_s)
```
